```python
import math
import jax, jax.numpy as jnp
from jax import lax
import numpy as np

D_MODEL = 2048
BATCH = 1
SEQ = 8192
DEPTH = 4

CHUNK = 64
QBLOCK = 128
DA_HEADS = 8
DA_QK = 64
DA_V = 2 * DA_QK
MLA_HEADS = 8
MLA_Q_LORA = 512
MLA_KV_LORA = 256
MLA_NOPE = 128
MLA_ROPE = 64
MLA_V = 128
ROPE_BASE = 10000.0
SB_HEADS = 8
SB_DIM = 128
BRANCH_WIDTH = 1024
N_BRANCHES = 3
T5_BUCKETS = 32
T5_MAX_DIST = 128
D_FF = 4 * D_MODEL
N_MOD = 6
EPS = 1e-6
NEG_INF = -1e30

IN_SIZES = (
    DA_HEADS * 2 * DA_QK,
    DA_HEADS * 2 * DA_QK,
    DA_HEADS * DA_V,
    MLA_Q_LORA,
    MLA_KV_LORA,
    MLA_ROPE,
    SB_HEADS * SB_DIM,
    SB_HEADS * SB_DIM,
    SB_HEADS * SB_DIM,
    N_BRANCHES * D_MODEL,
)
IN_COLS = sum(IN_SIZES)

kernel_name = "hybrid_gated_diffattn_mla_stickbreak_block"


def _rms(x, g):
    xf = x.astype(jnp.float32)
    y = xf * lax.rsqrt(jnp.mean(xf * xf, axis=-1, keepdims=True) + EPS)
    return (y * g.astype(jnp.float32)).astype(x.dtype)


def _split_cols(proj):
    parts, o = [], 0
    for n in IN_SIZES:
        parts.append(proj[..., o:o + n])
        o += n
    return parts


def _sweep(block_fn, seq):
    out = lax.map(block_fn, jnp.arange(seq // QBLOCK))
    nb, b, q, h, e = out.shape
    return jnp.moveaxis(out, 0, 1).reshape(b, nb * q, h, e)


def _chunk_mask(qpos, kpos):
    return (kpos[None, :] // CHUNK) <= (qpos[:, None] // CHUNK)


def _t5_bucket(rel):
    nb = T5_BUCKETS // 2
    max_exact = nb // 2
    n = jnp.abs(rel)
    large = max_exact + (jnp.log(jnp.maximum(n, 1).astype(jnp.float32) / max_exact)
                         / math.log(T5_MAX_DIST / max_exact) * (nb - max_exact)).astype(jnp.int32)
    large = jnp.minimum(large, nb - 1)
    return jnp.where(rel > 0, nb, 0) + jnp.where(n < max_exact, n, large)


def _rope(x, pos):
    half = x.shape[-1] // 2
    inv = ROPE_BASE ** (-jnp.arange(half, dtype=jnp.float32) / half)
    ang = pos.astype(jnp.float32)[:, None] * inv[None, :]
    cos = jnp.cos(ang)[None, :, None, :]
    sin = jnp.sin(ang)[None, :, None, :]
    xf = x.astype(jnp.float32)
    x1, x2 = xf[..., :half], xf[..., half:]
    return jnp.concatenate([x1 * cos - x2 * sin, x1 * sin + x2 * cos], axis=-1).astype(x.dtype)


def _diff_attention(q, k, v, t5_bias, lam, lam_init, sub_g):
    seq = q.shape[1]
    kpos = jnp.arange(seq)
    scale = DA_QK ** -0.5
    table = t5_bias.astype(jnp.float32)

    def block(i):
        start = i * QBLOCK
        qb = lax.dynamic_slice_in_dim(q, start, QBLOCK, axis=1)
        qpos = start + jnp.arange(QBLOCK)
        logits = jnp.einsum('bqhcd,bkhcd->bhcqk', qb, k).astype(jnp.float32) * scale
        bias = table[_t5_bucket(kpos[None, :] - qpos[:, None])]
        logits = logits + jnp.transpose(bias, (2, 0, 1))[None, :, None]
        p = jax.nn.softmax(jnp.where(_chunk_mask(qpos, kpos), logits, NEG_INF), axis=-1)
        a = p[:, :, 0] - lam * p[:, :, 1]
        return jnp.einsum('bhqk,bkhe->bqhe', a.astype(v.dtype), v)

    o = _sweep(block, seq)
    return _rms(o, sub_g) * (1.0 - lam_init)


def _chunk_softmax_attention(q, k, v, scale):
    seq = q.shape[1]
    kpos = jnp.arange(seq)

    def block(i):
        start = i * QBLOCK
        qb = lax.dynamic_slice_in_dim(q, start, QBLOCK, axis=1)
        qpos = start + jnp.arange(QBLOCK)
        logits = jnp.einsum('bqhd,bkhd->bhqk', qb, k).astype(jnp.float32) * scale
        p = jax.nn.softmax(jnp.where(_chunk_mask(qpos, kpos), logits, NEG_INF), axis=-1)
        return jnp.einsum('bhqk,bkhe->bqhe', p.astype(v.dtype), v)

    return _sweep(block, seq)


def _mla(c_q, c_kv, k_pe, q_norm_g, kv_norm_g, w_q_up, w_kv_up, qk_g):
    b, s, _ = c_q.shape
    pos = jnp.arange(s)
    q = (_rms(c_q, q_norm_g) @ w_q_up).reshape(b, s, MLA_HEADS, MLA_NOPE + MLA_ROPE)
    kv = (_rms(c_kv, kv_norm_g) @ w_kv_up).reshape(b, s, MLA_HEADS, MLA_NOPE + MLA_V)
    v = kv[..., MLA_NOPE:]
    q_nope = _rms(q[..., :MLA_NOPE], qk_g[0, :MLA_NOPE])
    q_pe = _rope(_rms(q[..., MLA_NOPE:], qk_g[0, MLA_NOPE:]), pos)
    k_nope = _rms(kv[..., :MLA_NOPE], qk_g[1, :MLA_NOPE])
    k_pe = _rope(_rms(k_pe.reshape(b, s, 1, MLA_ROPE), qk_g[1, MLA_NOPE:]), pos)
    qh = jnp.concatenate([q_nope, q_pe], axis=-1)
    kh = jnp.concatenate([k_nope, jnp.broadcast_to(k_pe, (b, s, MLA_HEADS, MLA_ROPE))], axis=-1)
    return _chunk_softmax_attention(qh, kh, v, (MLA_NOPE + MLA_ROPE) ** -0.5)


def _stick_breaking(q, k, v):
    seq = q.shape[1]
    kpos = jnp.arange(seq)
    scale = SB_DIM ** -0.5

    def block(i):
        start = i * QBLOCK
        qb = lax.dynamic_slice_in_dim(q, start, QBLOCK, axis=1)
        qpos = start + jnp.arange(QBLOCK)
        z = jnp.einsum('bqhd,bkhd->bhqk', qb, k).astype(jnp.float32) * scale
        mask = kpos[None, :] < qpos[:, None]
        log_1m = jnp.where(mask, jax.nn.log_sigmoid(-z), 0.0)
        between = lax.cumsum(log_1m, axis=3, reverse=True) - log_1m
        w = jnp.where(mask, jnp.exp(jax.nn.log_sigmoid(z) + between), 0.0)
        return jnp.einsum('bhqk,bkhd->bqhd', w.astype(v.dtype), v)

    return _sweep(block, seq)


def setup_inputs(seed: int = 0) -> dict:
    key = jax.random.key(seed)
    ks = jax.random.split(key, 24)
    f32 = jnp.float32

    def nrm(k, shape, std):
        return jax.random.normal(k, shape, f32) * std

    def gain(k, shape):
        return 1.0 + 0.02 * jax.random.normal(k, shape, f32)

    return {
        "x": nrm(ks[0], (BATCH, SEQ, D_MODEL), 1.0),
        "c": nrm(ks[1], (BATCH, D_MODEL), 1.0),
        "w_ada": nrm(ks[2], (DEPTH, D_MODEL, N_MOD * D_MODEL), 0.3 * D_MODEL ** -0.5),
        "b_ada": nrm(ks[3], (DEPTH, N_MOD * D_MODEL), 0.02),
        "norm_mix_g": gain(ks[4], (DEPTH, D_MODEL)),
        "norm_mlp_g": gain(ks[5], (DEPTH, D_MODEL)),
        "w_in": nrm(ks[6], (DEPTH, D_MODEL, IN_COLS), D_MODEL ** -0.5),
        "diff_qk_g": gain(ks[7], (DEPTH, 2, DA_QK)),
        "diff_lambda": nrm(ks[8], (DEPTH, 4, DA_QK), 0.1),
        "diff_subln_g": gain(ks[9], (DEPTH, DA_V)),
        "t5_bias": nrm(ks[10], (T5_BUCKETS, DA_HEADS), 0.5),
        "mla_q_norm_g": gain(ks[11], (DEPTH, MLA_Q_LORA)),
        "mla_kv_norm_g": gain(ks[12], (DEPTH, MLA_KV_LORA)),
        "w_q_up": nrm(ks[13], (DEPTH, MLA_Q_LORA, MLA_HEADS * (MLA_NOPE + MLA_ROPE)), MLA_Q_LORA ** -0.5),
        "w_kv_up": nrm(ks[14], (DEPTH, MLA_KV_LORA, MLA_HEADS * (MLA_NOPE + MLA_V)), MLA_KV_LORA ** -0.5),
        "mla_qk_g": gain(ks[15], (DEPTH, 2, MLA_NOPE + MLA_ROPE)),
        "w_branch": nrm(ks[16], (DEPTH, N_BRANCHES, BRANCH_WIDTH, D_MODEL), BRANCH_WIDTH ** -0.5),
        "w_out": nrm(ks[17], (DEPTH, D_MODEL, D_MODEL), D_MODEL ** -0.5),
        "w_mlp_in": nrm(ks[18], (DEPTH, D_MODEL, D_FF), D_MODEL ** -0.5),
        "w_mlp_out": nrm(ks[19], (DEPTH, D_FF, D_MODEL), D_FF ** -0.5),
    }


def reference(x, c, w_ada, b_ada, norm_mix_g, norm_mlp_g, w_in, diff_qk_g, diff_lambda,
              diff_subln_g, t5_bias, mla_q_norm_g, mla_kv_norm_g, w_q_up, w_kv_up, mla_qk_g,
              w_branch, w_out, w_mlp_in, w_mlp_out):
    b, s, d = x.shape
    for l in range(DEPTH):
        mod = c @ w_ada[l] + b_ada[l]
        sh1, sc1, g1, sh2, sc2, g2 = jnp.split(mod, N_MOD, axis=-1)

        h = _rms(x, norm_mix_g[l]) * (1.0 + sc1[:, None]) + sh1[:, None]
        (da_q, da_k, da_v, mla_cq, mla_ckv, mla_kpe,
         sb_q, sb_k, sb_v, gates) = _split_cols(h @ w_in[l])

        lam_init = 0.8 - 0.6 * math.exp(-0.3 * l)
        lp = diff_lambda[l].astype(jnp.float32)
        lam = jnp.exp(jnp.sum(lp[0] * lp[1])) - jnp.exp(jnp.sum(lp[2] * lp[3])) + lam_init
        qa = _rms(da_q.reshape(b, s, DA_HEADS, 2, DA_QK), diff_qk_g[l, 0])
        ka = _rms(da_k.reshape(b, s, DA_HEADS, 2, DA_QK), diff_qk_g[l, 1])
        va = da_v.reshape(b, s, DA_HEADS, DA_V)
        ya = _diff_attention(qa, ka, va, t5_bias, lam, lam_init, diff_subln_g[l])

        yb = _mla(mla_cq, mla_ckv, mla_kpe, mla_q_norm_g[l], mla_kv_norm_g[l],
                  w_q_up[l], w_kv_up[l], mla_qk_g[l])

        yc = _stick_breaking(sb_q.reshape(b, s, SB_HEADS, SB_DIM),
                             sb_k.reshape(b, s, SB_HEADS, SB_DIM),
                             sb_v.reshape(b, s, SB_HEADS, SB_DIM))

        branches = jnp.stack([ya.reshape(b, s, BRANCH_WIDTH),
                              yb.reshape(b, s, BRANCH_WIDTH),
                              yc.reshape(b, s, BRANCH_WIDTH)], axis=2)
        up = jnp.einsum('bsne,ned->bsnd', branches, w_branch[l])
        gate = jax.nn.sigmoid(gates.reshape(b, s, N_BRANCHES, d).astype(jnp.float32)).astype(x.dtype)
        merged = jnp.sum(gate * up, axis=2)
        x = x + g1[:, None] * (merged @ w_out[l])

        h2 = _rms(x, norm_mlp_g[l]) * (1.0 + sc2[:, None]) + sh2[:, None]
        x = x + g2[:, None] * (jnp.square(jax.nn.relu(h2 @ w_mlp_in[l])) @ w_mlp_out[l])
    return x
```

```python
import functools
import math

import jax
import jax.numpy as jnp
from jax import lax
from jax.experimental import pallas as pl
from jax.experimental.pallas import tpu as pltpu

BF = jnp.bfloat16
F32 = jnp.float32

D_MODEL = 2048
CHUNK = 64
DA_HEADS = 8
DA_QK = 64
DA_V = 128
MLA_HEADS = 8
MLA_Q_LORA = 512
MLA_KV_LORA = 256
MLA_NOPE = 128
MLA_ROPE = 64
MLA_V = 128
MLA_QK_PAD = 256
ROPE_BASE = 10000.0
SB_HEADS = 8
SB_DIM = 128
BRANCH_WIDTH = 1024
N_BRANCHES = 3
T5_BUCKETS = 32
T5_MAX_DIST = 128
D_FF = 4 * D_MODEL
N_MOD = 6
EPS = 1e-6
NEG_INF = -1e30
LOG2E = 1.4426950408889634

HEAD_W = 128
ATT_TQ = 256
ATT_TK = 256
V7X_VMEM_LIMIT = 56 * 1024 * 1024


def _cparams(*sem):
    return pltpu.CompilerParams(dimension_semantics=sem, vmem_limit_bytes=V7X_VMEM_LIMIT)


def _mod_kernel(c_ref, w_ref, b_ref, o_ref):
    w = w_ref[0].astype(BF)
    o_ref[0] = jnp.dot(c_ref[...], w, preferred_element_type=F32) + b_ref[0]


def _mod_call(c8, w_ada, b_ada3):
    depth, d, n = w_ada.shape
    tn = 1024
    return pl.pallas_call(
        _mod_kernel,
        grid=(depth, n // tn),
        in_specs=[pl.BlockSpec((8, d), lambda l, j: (0, 0)),
                  pl.BlockSpec((1, d, tn), lambda l, j: (l, 0, j)),
                  pl.BlockSpec((1, 1, tn), lambda l, j: (l, 0, j))],
        out_specs=pl.BlockSpec((1, 8, tn), lambda l, j: (l, 0, j)),
        out_shape=jax.ShapeDtypeStruct((depth, 8, n), F32),
        compiler_params=_cparams("arbitrary", "arbitrary"),
        name="adaln_mod",
    )(c8, w_ada, b_ada3)


def _norm_kernel(x_ref, g_ref, sc_ref, sh_ref, h_ref, *maybe_ht_ref):
    x = x_ref[...]
    ms = jnp.mean(x * x, axis=-1, keepdims=True)
    y = x * lax.rsqrt(ms + EPS) * g_ref[...]
    h = y * (1.0 + sc_ref[...]) + sh_ref[...]
    h_ref[...] = h.astype(BF)
    if maybe_ht_ref:
        maybe_ht_ref[0][...] = h.T.astype(BF)


def _norm_call(x, g, sc, sh, with_transpose):
    s, d = x.shape
    tm = min(512, s)
    row = pl.BlockSpec((1, d), lambda i: (0, 0))
    out_shape = [jax.ShapeDtypeStruct((s, d), BF)]
    out_specs = [pl.BlockSpec((tm, d), lambda i: (i, 0))]
    if with_transpose:
        out_shape.append(jax.ShapeDtypeStruct((d, s), BF))
        out_specs.append(pl.BlockSpec((d, tm), lambda i: (0, i)))
    return pl.pallas_call(
        _norm_kernel,
        grid=(s // tm,),
        in_specs=[pl.BlockSpec((tm, d), lambda i: (i, 0)), row, row, row],
        out_specs=out_specs,
        out_shape=out_shape,
        compiler_params=_cparams("arbitrary"),
        name="adaln_rmsnorm",
    )(x, g, sc, sh)


def _proj_t_kernel(w_ref, xt_ref, rs_ref, o_ref, *, group, layout, tk):
    y = jnp.dot(w_ref[...], xt_ref[...], preferred_element_type=F32)
    tn, tm = y.shape
    if group:
        y3 = y.reshape(tn // group, group, tm)
        ms = jnp.mean(y3 * y3, axis=1, keepdims=True)
        y = (y3 * lax.rsqrt(ms + EPS)).reshape(tn, tm)
    y = y * rs_ref[...]
    if layout == "t":
        o_ref[...] = y.astype(o_ref.dtype)
    elif layout == "n":
        o_ref[...] = y.T.astype(o_ref.dtype)
    elif layout == "t4":
        for a in range(tn // HEAD_W):
            for b in range(tm // tk):
                o_ref[a, b] = y[a * HEAD_W:(a + 1) * HEAD_W, b * tk:(b + 1) * tk].astype(o_ref.dtype)
    elif layout == "qpad":
        first = lax.broadcasted_iota(jnp.int32, (HEAD_W, tm), 0) < DA_QK
        for a in range(tn // HEAD_W):
            ya = y[a * HEAD_W:(a + 1) * HEAD_W]
            o_ref[2 * a] = jnp.where(first, ya, 0.0).astype(o_ref.dtype)
            o_ref[2 * a + 1] = jnp.where(first, 0.0, ya).astype(o_ref.dtype)


def _proj_t_call(wt, xt, rowscale, *, group=0, layout="t", out_dtype=BF, name):
    n, k = wt.shape
    s = xt.shape[1]
    tn = min(1024, n)
    tm = min(1024, s)
    tk = min(ATT_TK, s)
    if layout == "t":
        out_shape = jax.ShapeDtypeStruct((n, s), out_dtype)
        out_spec = pl.BlockSpec((tn, tm), lambda i, j: (i, j))
    elif layout == "n":
        out_shape = jax.ShapeDtypeStruct((s, n), out_dtype)
        out_spec = pl.BlockSpec((tm, tn), lambda i, j: (j, i))
    elif layout == "t4":
        out_shape = jax.ShapeDtypeStruct((n // HEAD_W, s // tk, HEAD_W, tk), out_dtype)
        out_spec = pl.BlockSpec((tn // HEAD_W, tm // tk, HEAD_W, tk), lambda i, j: (i, j, 0, 0))
    else:
        out_shape = jax.ShapeDtypeStruct((2 * n // HEAD_W, HEAD_W, s), out_dtype)
        out_spec = pl.BlockSpec((2 * tn // HEAD_W, HEAD_W, tm), lambda i, j: (i, 0, j))
    return pl.pallas_call(
        functools.partial(_proj_t_kernel, group=group, layout=layout, tk=tk),
        grid=(n // tn, s // tm),
        in_specs=[pl.BlockSpec((tn, k), lambda i, j: (i, 0)),
                  pl.BlockSpec((k, tm), lambda i, j: (0, j)),
                  pl.BlockSpec((tn, 1), lambda i, j: (i, 0))],
        out_specs=out_spec,
        out_shape=out_shape,
        compiler_params=_cparams("arbitrary", "arbitrary"),
        name=name,
    )(wt, xt, rowscale)


def _proj_n_kernel(x_ref, w_ref, o_ref):
    o_ref[...] = jnp.dot(x_ref[...], w_ref[...], preferred_element_type=F32).astype(o_ref.dtype)


def _proj_n_call(x, w, *, out_dtype, name):
    s, k = x.shape
    n = w.shape[1]
    tm = min(1024, s)
    tn = min(1024, n)
    return pl.pallas_call(
        _proj_n_kernel,
        grid=(s // tm, n // tn),
        in_specs=[pl.BlockSpec((tm, k), lambda i, j: (i, 0)),
                  pl.BlockSpec((k, tn), lambda i, j: (0, j))],
        out_specs=pl.BlockSpec((tm, tn), lambda i, j: (i, j)),
        out_shape=jax.ShapeDtypeStruct((s, n), out_dtype),
        compiler_params=_cparams("arbitrary", "arbitrary"),
        name=name,
    )(x, w)


def _resid_proj_kernel(a_ref, w_ref, x_ref, g_ref, o_ref):
    y = jnp.dot(a_ref[...], w_ref[...], preferred_element_type=F32)
    o_ref[...] = x_ref[...] + g_ref[...] * y


def _resid_proj_call(a, w, x, g):
    s, k = a.shape
    n = w.shape[1]
    tm = min(1024, s)
    tn = min(1024, n)
    return pl.pallas_call(
        _resid_proj_kernel,
        grid=(s // tm, n // tn),
        in_specs=[pl.BlockSpec((tm, k), lambda i, j: (i, 0)),
                  pl.BlockSpec((k, tn), lambda i, j: (0, j)),
                  pl.BlockSpec((tm, tn), lambda i, j: (i, j)),
                  pl.BlockSpec((1, tn), lambda i, j: (0, j))],
        out_specs=pl.BlockSpec((tm, tn), lambda i, j: (i, j)),
        out_shape=jax.ShapeDtypeStruct((s, n), F32),
        compiler_params=_cparams("arbitrary", "arbitrary"),
        name="out_proj_residual",
    )(a, w, x, g)


def _mla_up_kernel(cb_ref, gq_ref, gkv_ref, wqt_ref, wk_ref, wvt_ref, gqn_ref, gqp_ref, gkn_ref,
                   gkp_ref, gkps_ref, cost_ref, sint_ref, cosf_ref, sinf_ref,
                   qt_ref, k_ref, vt_ref, *, tk):
    tm = cb_ref.shape[0]
    half = MLA_ROPE // 2
    cq = cb_ref[:, 0:MLA_Q_LORA]
    cqn = cq * lax.rsqrt(jnp.mean(cq * cq, axis=-1, keepdims=True) + EPS) * gq_ref[...]
    qt = jnp.dot(wqt_ref[...], cqn.T.astype(BF), preferred_element_type=F32)
    cos_t = cost_ref[...]
    sin_t = sint_ref[...]
    hw = MLA_NOPE + MLA_ROPE
    for h in range(MLA_HEADS):
        nope = qt[h * hw:h * hw + MLA_NOPE]
        nope = nope * lax.rsqrt(jnp.mean(nope * nope, axis=0, keepdims=True) + EPS) * gqn_ref[...]
        pe = qt[h * hw + MLA_NOPE:(h + 1) * hw]
        pe = pe * lax.rsqrt(jnp.mean(pe * pe, axis=0, keepdims=True) + EPS) * gqp_ref[...]
        x1 = pe[:half]
        x2 = pe[half:]
        qt_ref[h, 0:MLA_NOPE] = nope.astype(BF)
        qt_ref[h, MLA_NOPE:MLA_NOPE + half] = (x1 * cos_t - x2 * sin_t).astype(BF)
        qt_ref[h, MLA_NOPE + half:hw] = (x1 * sin_t + x2 * cos_t).astype(BF)
        qt_ref[h, hw:MLA_QK_PAD] = jnp.zeros((MLA_QK_PAD - hw, tm), BF)
    ckv = cb_ref[:, MLA_Q_LORA:MLA_Q_LORA + MLA_KV_LORA]
    ckvn = ckv * lax.rsqrt(jnp.mean(ckv * ckv, axis=-1, keepdims=True) + EPS) * gkv_ref[...]
    kn = jnp.dot(ckvn.astype(BF), wk_ref[...], preferred_element_type=F32)
    vt = jnp.dot(wvt_ref[...], ckvn.T.astype(BF), preferred_element_type=F32)
    o = MLA_Q_LORA + MLA_KV_LORA
    kpe = cb_ref[:, o:o + MLA_ROPE]
    kpe_rot = cb_ref[:, o + 128:o + 128 + MLA_ROPE]
    r = lax.rsqrt(jnp.mean(kpe * kpe, axis=-1, keepdims=True) + EPS)
    kpe_f = (r * (kpe * gkp_ref[...] * cosf_ref[...] + kpe_rot * gkps_ref[...] * sinf_ref[...])).astype(BF)
    zpad = jnp.zeros((tm, MLA_QK_PAD - hw), BF)
    for h in range(MLA_HEADS):
        kh = kn[:, h * MLA_NOPE:(h + 1) * MLA_NOPE]
        kh = kh * lax.rsqrt(jnp.mean(kh * kh, axis=-1, keepdims=True) + EPS) * gkn_ref[...]
        k_ref[h, :, 0:MLA_NOPE] = kh.astype(BF)
        k_ref[h, :, MLA_NOPE:hw] = kpe_f
        k_ref[h, :, hw:MLA_QK_PAD] = zpad
        for b in range(tm // tk):
            vt_ref[h, b] = vt[h * MLA_V:(h + 1) * MLA_V, b * tk:(b + 1) * tk].astype(BF)


def _mla_up_call(cb, gq, gkv, wqt, wk, wvt, gqn, gqp, gkn, gkp, gkps, cos_t, sin_t, cosf, sinf):
    s = cb.shape[0]
    tm = min(512, s)
    tk = min(ATT_TK, s)
    hw = MLA_NOPE + MLA_ROPE
    half = MLA_ROPE // 2

    def const(shape):
        return pl.BlockSpec(shape, lambda i: tuple(0 for _ in shape))

    return pl.pallas_call(
        functools.partial(_mla_up_kernel, tk=tk),
        grid=(s // tm,),
        in_specs=[pl.BlockSpec((tm, cb.shape[1]), lambda i: (i, 0)),
                  const((1, MLA_Q_LORA)), const((1, MLA_KV_LORA)),
                  const((MLA_HEADS * hw, MLA_Q_LORA)),
                  const((MLA_KV_LORA, MLA_HEADS * MLA_NOPE)),
                  const((MLA_HEADS * MLA_V, MLA_KV_LORA)),
                  const((MLA_NOPE, 1)), const((MLA_ROPE, 1)), const((1, MLA_NOPE)),
                  const((1, MLA_ROPE)), const((1, MLA_ROPE)),
                  pl.BlockSpec((half, tm), lambda i: (0, i)),
                  pl.BlockSpec((half, tm), lambda i: (0, i)),
                  pl.BlockSpec((tm, MLA_ROPE), lambda i: (i, 0)),
                  pl.BlockSpec((tm, MLA_ROPE), lambda i: (i, 0))],
        out_specs=[pl.BlockSpec((MLA_HEADS, MLA_QK_PAD, tm), lambda i: (0, 0, i)),
                   pl.BlockSpec((MLA_HEADS, tm, MLA_QK_PAD), lambda i: (0, i, 0)),
                   pl.BlockSpec((MLA_HEADS, tm // tk, MLA_V, tk), lambda i: (0, i, 0, 0))],
        out_shape=[jax.ShapeDtypeStruct((MLA_HEADS, MLA_QK_PAD, s), BF),
                   jax.ShapeDtypeStruct((MLA_HEADS, s, MLA_QK_PAD), BF),
                   jax.ShapeDtypeStruct((MLA_HEADS, s // tk, MLA_V, tk), BF)],
        compiler_params=_cparams("arbitrary"),
        name="mla_up",
    )(cb, gq, gkv, wqt, wk, wvt, gqn, gqp, gkn, gkp, gkps, cos_t, sin_t, cosf, sinf)


def _softmax_block(kb, bias, qt, k_of, vt_ref, m_ref, l_ref, acc_ref):
    s = jnp.dot(k_of(kb), qt, preferred_element_type=F32)
    if bias is not None:
        s = s + bias
    m_old = m_ref[...]
    m_new = jnp.maximum(m_old, jnp.max(s, axis=0, keepdims=True))
    alpha = jnp.exp2(m_old - m_new)
    p = jnp.exp2(s - m_new)
    l_ref[...] = alpha * l_ref[...] + jnp.sum(p, axis=0, keepdims=True)
    acc_ref[...] = alpha * acc_ref[...] + jnp.dot(vt_ref[0, kb], p.astype(BF),
                                                  preferred_element_type=F32)
    m_ref[...] = m_new


def _diff_attn_kernel(q_ref, k_ref, vt_ref, bd_ref, bp_ref, lam_ref, g_ref, o_ref,
                      m_ref, l_ref, acc_ref, *, tq, tk, lam_init):
    qb = pl.program_id(1)
    qt = jnp.concatenate([q_ref[0], q_ref[1]], axis=1)

    def k_of(kb):
        return k_ref[pl.ds(pl.multiple_of(kb * tk, tk), tk), :]

    m_ref[...] = jnp.full(m_ref.shape, NEG_INF, F32)
    l_ref[...] = jnp.zeros(l_ref.shape, F32)
    acc_ref[...] = jnp.zeros(acc_ref.shape, F32)

    def far(kb, carry):
        _softmax_block(kb, None, qt, k_of, vt_ref, m_ref, l_ref, acc_ref)
        return carry

    lax.fori_loop(0, jnp.maximum(qb - 1, 0), far, 0)

    @pl.when(qb >= 1)
    def _():
        _softmax_block(qb - 1, bp_ref[0], qt, k_of, vt_ref, m_ref, l_ref, acc_ref)

    _softmax_block(qb, bd_ref[0], qt, k_of, vt_ref, m_ref, l_ref, acc_ref)

    o = acc_ref[...] / l_ref[...]
    lp = lam_ref[...]
    lam = (jnp.exp(jnp.sum(lp[0:1] * lp[1:2], axis=1, keepdims=True))
           - jnp.exp(jnp.sum(lp[2:3] * lp[3:4], axis=1, keepdims=True)) + lam_init)
    d = o[:, :tq] - lam * o[:, tq:]
    ms = jnp.mean(d * d, axis=0, keepdims=True)
    d = d * lax.rsqrt(ms + EPS) * g_ref[...] * (1.0 - lam_init)
    o_ref[...] = d.T.astype(o_ref.dtype)


def _diff_attn_call(qpad, k, vt4, bias_diag, bias_prev, lam_p, sub_g, *, head0, lam_init):
    s = k.shape[0]
    tq = tk = min(ATT_TQ, s)
    nkb = s // tk
    n = 2 * tq
    return pl.pallas_call(
        functools.partial(_diff_attn_kernel, tq=tq, tk=tk, lam_init=lam_init),
        grid=(DA_HEADS, s // tq),
        in_specs=[pl.BlockSpec((2, HEAD_W, tq), lambda h, i: (h, 0, i)),
                  pl.BlockSpec((s, HEAD_W), lambda h, i: (0, h)),
                  pl.BlockSpec((1, nkb, HEAD_W, tk), lambda h, i: (head0 + h, 0, 0, 0)),
                  pl.BlockSpec((1, tk, n), lambda h, i: (h, 0, 0)),
                  pl.BlockSpec((1, tk, n), lambda h, i: (h, 0, 0)),
                  pl.BlockSpec((4, DA_QK), lambda h, i: (0, 0)),
                  pl.BlockSpec((HEAD_W, 1), lambda h, i: (0, 0))],
        out_specs=pl.BlockSpec((tq, HEAD_W), lambda h, i: (i, h)),
        out_shape=jax.ShapeDtypeStruct((s, DA_HEADS * HEAD_W), BF),
        scratch_shapes=[pltpu.VMEM((1, n), F32), pltpu.VMEM((1, n), F32),
                        pltpu.VMEM((HEAD_W, n), F32)],
        compiler_params=_cparams("arbitrary", "arbitrary"),
        name="diff_attention",
    )(qpad, k, vt4, bias_diag, bias_prev, lam_p, sub_g)


def _mla_attn_kernel(q_ref, k_ref, vt_ref, mask_ref, o_ref, m_ref, l_ref, acc_ref, *, tk):
    qb = pl.program_id(1)
    qt = q_ref[0]

    def k_of(kb):
        return k_ref[0, pl.ds(pl.multiple_of(kb * tk, tk), tk), :]

    m_ref[...] = jnp.full(m_ref.shape, NEG_INF, F32)
    l_ref[...] = jnp.zeros(l_ref.shape, F32)
    acc_ref[...] = jnp.zeros(acc_ref.shape, F32)

    def far(kb, carry):
        _softmax_block(kb, None, qt, k_of, vt_ref, m_ref, l_ref, acc_ref)
        return carry

    lax.fori_loop(0, qb, far, 0)
    _softmax_block(qb, mask_ref[...], qt, k_of, vt_ref, m_ref, l_ref, acc_ref)
    o = acc_ref[...] / l_ref[...]
    o_ref[...] = o.T.astype(o_ref.dtype)


def _mla_attn_call(qt, k, vt4, mask_diag):
    s = k.shape[1]
    tq = tk = min(ATT_TQ, s)
    nkb = s // tk
    return pl.pallas_call(
        functools.partial(_mla_attn_kernel, tk=tk),
        grid=(MLA_HEADS, s // tq),
        in_specs=[pl.BlockSpec((1, MLA_QK_PAD, tq), lambda h, i: (h, 0, i)),
                  pl.BlockSpec((1, s, MLA_QK_PAD), lambda h, i: (h, 0, 0)),
                  pl.BlockSpec((1, nkb, HEAD_W, tk), lambda h, i: (h, 0, 0, 0)),
                  pl.BlockSpec((tk, tq), lambda h, i: (0, 0))],
        out_specs=pl.BlockSpec((tq, HEAD_W), lambda h, i: (i, h)),
        out_shape=jax.ShapeDtypeStruct((s, MLA_HEADS * HEAD_W), BF),
        scratch_shapes=[pltpu.VMEM((1, tq), F32), pltpu.VMEM((1, tq), F32),
                        pltpu.VMEM((HEAD_W, tq), F32)],
        compiler_params=_cparams("arbitrary", "arbitrary"),
        name="mla_attention",
    )(qt, k, vt4, mask_diag)


def _stick_block(kb, mask, qt, k_ref, vt_ref, tri, carry_ref, acc_ref, tk):
    k = k_ref[pl.ds(pl.multiple_of(kb * tk, tk), tk), :]
    z = jnp.dot(k, qt, preferred_element_type=F32)
    sp = jnp.maximum(z, 0.0) + jnp.log1p(jnp.exp(-jnp.abs(z)))
    log1m = -sp
    if mask is not None:
        log1m = jnp.where(mask, log1m, 0.0)
    hi = log1m.astype(BF)
    lo = (log1m - hi.astype(F32)).astype(BF)
    between = (jnp.dot(tri, hi, preferred_element_type=F32)
               + jnp.dot(tri, lo, preferred_element_type=F32))
    carry = carry_ref[...]
    w = jnp.exp((z - sp) + between + carry)
    if mask is not None:
        w = jnp.where(mask, w, 0.0)
    acc_ref[...] += jnp.dot(vt_ref[0, kb], w.astype(BF), preferred_element_type=F32)
    carry_ref[...] = carry + jnp.sum(log1m, axis=0, keepdims=True)


def _stick_kernel(q_ref, k_ref, vt_ref, tri_ref, o_ref, carry_ref, acc_ref, *, tq, tk):
    qb = pl.program_id(1)
    qt = q_ref[...]
    tri = tri_ref[...]
    carry_ref[...] = jnp.zeros(carry_ref.shape, F32)
    acc_ref[...] = jnp.zeros(acc_ref.shape, F32)
    kpos = lax.broadcasted_iota(jnp.int32, (tk, tq), 0)
    qpos = lax.broadcasted_iota(jnp.int32, (tk, tq), 1)
    _stick_block(qb, kpos < qpos, qt, k_ref, vt_ref, tri, carry_ref, acc_ref, tk)

    def earlier(i, c):
        _stick_block(qb - 1 - i, None, qt, k_ref, vt_ref, tri, carry_ref, acc_ref, tk)
        return c

    lax.fori_loop(0, qb, earlier, 0)
    o_ref[...] = acc_ref[...].T.astype(o_ref.dtype)


def _stick_call(qt, k, vt4, tri, *, head0):
    s = k.shape[0]
    tq = tk = min(ATT_TQ, s)
    nkb = s // tk
    return pl.pallas_call(
        functools.partial(_stick_kernel, tq=tq, tk=tk),
        grid=(SB_HEADS, s // tq),
        in_specs=[pl.BlockSpec((HEAD_W, tq), lambda h, i: (h, i)),
                  pl.BlockSpec((s, HEAD_W), lambda h, i: (0, h)),
                  pl.BlockSpec((1, nkb, HEAD_W, tk), lambda h, i: (head0 + h, 0, 0, 0)),
                  pl.BlockSpec((tk, tk), lambda h, i: (0, 0))],
        out_specs=pl.BlockSpec((tq, HEAD_W), lambda h, i: (i, h)),
        out_shape=jax.ShapeDtypeStruct((s, SB_HEADS * HEAD_W), BF),
        scratch_shapes=[pltpu.VMEM((1, tq), F32), pltpu.VMEM((HEAD_W, tq), F32)],
        compiler_params=_cparams("arbitrary", "arbitrary"),
        name="stick_breaking",
    )(qt, k, vt4, tri)


def _merge_kernel(h_ref, ya_ref, yb_ref, yc_ref, wg_ref, wb_ref, o_ref):
    h = h_ref[...]
    acc = None
    for n, y_ref in enumerate((ya_ref, yb_ref, yc_ref)):
        gate = jax.nn.sigmoid(jnp.dot(h, wg_ref[n], preferred_element_type=F32))
        up = jnp.dot(y_ref[...], wb_ref[n], preferred_element_type=F32)
        acc = gate * up if acc is None else acc + gate * up
    o_ref[...] = acc.astype(o_ref.dtype)


def _merge_call(h, ya, yb, yc, wg, wb):
    s, d = h.shape
    tm = min(512, s)
    tn = 512
    ysp = pl.BlockSpec((tm, BRANCH_WIDTH), lambda i, j: (i, 0))
    return pl.pallas_call(
        _merge_kernel,
        grid=(s // tm, d // tn),
        in_specs=[pl.BlockSpec((tm, d), lambda i, j: (i, 0)), ysp, ysp, ysp,
                  pl.BlockSpec((N_BRANCHES, d, tn), lambda i, j: (0, 0, j)),
                  pl.BlockSpec((N_BRANCHES, BRANCH_WIDTH, tn), lambda i, j: (0, 0, j))],
        out_specs=pl.BlockSpec((tm, tn), lambda i, j: (i, j)),
        out_shape=jax.ShapeDtypeStruct((s, d), BF),
        compiler_params=_cparams("arbitrary", "arbitrary"),
        name="gated_merge",
    )(h, ya, yb, yc, wg, wb)


def _mlp_kernel(h_ref, w1_ref, w2_ref, x_ref, g_ref, o_ref, acc_ref):
    f = pl.program_id(1)

    @pl.when(f == 0)
    def _():
        acc_ref[...] = jnp.zeros(acc_ref.shape, F32)

    u = jnp.dot(h_ref[...], w1_ref[...], preferred_element_type=F32)
    u = jnp.square(jnp.maximum(u, 0.0)).astype(BF)
    acc_ref[...] += jnp.dot(u, w2_ref[...], preferred_element_type=F32)

    @pl.when(f == pl.num_programs(1) - 1)
    def _():
        o_ref[...] = x_ref[...] + g_ref[...] * acc_ref[...]


def _mlp_call(h, w1, w2, x, g):
    s, d = h.shape
    ff = w1.shape[1]
    tm = min(512, s)
    tf = 1024
    return pl.pallas_call(
        _mlp_kernel,
        grid=(s // tm, ff // tf),
        in_specs=[pl.BlockSpec((tm, d), lambda i, f: (i, 0)),
                  pl.BlockSpec((d, tf), lambda i, f: (0, f)),
                  pl.BlockSpec((tf, d), lambda i, f: (f, 0)),
                  pl.BlockSpec((tm, d), lambda i, f: (i, 0)),
                  pl.BlockSpec((1, d), lambda i, f: (0, 0))],
        out_specs=pl.BlockSpec((tm, d), lambda i, f: (i, 0)),
        out_shape=jax.ShapeDtypeStruct((s, d), F32),
        scratch_shapes=[pltpu.VMEM((tm, d), F32)],
        compiler_params=_cparams("arbitrary", "arbitrary"),
        name="sqrelu_mlp",
    )(h, w1, w2, x, g)


def _t5_bucket(rel):
    nb = T5_BUCKETS // 2
    max_exact = nb // 2
    n = jnp.abs(rel)
    large = max_exact + (jnp.log(jnp.maximum(n, 1).astype(F32) / max_exact)
                         / math.log(T5_MAX_DIST / max_exact) * (nb - max_exact)).astype(jnp.int32)
    large = jnp.minimum(large, nb - 1)
    return jnp.where(rel > 0, nb, 0) + jnp.where(n < max_exact, n, large)


def _bias_tiles(t5_bias, tq, tk):
    table = t5_bias.astype(F32)
    kl = jnp.arange(tk)[:, None]
    ql = jnp.arange(tq)[None, :]
    far = table[_t5_bucket(jnp.array(-(tk + tq), jnp.int32))]
    diag = jnp.transpose(table[_t5_bucket(kl - ql)], (2, 0, 1)) - far[:, None, None]
    prev = jnp.transpose(table[_t5_bucket(kl - tk - ql)], (2, 0, 1)) - far[:, None, None]
    allowed = (kl // CHUNK) <= (ql // CHUNK)
    diag = jnp.where(allowed[None], diag * LOG2E, NEG_INF)
    prev = prev * LOG2E
    return (jnp.concatenate([diag, diag], axis=2), jnp.concatenate([prev, prev], axis=2))


def _rope_tables(s):
    half = MLA_ROPE // 2
    inv = ROPE_BASE ** (-jnp.arange(half, dtype=F32) / half)
    ang = jnp.arange(s).astype(F32)[:, None] * inv[None, :]
    cos, sin = jnp.cos(ang), jnp.sin(ang)
    return (cos.T, sin.T, jnp.concatenate([cos, cos], axis=1), jnp.concatenate([sin, sin], axis=1))


def kernel(x, c, w_ada, b_ada, norm_mix_g, norm_mlp_g, w_in, diff_qk_g, diff_lambda, diff_subln_g,
           t5_bias, mla_q_norm_g, mla_kv_norm_g, w_q_up, w_kv_up, mla_qk_g, w_branch, w_out,
           w_mlp_in, w_mlp_out):
    b, s, d = x.shape
    assert b == 1 and d == D_MODEL
    depth = w_ada.shape[0]
    tq = tk = min(ATT_TQ, s)
    hw = MLA_NOPE + MLA_ROPE
    half = MLA_ROPE // 2

    mod = _mod_call(jnp.broadcast_to(c, (8, d)).astype(BF), w_ada, b_ada[:, None, :])[:, 0, :]
    mod = mod.reshape(depth, N_MOD, 1, d)

    o_q, o_k, o_v = 0, 1024, 2048
    o_cq, o_ckv, o_kpe = 3072, 3584, 3840
    o_sq, o_sk, o_sv, o_g = 3904, 4928, 5952, 6976

    def tr(w):
        return jnp.swapaxes(w, 1, 2).astype(BF)

    wt_qa = tr(w_in[:, :, o_q:o_k])
    wt_ka = tr(w_in[:, :, o_k:o_v])
    wt_v = tr(jnp.concatenate([w_in[:, :, o_v:o_cq], w_in[:, :, o_sv:o_g]], axis=2))
    wt_qc = tr(w_in[:, :, o_sq:o_sk])
    w_kc = w_in[:, :, o_sk:o_sv].astype(BF)
    w_kpe = w_in[:, :, o_kpe:o_sq]
    w_kpe_rot = jnp.concatenate([-w_kpe[:, :, half:], w_kpe[:, :, :half]], axis=2)
    zcol = jnp.zeros((depth, d, 128 - MLA_ROPE), w_in.dtype)
    w_b = jnp.concatenate([w_in[:, :, o_cq:o_kpe], w_kpe, zcol, w_kpe_rot, zcol], axis=2).astype(BF)
    w_gate = jnp.transpose(w_in[:, :, o_g:].reshape(depth, d, N_BRANCHES, d), (0, 2, 1, 3)).astype(BF)
    w_br = w_branch.astype(BF)
    w_o = w_out.astype(BF)
    w_1 = w_mlp_in.astype(BF)
    w_2 = w_mlp_out.astype(BF)
    wqt = tr(w_q_up)
    wkv = w_kv_up.reshape(depth, MLA_KV_LORA, MLA_HEADS, MLA_NOPE + MLA_V)
    w_kn = wkv[..., :MLA_NOPE].reshape(depth, MLA_KV_LORA, MLA_HEADS * MLA_NOPE).astype(BF)
    wt_vb = tr(wkv[..., MLA_NOPE:].reshape(depth, MLA_KV_LORA, MLA_HEADS * MLA_V))

    bias_diag, bias_prev = _bias_tiles(t5_bias, tq, tk)
    kl = jnp.arange(tk)[:, None]
    ql = jnp.arange(tq)[None, :]
    mask_diag = jnp.where((kl // CHUNK) <= (ql // CHUNK), 0.0, NEG_INF).astype(F32)
    tri = (jnp.arange(tk)[None, :] > jnp.arange(tk)[:, None]).astype(BF)
    cos_t, sin_t, cosf, sinf = _rope_tables(s)

    ones_v = jnp.ones((2 * BRANCH_WIDTH, 1), F32)
    sb_scale = jnp.full((SB_HEADS * SB_DIM, 1), SB_DIM ** -0.5, F32)
    da_scale = DA_QK ** -0.5 * LOG2E
    mla_scale = hw ** -0.5 * LOG2E

    x2 = x[0]
    for l in range(depth):
        sh1, sc1, g1, sh2, sc2, g2 = (mod[l, i] for i in range(N_MOD))
        lam_init = 0.8 - 0.6 * math.exp(-0.3 * l)

        h, ht = _norm_call(x2, norm_mix_g[l][None], sc1, sh1, True)

        gq_rows = jnp.tile(diff_qk_g[l, 0], 2 * DA_HEADS)[:, None] * da_scale
        gk_rows = jnp.tile(diff_qk_g[l, 1], 2 * DA_HEADS)[:, None]
        qa = _proj_t_call(wt_qa[l], ht, gq_rows, group=DA_QK, layout="qpad", name="proj_diff_q")
        ka = _proj_t_call(wt_ka[l], ht, gk_rows, group=DA_QK, layout="n", name="proj_diff_k")
        vt = _proj_t_call(wt_v[l], ht, ones_v, layout="t4", name="proj_v")
        qc = _proj_t_call(wt_qc[l], ht, sb_scale, layout="t", name="proj_sb_q")
        kc = _proj_n_call(h, w_kc[l], out_dtype=BF, name="proj_sb_k")
        cb = _proj_n_call(h, w_b[l], out_dtype=F32, name="proj_mla_latent")
        gqk = mla_qk_g[l]
        qb_t, kb, vb_t = _mla_up_call(
            cb, mla_q_norm_g[l][None], mla_kv_norm_g[l][None], wqt[l], w_kn[l], wt_vb[l],
            gqk[0, :MLA_NOPE, None] * mla_scale, gqk[0, MLA_NOPE:, None] * mla_scale,
            gqk[1, None, :MLA_NOPE], gqk[1, None, MLA_NOPE:],
            jnp.concatenate([gqk[1, MLA_NOPE + half:], gqk[1, MLA_NOPE:MLA_NOPE + half]])[None],
            cos_t, sin_t, cosf, sinf)

        ya = _diff_attn_call(qa, ka, vt, bias_diag, bias_prev, diff_lambda[l],
                             diff_subln_g[l][:, None], head0=0, lam_init=lam_init)
        yb = _mla_attn_call(qb_t, kb, vb_t, mask_diag)
        yc = _stick_call(qc, kc, vt, tri, head0=DA_HEADS)

        merged = _merge_call(h, ya, yb, yc, w_gate[l], w_br[l])
        x2 = _resid_proj_call(merged, w_o[l], x2, g1)

        h2 = _norm_call(x2, norm_mlp_g[l][None], sc2, sh2, False)[0]
        x2 = _mlp_call(h2, w_1[l], w_2[l], x2, g2)
    return x2[None]
```

```python
import functools
import math

import jax
import jax.numpy as jnp
from jax import lax
from jax.experimental import pallas as pl
from jax.experimental.pallas import tpu as pltpu

BF = jnp.bfloat16
F32 = jnp.float32

D_MODEL = 2048
CHUNK = 64
DA_HEADS = 8
DA_QK = 64
DA_V = 128
MLA_HEADS = 8
MLA_Q_LORA = 512
MLA_KV_LORA = 256
MLA_NOPE = 128
MLA_ROPE = 64
MLA_V = 128
MLA_QK_PAD = 256
ROPE_BASE = 10000.0
SB_HEADS = 8
SB_DIM = 128
BRANCH_WIDTH = 1024
N_BRANCHES = 3
T5_BUCKETS = 32
T5_MAX_DIST = 128
D_FF = 4 * D_MODEL
N_MOD = 6
EPS = 1e-6
NEG_INF = -1e30
LOG2E = 1.4426950408889634

HEAD_W = 128
ATT_TQ = 256
ATT_TK = 256
DIFF_HEAD_GROUP = 4
MLA_HEAD_GROUP = 4
SB_HEAD_GROUP = 4
SB_LOGW_UNDERFLOW = -105.0
V7X_VMEM_LIMIT = 56 * 1024 * 1024


def _cparams(*sem):
    return pltpu.CompilerParams(dimension_semantics=sem, vmem_limit_bytes=V7X_VMEM_LIMIT)


def _mod_kernel(c_ref, w_ref, b_ref, o_ref):
    w = w_ref[0].astype(BF)
    o_ref[0] = jnp.dot(c_ref[...], w, preferred_element_type=F32) + b_ref[0]


def _mod_call(c8, w_ada, b_ada3):
    depth, d, n = w_ada.shape
    tn = 1024
    return pl.pallas_call(
        _mod_kernel,
        grid=(depth, n // tn),
        in_specs=[pl.BlockSpec((8, d), lambda l, j: (0, 0)),
                  pl.BlockSpec((1, d, tn), lambda l, j: (l, 0, j)),
                  pl.BlockSpec((1, 1, tn), lambda l, j: (l, 0, j))],
        out_specs=pl.BlockSpec((1, 8, tn), lambda l, j: (l, 0, j)),
        out_shape=jax.ShapeDtypeStruct((depth, 8, n), F32),
        compiler_params=_cparams("arbitrary", "arbitrary"),
        name="adaln_mod",
    )(c8, w_ada, b_ada3)


def _norm_kernel(x_ref, g_ref, sc_ref, sh_ref, h_ref, *maybe_ht_ref):
    x = x_ref[...]
    ms = jnp.mean(x * x, axis=-1, keepdims=True)
    y = x * lax.rsqrt(ms + EPS) * g_ref[...]
    h = y * (1.0 + sc_ref[...]) + sh_ref[...]
    h_ref[...] = h.astype(BF)
    if maybe_ht_ref:
        maybe_ht_ref[0][...] = h.T.astype(BF)


def _norm_call(x, g, sc, sh, with_transpose):
    s, d = x.shape
    tm = min(512, s)
    row = pl.BlockSpec((1, d), lambda i: (0, 0))
    out_shape = [jax.ShapeDtypeStruct((s, d), BF)]
    out_specs = [pl.BlockSpec((tm, d), lambda i: (i, 0))]
    if with_transpose:
        out_shape.append(jax.ShapeDtypeStruct((d, s), BF))
        out_specs.append(pl.BlockSpec((d, tm), lambda i: (0, i)))
    return pl.pallas_call(
        _norm_kernel,
        grid=(s // tm,),
        in_specs=[pl.BlockSpec((tm, d), lambda i: (i, 0)), row, row, row],
        out_specs=out_specs,
        out_shape=out_shape,
        compiler_params=_cparams("arbitrary"),
        name="adaln_rmsnorm",
    )(x, g, sc, sh)


def _proj_t_kernel(w_ref, xt_ref, rs_ref, o_ref, *, group, layout, tk):
    y = jnp.dot(w_ref[...], xt_ref[...], preferred_element_type=F32)
    tn, tm = y.shape
    if group:
        y3 = y.reshape(tn // group, group, tm)
        ms = jnp.mean(y3 * y3, axis=1, keepdims=True)
        y = (y3 * lax.rsqrt(ms + EPS)).reshape(tn, tm)
    y = y * rs_ref[...]
    if layout == "t":
        o_ref[...] = y.astype(o_ref.dtype)
    elif layout == "n":
        o_ref[...] = y.T.astype(o_ref.dtype)
    elif layout == "t4":
        for a in range(tn // HEAD_W):
            for b in range(tm // tk):
                o_ref[a, b] = y[a * HEAD_W:(a + 1) * HEAD_W, b * tk:(b + 1) * tk].astype(o_ref.dtype)
    elif layout == "qpad":
        first = lax.broadcasted_iota(jnp.int32, (HEAD_W, tm), 0) < DA_QK
        for a in range(tn // HEAD_W):
            ya = y[a * HEAD_W:(a + 1) * HEAD_W]
            o_ref[2 * a] = jnp.where(first, ya, 0.0).astype(o_ref.dtype)
            o_ref[2 * a + 1] = jnp.where(first, 0.0, ya).astype(o_ref.dtype)


def _proj_t_call(wt, xt, rowscale, *, group=0, layout="t", out_dtype=BF, name):
    n, k = wt.shape
    s = xt.shape[1]
    tn = min(1024, n)
    tm = min(1024, s)
    tk = min(ATT_TK, s)
    if layout == "t":
        out_shape = jax.ShapeDtypeStruct((n, s), out_dtype)
        out_spec = pl.BlockSpec((tn, tm), lambda i, j: (i, j))
    elif layout == "n":
        out_shape = jax.ShapeDtypeStruct((s, n), out_dtype)
        out_spec = pl.BlockSpec((tm, tn), lambda i, j: (j, i))
    elif layout == "t4":
        out_shape = jax.ShapeDtypeStruct((n // HEAD_W, s // tk, HEAD_W, tk), out_dtype)
        out_spec = pl.BlockSpec((tn // HEAD_W, tm // tk, HEAD_W, tk), lambda i, j: (i, j, 0, 0))
    else:
        out_shape = jax.ShapeDtypeStruct((2 * n // HEAD_W, HEAD_W, s), out_dtype)
        out_spec = pl.BlockSpec((2 * tn // HEAD_W, HEAD_W, tm), lambda i, j: (i, 0, j))
    return pl.pallas_call(
        functools.partial(_proj_t_kernel, group=group, layout=layout, tk=tk),
        grid=(n // tn, s // tm),
        in_specs=[pl.BlockSpec((tn, k), lambda i, j: (i, 0)),
                  pl.BlockSpec((k, tm), lambda i, j: (0, j)),
                  pl.BlockSpec((tn, 1), lambda i, j: (i, 0))],
        out_specs=out_spec,
        out_shape=out_shape,
        compiler_params=_cparams("arbitrary", "arbitrary"),
        name=name,
    )(wt, xt, rowscale)


def _proj_n_kernel(x_ref, w_ref, o_ref):
    o_ref[...] = jnp.dot(x_ref[...], w_ref[...], preferred_element_type=F32).astype(o_ref.dtype)


def _proj_n_call(x, w, *, out_dtype, name):
    s, k = x.shape
    n = w.shape[1]
    tm = min(1024, s)
    tn = min(1024, n)
    return pl.pallas_call(
        _proj_n_kernel,
        grid=(s // tm, n // tn),
        in_specs=[pl.BlockSpec((tm, k), lambda i, j: (i, 0)),
                  pl.BlockSpec((k, tn), lambda i, j: (0, j))],
        out_specs=pl.BlockSpec((tm, tn), lambda i, j: (i, j)),
        out_shape=jax.ShapeDtypeStruct((s, n), out_dtype),
        compiler_params=_cparams("arbitrary", "arbitrary"),
        name=name,
    )(x, w)


def _resid_proj_kernel(a_ref, w_ref, x_ref, g_ref, o_ref):
    y = jnp.dot(a_ref[...], w_ref[...], preferred_element_type=F32)
    o_ref[...] = x_ref[...] + g_ref[...] * y


def _resid_proj_call(a, w, x, g):
    s, k = a.shape
    n = w.shape[1]
    tm = min(1024, s)
    tn = min(1024, n)
    return pl.pallas_call(
        _resid_proj_kernel,
        grid=(s // tm, n // tn),
        in_specs=[pl.BlockSpec((tm, k), lambda i, j: (i, 0)),
                  pl.BlockSpec((k, tn), lambda i, j: (0, j)),
                  pl.BlockSpec((tm, tn), lambda i, j: (i, j)),
                  pl.BlockSpec((1, tn), lambda i, j: (0, j))],
        out_specs=pl.BlockSpec((tm, tn), lambda i, j: (i, j)),
        out_shape=jax.ShapeDtypeStruct((s, n), F32),
        compiler_params=_cparams("arbitrary", "arbitrary"),
        name="out_proj_residual",
    )(a, w, x, g)


def _mla_up_kernel(cb_ref, gq_ref, gkv_ref, wqt_ref, wk_ref, wvt_ref, gqn_ref, gqp_ref, gkn_ref,
                   gkp_ref, gkps_ref, cost_ref, sint_ref, cosf_ref, sinf_ref,
                   qt_ref, k_ref, vt_ref, *, tk):
    tm = cb_ref.shape[0]
    half = MLA_ROPE // 2
    cq = cb_ref[:, 0:MLA_Q_LORA]
    cqn = cq * lax.rsqrt(jnp.mean(cq * cq, axis=-1, keepdims=True) + EPS) * gq_ref[...]
    qt = jnp.dot(wqt_ref[...], cqn.T.astype(BF), preferred_element_type=F32)
    cos_t = cost_ref[...]
    sin_t = sint_ref[...]
    hw = MLA_NOPE + MLA_ROPE
    for h in range(MLA_HEADS):
        nope = qt[h * hw:h * hw + MLA_NOPE]
        nope = nope * lax.rsqrt(jnp.mean(nope * nope, axis=0, keepdims=True) + EPS) * gqn_ref[...]
        pe = qt[h * hw + MLA_NOPE:(h + 1) * hw]
        pe = pe * lax.rsqrt(jnp.mean(pe * pe, axis=0, keepdims=True) + EPS) * gqp_ref[...]
        x1 = pe[:half]
        x2 = pe[half:]
        qt_ref[h, 0:MLA_NOPE] = nope.astype(BF)
        qt_ref[h, MLA_NOPE:MLA_NOPE + half] = (x1 * cos_t - x2 * sin_t).astype(BF)
        qt_ref[h, MLA_NOPE + half:hw] = (x1 * sin_t + x2 * cos_t).astype(BF)
        qt_ref[h, hw:MLA_QK_PAD] = jnp.zeros((MLA_QK_PAD - hw, tm), BF)
    ckv = cb_ref[:, MLA_Q_LORA:MLA_Q_LORA + MLA_KV_LORA]
    ckvn = ckv * lax.rsqrt(jnp.mean(ckv * ckv, axis=-1, keepdims=True) + EPS) * gkv_ref[...]
    kn = jnp.dot(ckvn.astype(BF), wk_ref[...], preferred_element_type=F32)
    vt = jnp.dot(wvt_ref[...], ckvn.T.astype(BF), preferred_element_type=F32)
    o = MLA_Q_LORA + MLA_KV_LORA
    kpe = cb_ref[:, o:o + MLA_ROPE]
    kpe_rot = cb_ref[:, o + 128:o + 128 + MLA_ROPE]
    r = lax.rsqrt(jnp.mean(kpe * kpe, axis=-1, keepdims=True) + EPS)
    kpe_f = (r * (kpe * gkp_ref[...] * cosf_ref[...] + kpe_rot * gkps_ref[...] * sinf_ref[...])).astype(BF)
    zpad = jnp.zeros((tm, MLA_QK_PAD - hw), BF)
    for h in range(MLA_HEADS):
        kh = kn[:, h * MLA_NOPE:(h + 1) * MLA_NOPE]
        kh = kh * lax.rsqrt(jnp.mean(kh * kh, axis=-1, keepdims=True) + EPS) * gkn_ref[...]
        k_ref[h, :, 0:MLA_NOPE] = kh.astype(BF)
        k_ref[h, :, MLA_NOPE:hw] = kpe_f
        k_ref[h, :, hw:MLA_QK_PAD] = zpad
        for b in range(tm // tk):
            vt_ref[h, b] = vt[h * MLA_V:(h + 1) * MLA_V, b * tk:(b + 1) * tk].astype(BF)


def _mla_up_call(cb, gq, gkv, wqt, wk, wvt, gqn, gqp, gkn, gkp, gkps, cos_t, sin_t, cosf, sinf):
    s = cb.shape[0]
    tm = min(512, s)
    tk = min(ATT_TK, s)
    hw = MLA_NOPE + MLA_ROPE
    half = MLA_ROPE // 2

    def const(shape):
        return pl.BlockSpec(shape, lambda i: tuple(0 for _ in shape))

    return pl.pallas_call(
        functools.partial(_mla_up_kernel, tk=tk),
        grid=(s // tm,),
        in_specs=[pl.BlockSpec((tm, cb.shape[1]), lambda i: (i, 0)),
                  const((1, MLA_Q_LORA)), const((1, MLA_KV_LORA)),
                  const((MLA_HEADS * hw, MLA_Q_LORA)),
                  const((MLA_KV_LORA, MLA_HEADS * MLA_NOPE)),
                  const((MLA_HEADS * MLA_V, MLA_KV_LORA)),
                  const((MLA_NOPE, 1)), const((MLA_ROPE, 1)), const((1, MLA_NOPE)),
                  const((1, MLA_ROPE)), const((1, MLA_ROPE)),
                  pl.BlockSpec((half, tm), lambda i: (0, i)),
                  pl.BlockSpec((half, tm), lambda i: (0, i)),
                  pl.BlockSpec((tm, MLA_ROPE), lambda i: (i, 0)),
                  pl.BlockSpec((tm, MLA_ROPE), lambda i: (i, 0))],
        out_specs=[pl.BlockSpec((MLA_HEADS, MLA_QK_PAD, tm), lambda i: (0, 0, i)),
                   pl.BlockSpec((MLA_HEADS, tm, MLA_QK_PAD), lambda i: (0, i, 0)),
                   pl.BlockSpec((MLA_HEADS, tm // tk, MLA_V, tk), lambda i: (0, i, 0, 0))],
        out_shape=[jax.ShapeDtypeStruct((MLA_HEADS, MLA_QK_PAD, s), BF),
                   jax.ShapeDtypeStruct((MLA_HEADS, s, MLA_QK_PAD), BF),
                   jax.ShapeDtypeStruct((MLA_HEADS, s // tk, MLA_V, tk), BF)],
        compiler_params=_cparams("arbitrary"),
        name="mla_up",
    )(cb, gq, gkv, wqt, wk, wvt, gqn, gqp, gkn, gkp, gkps, cos_t, sin_t, cosf, sinf)


def _softmax_block(kb, biases, qts, k_ofs, vt_ref, m_ref, l_ref, acc_ref):
    hg = len(qts)
    ss = [jnp.dot(k_ofs[g](kb), qts[g], preferred_element_type=F32) for g in range(hg)]
    alphas, ps = [], []
    for g in range(hg):
        s = ss[g] if biases is None else ss[g] + biases[g]
        m_old = m_ref[g]
        m_new = jnp.maximum(m_old, jnp.max(s, axis=0, keepdims=True))
        alpha = jnp.exp2(m_old - m_new)
        p = jnp.exp2(s - m_new)
        l_ref[g] = alpha * l_ref[g] + jnp.sum(p, axis=0, keepdims=True)
        m_ref[g] = m_new
        alphas.append(alpha)
        ps.append(p.astype(BF))
    pvs = [jnp.dot(vt_ref[g, kb], ps[g], preferred_element_type=F32) for g in range(hg)]
    for g in range(hg):
        acc_ref[g] = alphas[g] * acc_ref[g] + pvs[g]


def _resident_spec(shape, index_map):
    return pl.BlockSpec(shape, index_map, pipeline_mode=pl.Buffered(1))


def _init_softmax_state(m_ref, l_ref, acc_ref):
    m_ref[...] = jnp.full(m_ref.shape, NEG_INF, F32)
    l_ref[...] = jnp.zeros(l_ref.shape, F32)
    acc_ref[...] = jnp.zeros(acc_ref.shape, F32)


def _diff_attn_kernel(q_ref, k_ref, vt_ref, bd_ref, bp_ref, lam_ref, g_ref, o_ref,
                      m_ref, l_ref, acc_ref, *, tq, tk, hg, lam_init):
    qb = pl.program_id(1)
    qts = [jnp.concatenate([q_ref[2 * g], q_ref[2 * g + 1]], axis=1) for g in range(hg)]

    def k_of(g):
        return lambda kb: k_ref[pl.ds(pl.multiple_of(kb * tk, tk), tk), g * HEAD_W:(g + 1) * HEAD_W]

    k_ofs = [k_of(g) for g in range(hg)]
    _init_softmax_state(m_ref, l_ref, acc_ref)

    def far(kb, carry):
        _softmax_block(kb, None, qts, k_ofs, vt_ref, m_ref, l_ref, acc_ref)
        return carry

    lax.fori_loop(0, jnp.maximum(qb - 1, 0), far, 0)

    @pl.when(qb >= 1)
    def _():
        _softmax_block(qb - 1, [bp_ref[g] for g in range(hg)], qts, k_ofs, vt_ref, m_ref, l_ref, acc_ref)

    _softmax_block(qb, [bd_ref[g] for g in range(hg)], qts, k_ofs, vt_ref, m_ref, l_ref, acc_ref)

    lp = lam_ref[...]
    lam = (jnp.exp(jnp.sum(lp[0:1] * lp[1:2], axis=1, keepdims=True))
           - jnp.exp(jnp.sum(lp[2:3] * lp[3:4], axis=1, keepdims=True)) + lam_init)
    for g in range(hg):
        o = acc_ref[g] / l_ref[g]
        d = o[:, :tq] - lam * o[:, tq:]
        ms = jnp.mean(d * d, axis=0, keepdims=True)
        d = d * lax.rsqrt(ms + EPS) * g_ref[...] * (1.0 - lam_init)
        o_ref[:, g * HEAD_W:(g + 1) * HEAD_W] = d.T.astype(o_ref.dtype)


def _diff_attn_call(qpad, k, vt4, bias_diag, bias_prev, lam_p, sub_g, *, head0, lam_init):
    s = k.shape[0]
    tq = tk = min(ATT_TQ, s)
    nkb = s // tk
    n = 2 * tq
    hg = DIFF_HEAD_GROUP
    return pl.pallas_call(
        functools.partial(_diff_attn_kernel, tq=tq, tk=tk, hg=hg, lam_init=lam_init),
        grid=(DA_HEADS // hg, s // tq),
        in_specs=[pl.BlockSpec((2 * hg, HEAD_W, tq), lambda h, i: (h, 0, i)),
                  _resident_spec((s, hg * HEAD_W), lambda h, i: (0, h)),
                  _resident_spec((hg, nkb, HEAD_W, tk), lambda h, i: (head0 // hg + h, 0, 0, 0)),
                  _resident_spec((hg, tk, n), lambda h, i: (h, 0, 0)),
                  _resident_spec((hg, tk, n), lambda h, i: (h, 0, 0)),
                  pl.BlockSpec((4, DA_QK), lambda h, i: (0, 0)),
                  pl.BlockSpec((HEAD_W, 1), lambda h, i: (0, 0))],
        out_specs=pl.BlockSpec((tq, hg * HEAD_W), lambda h, i: (i, h)),
        out_shape=jax.ShapeDtypeStruct((s, DA_HEADS * HEAD_W), BF),
        scratch_shapes=[pltpu.VMEM((hg, 1, n), F32), pltpu.VMEM((hg, 1, n), F32),
                        pltpu.VMEM((hg, HEAD_W, n), F32)],
        compiler_params=_cparams("arbitrary", "arbitrary"),
        name="diff_attention",
    )(qpad, k, vt4, bias_diag, bias_prev, lam_p, sub_g)


def _mla_attn_kernel(q_ref, k_ref, vt_ref, mask_ref, o_ref, m_ref, l_ref, acc_ref, *, tk, hg):
    qb = pl.program_id(1)
    qts = [q_ref[g] for g in range(hg)]

    def k_of(g):
        return lambda kb: k_ref[g, pl.ds(pl.multiple_of(kb * tk, tk), tk), :]

    k_ofs = [k_of(g) for g in range(hg)]
    _init_softmax_state(m_ref, l_ref, acc_ref)

    def far(kb, carry):
        _softmax_block(kb, None, qts, k_ofs, vt_ref, m_ref, l_ref, acc_ref)
        return carry

    lax.fori_loop(0, qb, far, 0)
    _softmax_block(qb, [mask_ref[...]] * hg, qts, k_ofs, vt_ref, m_ref, l_ref, acc_ref)
    for g in range(hg):
        o = acc_ref[g] / l_ref[g]
        o_ref[:, g * HEAD_W:(g + 1) * HEAD_W] = o.T.astype(o_ref.dtype)


def _mla_attn_call(qt, k, vt4, mask_diag):
    s = k.shape[1]
    tq = tk = min(ATT_TQ, s)
    nkb = s // tk
    hg = MLA_HEAD_GROUP
    return pl.pallas_call(
        functools.partial(_mla_attn_kernel, tk=tk, hg=hg),
        grid=(MLA_HEADS // hg, s // tq),
        in_specs=[pl.BlockSpec((hg, MLA_QK_PAD, tq), lambda h, i: (h, 0, i)),
                  _resident_spec((hg, s, MLA_QK_PAD), lambda h, i: (h, 0, 0)),
                  _resident_spec((hg, nkb, HEAD_W, tk), lambda h, i: (h, 0, 0, 0)),
                  pl.BlockSpec((tk, tq), lambda h, i: (0, 0))],
        out_specs=pl.BlockSpec((tq, hg * HEAD_W), lambda h, i: (i, h)),
        out_shape=jax.ShapeDtypeStruct((s, MLA_HEADS * HEAD_W), BF),
        scratch_shapes=[pltpu.VMEM((hg, 1, tq), F32), pltpu.VMEM((hg, 1, tq), F32),
                        pltpu.VMEM((hg, HEAD_W, tq), F32)],
        compiler_params=_cparams("arbitrary", "arbitrary"),
        name="mla_attention",
    )(qt, k, vt4, mask_diag)


def _stick_block(kb, mask, qts, k_ref, vt_ref, tri, carry_ref, acc_ref, tk):
    hg = len(qts)
    rows = pl.ds(pl.multiple_of(kb * tk, tk), tk)
    zs = [jnp.dot(k_ref[rows, g * HEAD_W:(g + 1) * HEAD_W], qts[g], preferred_element_type=F32)
          for g in range(hg)]
    logsigs, his, los = [], [], []
    for g in range(hg):
        z = zs[g]
        sp = jnp.maximum(z, 0.0) + jnp.log1p(jnp.exp(-jnp.abs(z)))
        log1m = -sp
        if mask is not None:
            log1m = jnp.where(mask, log1m, 0.0)
        hi = log1m.astype(BF)
        his.append(hi)
        los.append((log1m - hi.astype(F32)).astype(BF))
        logsigs.append(z - sp + carry_ref[g])
        carry_ref[g] += jnp.sum(log1m, axis=0, keepdims=True)
    betweens = [jnp.dot(tri, his[g], preferred_element_type=F32)
                + jnp.dot(tri, los[g], preferred_element_type=F32) for g in range(hg)]
    ws = []
    for g in range(hg):
        w = jnp.exp(logsigs[g] + betweens[g])
        if mask is not None:
            w = jnp.where(mask, w, 0.0)
        ws.append(w.astype(BF))
    pvs = [jnp.dot(vt_ref[g, kb], ws[g], preferred_element_type=F32) for g in range(hg)]
    for g in range(hg):
        acc_ref[g] += pvs[g]


def _stick_kernel(q_ref, k_ref, vt_ref, tri_ref, o_ref, carry_ref, acc_ref, *, tq, tk, hg):
    qb = pl.program_id(1)
    qts = [q_ref[g * HEAD_W:(g + 1) * HEAD_W, :] for g in range(hg)]
    tri = tri_ref[...]
    carry_ref[...] = jnp.zeros(carry_ref.shape, F32)
    acc_ref[...] = jnp.zeros(acc_ref.shape, F32)
    kpos = lax.broadcasted_iota(jnp.int32, (tk, tq), 0)
    qpos = lax.broadcasted_iota(jnp.int32, (tk, tq), 1)
    _stick_block(qb, kpos < qpos, qts, k_ref, vt_ref, tri, carry_ref, acc_ref, tk)

    def more(state):
        i, cmax = state
        return jnp.logical_and(i < qb, cmax > SB_LOGW_UNDERFLOW)

    def earlier(state):
        i, _ = state
        _stick_block(qb - 1 - i, None, qts, k_ref, vt_ref, tri, carry_ref, acc_ref, tk)
        return i + 1, jnp.max(carry_ref[...])

    lax.while_loop(more, earlier, (jnp.int32(0), jnp.max(carry_ref[...])))
    for g in range(hg):
        o_ref[:, g * HEAD_W:(g + 1) * HEAD_W] = acc_ref[g].T.astype(o_ref.dtype)


def _stick_call(qt, k, vt4, tri, *, head0):
    s = k.shape[0]
    tq = tk = min(ATT_TQ, s)
    nkb = s // tk
    hg = SB_HEAD_GROUP
    return pl.pallas_call(
        functools.partial(_stick_kernel, tq=tq, tk=tk, hg=hg),
        grid=(SB_HEADS // hg, s // tq),
        in_specs=[pl.BlockSpec((hg * HEAD_W, tq), lambda h, i: (h, i)),
                  _resident_spec((s, hg * HEAD_W), lambda h, i: (0, h)),
                  _resident_spec((hg, nkb, HEAD_W, tk), lambda h, i: (head0 // hg + h, 0, 0, 0)),
                  pl.BlockSpec((tk, tk), lambda h, i: (0, 0))],
        out_specs=pl.BlockSpec((tq, hg * HEAD_W), lambda h, i: (i, h)),
        out_shape=jax.ShapeDtypeStruct((s, SB_HEADS * HEAD_W), BF),
        scratch_shapes=[pltpu.VMEM((hg, 1, tq), F32), pltpu.VMEM((hg, HEAD_W, tq), F32)],
        compiler_params=_cparams("arbitrary", "arbitrary"),
        name="stick_breaking",
    )(qt, k, vt4, tri)


def _merge_kernel(h_ref, ya_ref, yb_ref, yc_ref, wg_ref, wb_ref, o_ref):
    h = h_ref[...]
    acc = None
    for n, y_ref in enumerate((ya_ref, yb_ref, yc_ref)):
        gate = jax.nn.sigmoid(jnp.dot(h, wg_ref[n], preferred_element_type=F32))
        up = jnp.dot(y_ref[...], wb_ref[n], preferred_element_type=F32)
        acc = gate * up if acc is None else acc + gate * up
    o_ref[...] = acc.astype(o_ref.dtype)


def _merge_call(h, ya, yb, yc, wg, wb):
    s, d = h.shape
    tm = min(512, s)
    tn = 512
    ysp = pl.BlockSpec((tm, BRANCH_WIDTH), lambda i, j: (i, 0))
    return pl.pallas_call(
        _merge_kernel,
        grid=(s // tm, d // tn),
        in_specs=[pl.BlockSpec((tm, d), lambda i, j: (i, 0)), ysp, ysp, ysp,
                  pl.BlockSpec((N_BRANCHES, d, tn), lambda i, j: (0, 0, j)),
                  pl.BlockSpec((N_BRANCHES, BRANCH_WIDTH, tn), lambda i, j: (0, 0, j))],
        out_specs=pl.BlockSpec((tm, tn), lambda i, j: (i, j)),
        out_shape=jax.ShapeDtypeStruct((s, d), BF),
        compiler_params=_cparams("arbitrary", "arbitrary"),
        name="gated_merge",
    )(h, ya, yb, yc, wg, wb)


def _mlp_kernel(h_ref, w1_ref, w2_ref, x_ref, g_ref, o_ref, acc_ref):
    f = pl.program_id(1)

    @pl.when(f == 0)
    def _():
        acc_ref[...] = jnp.zeros(acc_ref.shape, F32)

    u = jnp.dot(h_ref[...], w1_ref[...], preferred_element_type=F32)
    u = jnp.square(jnp.maximum(u, 0.0)).astype(BF)
    acc_ref[...] += jnp.dot(u, w2_ref[...], preferred_element_type=F32)

    @pl.when(f == pl.num_programs(1) - 1)
    def _():
        o_ref[...] = x_ref[...] + g_ref[...] * acc_ref[...]


def _mlp_call(h, w1, w2, x, g):
    s, d = h.shape
    ff = w1.shape[1]
    tm = min(512, s)
    tf = 1024
    return pl.pallas_call(
        _mlp_kernel,
        grid=(s // tm, ff // tf),
        in_specs=[pl.BlockSpec((tm, d), lambda i, f: (i, 0)),
                  pl.BlockSpec((d, tf), lambda i, f: (0, f)),
                  pl.BlockSpec((tf, d), lambda i, f: (f, 0)),
                  pl.BlockSpec((tm, d), lambda i, f: (i, 0)),
                  pl.BlockSpec((1, d), lambda i, f: (0, 0))],
        out_specs=pl.BlockSpec((tm, d), lambda i, f: (i, 0)),
        out_shape=jax.ShapeDtypeStruct((s, d), F32),
        scratch_shapes=[pltpu.VMEM((tm, d), F32)],
        compiler_params=_cparams("arbitrary", "arbitrary"),
        name="sqrelu_mlp",
    )(h, w1, w2, x, g)


def _t5_bucket(rel):
    nb = T5_BUCKETS // 2
    max_exact = nb // 2
    n = jnp.abs(rel)
    large = max_exact + (jnp.log(jnp.maximum(n, 1).astype(F32) / max_exact)
                         / math.log(T5_MAX_DIST / max_exact) * (nb - max_exact)).astype(jnp.int32)
    large = jnp.minimum(large, nb - 1)
    return jnp.where(rel > 0, nb, 0) + jnp.where(n < max_exact, n, large)


def _bias_tiles(t5_bias, tq, tk):
    table = t5_bias.astype(F32)
    kl = jnp.arange(tk)[:, None]
    ql = jnp.arange(tq)[None, :]
    far = table[_t5_bucket(jnp.array(-(tk + tq), jnp.int32))]
    diag = jnp.transpose(table[_t5_bucket(kl - ql)], (2, 0, 1)) - far[:, None, None]
    prev = jnp.transpose(table[_t5_bucket(kl - tk - ql)], (2, 0, 1)) - far[:, None, None]
    allowed = (kl // CHUNK) <= (ql // CHUNK)
    diag = jnp.where(allowed[None], diag * LOG2E, NEG_INF)
    prev = prev * LOG2E
    return (jnp.concatenate([diag, diag], axis=2), jnp.concatenate([prev, prev], axis=2))


def _rope_tables(s):
    half = MLA_ROPE // 2
    inv = ROPE_BASE ** (-jnp.arange(half, dtype=F32) / half)
    ang = jnp.arange(s).astype(F32)[:, None] * inv[None, :]
    cos, sin = jnp.cos(ang), jnp.sin(ang)
    return (cos.T, sin.T, jnp.concatenate([cos, cos], axis=1), jnp.concatenate([sin, sin], axis=1))


def kernel(x, c, w_ada, b_ada, norm_mix_g, norm_mlp_g, w_in, diff_qk_g, diff_lambda, diff_subln_g,
           t5_bias, mla_q_norm_g, mla_kv_norm_g, w_q_up, w_kv_up, mla_qk_g, w_branch, w_out,
           w_mlp_in, w_mlp_out):
    b, s, d = x.shape
    assert b == 1 and d == D_MODEL
    depth = w_ada.shape[0]
    tq = tk = min(ATT_TQ, s)
    hw = MLA_NOPE + MLA_ROPE
    half = MLA_ROPE // 2

    mod = _mod_call(jnp.broadcast_to(c, (8, d)).astype(BF), w_ada, b_ada[:, None, :])[:, 0, :]
    mod = mod.reshape(depth, N_MOD, 1, d)

    o_q, o_k, o_v = 0, 1024, 2048
    o_cq, o_ckv, o_kpe = 3072, 3584, 3840
    o_sq, o_sk, o_sv, o_g = 3904, 4928, 5952, 6976

    def tr(w):
        return jnp.swapaxes(w, 1, 2).astype(BF)

    wt_qa = tr(w_in[:, :, o_q:o_k])
    wt_ka = tr(w_in[:, :, o_k:o_v])
    wt_v = tr(jnp.concatenate([w_in[:, :, o_v:o_cq], w_in[:, :, o_sv:o_g]], axis=2))
    wt_qc = tr(w_in[:, :, o_sq:o_sk])
    w_kc = w_in[:, :, o_sk:o_sv].astype(BF)
    w_kpe = w_in[:, :, o_kpe:o_sq]
    w_kpe_rot = jnp.concatenate([-w_kpe[:, :, half:], w_kpe[:, :, :half]], axis=2)
    zcol = jnp.zeros((depth, d, 128 - MLA_ROPE), w_in.dtype)
    w_b = jnp.concatenate([w_in[:, :, o_cq:o_kpe], w_kpe, zcol, w_kpe_rot, zcol], axis=2).astype(BF)
    w_gate = jnp.transpose(w_in[:, :, o_g:].reshape(depth, d, N_BRANCHES, d), (0, 2, 1, 3)).astype(BF)
    w_br = w_branch.astype(BF)
    w_o = w_out.astype(BF)
    w_1 = w_mlp_in.astype(BF)
    w_2 = w_mlp_out.astype(BF)
    wqt = tr(w_q_up)
    wkv = w_kv_up.reshape(depth, MLA_KV_LORA, MLA_HEADS, MLA_NOPE + MLA_V)
    w_kn = wkv[..., :MLA_NOPE].reshape(depth, MLA_KV_LORA, MLA_HEADS * MLA_NOPE).astype(BF)
    wt_vb = tr(wkv[..., MLA_NOPE:].reshape(depth, MLA_KV_LORA, MLA_HEADS * MLA_V))

    bias_diag, bias_prev = _bias_tiles(t5_bias, tq, tk)
    kl = jnp.arange(tk)[:, None]
    ql = jnp.arange(tq)[None, :]
    mask_diag = jnp.where((kl // CHUNK) <= (ql // CHUNK), 0.0, NEG_INF).astype(F32)
    tri = (jnp.arange(tk)[None, :] > jnp.arange(tk)[:, None]).astype(BF)
    cos_t, sin_t, cosf, sinf = _rope_tables(s)

    ones_v = jnp.ones((2 * BRANCH_WIDTH, 1), F32)
    sb_scale = jnp.full((SB_HEADS * SB_DIM, 1), SB_DIM ** -0.5, F32)
    da_scale = DA_QK ** -0.5 * LOG2E
    mla_scale = hw ** -0.5 * LOG2E

    x2 = x[0]
    for l in range(depth):
        sh1, sc1, g1, sh2, sc2, g2 = (mod[l, i] for i in range(N_MOD))
        lam_init = 0.8 - 0.6 * math.exp(-0.3 * l)

        h, ht = _norm_call(x2, norm_mix_g[l][None], sc1, sh1, True)

        gq_rows = jnp.tile(diff_qk_g[l, 0], 2 * DA_HEADS)[:, None] * da_scale
        gk_rows = jnp.tile(diff_qk_g[l, 1], 2 * DA_HEADS)[:, None]
        qa = _proj_t_call(wt_qa[l], ht, gq_rows, group=DA_QK, layout="qpad", name="proj_diff_q")
        ka = _proj_t_call(wt_ka[l], ht, gk_rows, group=DA_QK, layout="n", name="proj_diff_k")
        vt = _proj_t_call(wt_v[l], ht, ones_v, layout="t4", name="proj_v")
        qc = _proj_t_call(wt_qc[l], ht, sb_scale, layout="t", name="proj_sb_q")
        kc = _proj_n_call(h, w_kc[l], out_dtype=BF, name="proj_sb_k")
        cb = _proj_n_call(h, w_b[l], out_dtype=F32, name="proj_mla_latent")
        gqk = mla_qk_g[l]
        qb_t, kb, vb_t = _mla_up_call(
            cb, mla_q_norm_g[l][None], mla_kv_norm_g[l][None], wqt[l], w_kn[l], wt_vb[l],
            gqk[0, :MLA_NOPE, None] * mla_scale, gqk[0, MLA_NOPE:, None] * mla_scale,
            gqk[1, None, :MLA_NOPE], gqk[1, None, MLA_NOPE:],
            jnp.concatenate([gqk[1, MLA_NOPE + half:], gqk[1, MLA_NOPE:MLA_NOPE + half]])[None],
            cos_t, sin_t, cosf, sinf)

        ya = _diff_attn_call(qa, ka, vt, bias_diag, bias_prev, diff_lambda[l],
                             diff_subln_g[l][:, None], head0=0, lam_init=lam_init)
        yb = _mla_attn_call(qb_t, kb, vb_t, mask_diag)
        yc = _stick_call(qc, kc, vt, tri, head0=DA_HEADS)

        merged = _merge_call(h, ya, yb, yc, w_gate[l], w_br[l])
        x2 = _resid_proj_call(merged, w_o[l], x2, g1)

        h2 = _norm_call(x2, norm_mlp_g[l][None], sc2, sh2, False)[0]
        x2 = _mlp_call(h2, w_1[l], w_2[l], x2, g2)
    return x2[None]
```

```python
import functools
import math

import jax
import jax.numpy as jnp
from jax import lax
from jax.experimental import pallas as pl
from jax.experimental.pallas import tpu as pltpu

BF = jnp.bfloat16
F32 = jnp.float32

D_MODEL = 2048
CHUNK = 64
DA_HEADS = 8
DA_QK = 64
DA_V = 128
MLA_HEADS = 8
MLA_Q_LORA = 512
MLA_KV_LORA = 256
MLA_NOPE = 128
MLA_ROPE = 64
MLA_V = 128
MLA_QK_PAD = 256
ROPE_BASE = 10000.0
SB_HEADS = 8
SB_DIM = 128
BRANCH_WIDTH = 1024
N_BRANCHES = 3
T5_BUCKETS = 32
T5_MAX_DIST = 128
D_FF = 4 * D_MODEL
N_MOD = 6
EPS = 1e-6
NEG_INF = -1e30
LOG2E = 1.4426950408889634

HEAD_W = 128
ATT_TQ = 256
ATT_TK = 256
DIFF_FAR_BLOCKS = 4
MLA_FAR_BLOCKS = 4
DIFF_HEAD_GROUP = 4
MLA_HEAD_GROUP = 4
SB_HEAD_GROUP = 4
SB_LOGW_UNDERFLOW = -105.0
MXU_TILE = 256
V7X_VMEM_LIMIT = 56 * 1024 * 1024

SEG = 1024
SEG_DQ, SEG_DK, SEG_DV, SEG_LAT, SEG_SQ, SEG_SK, SEG_SV, SEG_GATE = 0, 1, 2, 3, 4, 5, 6, 7
LAT_KPE = MLA_Q_LORA + MLA_KV_LORA
LAT_KPE_ROT = LAT_KPE + 128


def _cparams(*sem):
    return pltpu.CompilerParams(dimension_semantics=sem, vmem_limit_bytes=V7X_VMEM_LIMIT)


def _mod_kernel(c_ref, w_ref, b_ref, o_ref):
    w = w_ref[0].astype(BF)
    o_ref[0] = jnp.dot(c_ref[...], w, preferred_element_type=F32) + b_ref[0]


def _mod_call(c8, w_ada, b_ada3):
    depth, d, n = w_ada.shape
    tn = 1024
    return pl.pallas_call(
        _mod_kernel,
        grid=(depth, n // tn),
        in_specs=[pl.BlockSpec((8, d), lambda l, j: (0, 0)),
                  pl.BlockSpec((1, d, tn), lambda l, j: (l, 0, j)),
                  pl.BlockSpec((1, 1, tn), lambda l, j: (l, 0, j))],
        out_specs=pl.BlockSpec((1, 8, tn), lambda l, j: (l, 0, j)),
        out_shape=jax.ShapeDtypeStruct((depth, 8, n), F32),
        compiler_params=_cparams("arbitrary", "arbitrary"),
        name="adaln_mod",
    )(c8, w_ada, b_ada3)


def _norm_kernel(x_ref, g_ref, sc_ref, sh_ref, h_ref):
    x = x_ref[...]
    ms = jnp.mean(x * x, axis=-1, keepdims=True)
    y = x * lax.rsqrt(ms + EPS) * g_ref[...]
    h_ref[...] = (y * (1.0 + sc_ref[...]) + sh_ref[...]).astype(BF)


def _norm_call(x, g, sc, sh):
    s, d = x.shape
    tm = min(512, s)
    row = pl.BlockSpec((1, d), lambda i: (0, 0))
    return pl.pallas_call(
        _norm_kernel,
        grid=(s // tm,),
        in_specs=[pl.BlockSpec((tm, d), lambda i: (i, 0)), row, row, row],
        out_specs=pl.BlockSpec((tm, d), lambda i: (i, 0)),
        out_shape=jax.ShapeDtypeStruct((s, d), BF),
        compiler_params=_cparams("arbitrary"),
        name="adaln_rmsnorm",
    )(x, g, sc, sh)


def _proj_kernel(x_ref, w_ref, rs_ref, *rest, layout, group, tk):
    o_ref = rest[-1]
    y = jnp.dot(x_ref[...], w_ref[...], preferred_element_type=F32)
    tm, tn = y.shape
    if layout == "n":
        o_ref[...] = y.astype(o_ref.dtype)
        return
    if layout == "n_gnorm":
        gmat = rest[0][...]
        ysq = y * y
        hi = ysq.astype(BF)
        lo = (ysq - hi.astype(F32)).astype(BF)
        parts = []
        for t in range(tn // MXU_TILE):
            cols = slice(t * MXU_TILE, (t + 1) * MXU_TILE)
            parts.append(jnp.dot(hi[:, cols], gmat, preferred_element_type=F32)
                         + jnp.dot(lo[:, cols], gmat, preferred_element_type=F32))
        ms = jnp.concatenate(parts, axis=1) * (1.0 / group)
        o_ref[...] = (y * lax.rsqrt(ms + EPS) * rs_ref[...]).astype(o_ref.dtype)
        return
    yt = y.T
    if group:
        y3 = yt.reshape(tn // group, group, tm)
        ms = jnp.mean(y3 * y3, axis=1, keepdims=True)
        yt = (y3 * lax.rsqrt(ms + EPS)).reshape(tn, tm)
    yt = yt * rs_ref[...]
    if layout == "t":
        o_ref[...] = yt.astype(o_ref.dtype)
    elif layout == "t4":
        for a in range(tn // HEAD_W):
            for b in range(tm // tk):
                o_ref[a, b] = yt[a * HEAD_W:(a + 1) * HEAD_W, b * tk:(b + 1) * tk].astype(o_ref.dtype)
    elif layout == "qpad":
        first = lax.broadcasted_iota(jnp.int32, (HEAD_W, tm), 0) < DA_QK
        for a in range(tn // HEAD_W):
            ya = yt[a * HEAD_W:(a + 1) * HEAD_W]
            o_ref[2 * a] = jnp.where(first, ya, 0.0).astype(o_ref.dtype)
            o_ref[2 * a + 1] = jnp.where(first, 0.0, ya).astype(o_ref.dtype)


def _proj_call(h, w_all, layer, segs, scale, *, layout, group=0, gmat=None, out_dtype=BF, name):
    s, k = h.shape
    seg0, nseg, stride = segs
    n = nseg * SEG
    tm = min(1024, s)
    tn = SEG
    tk = min(ATT_TK, s)
    in_specs = [pl.BlockSpec((tm, k), lambda i, j: (i, 0)),
                pl.BlockSpec((None, k, tn), lambda i, j: (layer, 0, seg0 + j * stride))]
    args = [h, w_all]
    if layout.startswith("n"):
        in_specs.append(pl.BlockSpec((1, tn), lambda i, j: (0, j)))
    else:
        in_specs.append(pl.BlockSpec((tn, 1), lambda i, j: (j, 0)))
    args.append(scale)
    if gmat is not None:
        in_specs.append(pl.BlockSpec(gmat.shape, lambda i, j: (0, 0)))
        args.append(gmat)
    if layout in ("n", "n_gnorm"):
        out_shape = jax.ShapeDtypeStruct((s, n), out_dtype)
        out_spec = pl.BlockSpec((tm, tn), lambda i, j: (i, j))
    elif layout == "t":
        out_shape = jax.ShapeDtypeStruct((n, s), out_dtype)
        out_spec = pl.BlockSpec((tn, tm), lambda i, j: (j, i))
    elif layout == "t4":
        out_shape = jax.ShapeDtypeStruct((n // HEAD_W, s // tk, HEAD_W, tk), out_dtype)
        out_spec = pl.BlockSpec((tn // HEAD_W, tm // tk, HEAD_W, tk), lambda i, j: (j, i, 0, 0))
    else:
        out_shape = jax.ShapeDtypeStruct((2 * n // HEAD_W, HEAD_W, s), out_dtype)
        out_spec = pl.BlockSpec((2 * tn // HEAD_W, HEAD_W, tm), lambda i, j: (j, 0, i))
    return pl.pallas_call(
        functools.partial(_proj_kernel, layout=layout, group=group, tk=tk),
        grid=(s // tm, nseg),
        in_specs=in_specs,
        out_specs=out_spec,
        out_shape=out_shape,
        compiler_params=_cparams("arbitrary", "arbitrary"),
        name=name,
    )(*args)


def _resid_proj_kernel(a_ref, w_ref, x_ref, g_ref, o_ref):
    y = jnp.dot(a_ref[...], w_ref[...], preferred_element_type=F32)
    o_ref[...] = x_ref[...] + g_ref[...] * y


def _resid_proj_call(a, w, layer, x, g):
    s, k = a.shape
    n = w.shape[2]
    tm = min(1024, s)
    tn = min(1024, n)
    return pl.pallas_call(
        _resid_proj_kernel,
        grid=(s // tm, n // tn),
        in_specs=[pl.BlockSpec((tm, k), lambda i, j: (i, 0)),
                  pl.BlockSpec((None, k, tn), lambda i, j: (layer, 0, j)),
                  pl.BlockSpec((tm, tn), lambda i, j: (i, j)),
                  pl.BlockSpec((1, tn), lambda i, j: (0, j))],
        out_specs=pl.BlockSpec((tm, tn), lambda i, j: (i, j)),
        out_shape=jax.ShapeDtypeStruct((s, n), F32),
        compiler_params=_cparams("arbitrary", "arbitrary"),
        name="out_proj_residual",
    )(a, w, x, g)


def _mla_up_kernel(cb_ref, gq_ref, gkv_ref, wq_ref, wkv_ref, gqn_ref, gqp_ref, gkn_ref,
                   gkp_ref, gkps_ref, cost_ref, sint_ref, cosf_ref, sinf_ref,
                   qt_ref, k_ref, vt_ref, *, tk):
    tm = cb_ref.shape[0]
    half = MLA_ROPE // 2
    hw = MLA_NOPE + MLA_ROPE
    cq = cb_ref[:, 0:MLA_Q_LORA]
    cqn = cq * lax.rsqrt(jnp.mean(cq * cq, axis=-1, keepdims=True) + EPS) * gq_ref[...]
    qt = jnp.dot(cqn.astype(BF), wq_ref[...], preferred_element_type=F32).T
    cos_t = cost_ref[...]
    sin_t = sint_ref[...]
    for h in range(MLA_HEADS):
        nope = qt[h * hw:h * hw + MLA_NOPE]
        nope = nope * lax.rsqrt(jnp.mean(nope * nope, axis=0, keepdims=True) + EPS) * gqn_ref[...]
        pe = qt[h * hw + MLA_NOPE:(h + 1) * hw]
        pe = pe * lax.rsqrt(jnp.mean(pe * pe, axis=0, keepdims=True) + EPS) * gqp_ref[...]
        x1 = pe[:half]
        x2 = pe[half:]
        qt_ref[h, 0:MLA_NOPE] = nope.astype(BF)
        qt_ref[h, MLA_NOPE:MLA_NOPE + half] = (x1 * cos_t - x2 * sin_t).astype(BF)
        qt_ref[h, MLA_NOPE + half:hw] = (x1 * sin_t + x2 * cos_t).astype(BF)
        qt_ref[h, hw:MLA_QK_PAD] = jnp.zeros((MLA_QK_PAD - hw, tm), BF)
    ckv = cb_ref[:, MLA_Q_LORA:MLA_Q_LORA + MLA_KV_LORA]
    ckvn = ckv * lax.rsqrt(jnp.mean(ckv * ckv, axis=-1, keepdims=True) + EPS) * gkv_ref[...]
    kv = jnp.dot(ckvn.astype(BF), wkv_ref[...], preferred_element_type=F32)
    kpe = cb_ref[:, LAT_KPE:LAT_KPE + MLA_ROPE]
    kpe_rot = cb_ref[:, LAT_KPE_ROT:LAT_KPE_ROT + MLA_ROPE]
    r = lax.rsqrt(jnp.mean(kpe * kpe, axis=-1, keepdims=True) + EPS)
    kpe_f = (r * (kpe * gkp_ref[...] * cosf_ref[...] + kpe_rot * gkps_ref[...] * sinf_ref[...])).astype(BF)
    zpad = jnp.zeros((tm, MLA_QK_PAD - hw), BF)
    kvw = MLA_NOPE + MLA_V
    for h in range(MLA_HEADS):
        kh = kv[:, h * kvw:h * kvw + MLA_NOPE]
        kh = kh * lax.rsqrt(jnp.mean(kh * kh, axis=-1, keepdims=True) + EPS) * gkn_ref[...]
        k_ref[h, :, 0:MLA_NOPE] = kh.astype(BF)
        k_ref[h, :, MLA_NOPE:hw] = kpe_f
        k_ref[h, :, hw:MLA_QK_PAD] = zpad
        vth = kv[:, h * kvw + MLA_NOPE:(h + 1) * kvw].T
        for b in range(tm // tk):
            vt_ref[h, b] = vth[:, b * tk:(b + 1) * tk].astype(BF)


def _mla_up_call(cb, layer, gq, gkv, wq, wkv, gqn, gqp, gkn, gkp, gkps, cos_t, sin_t, cosf, sinf):
    s = cb.shape[0]
    tm = min(512, s)
    tk = min(ATT_TK, s)
    hw = MLA_NOPE + MLA_ROPE
    half = MLA_ROPE // 2

    def const(shape):
        return pl.BlockSpec(shape, lambda i: tuple(0 for _ in shape))

    return pl.pallas_call(
        functools.partial(_mla_up_kernel, tk=tk),
        grid=(s // tm,),
        in_specs=[pl.BlockSpec((tm, cb.shape[1]), lambda i: (i, 0)),
                  const((1, MLA_Q_LORA)), const((1, MLA_KV_LORA)),
                  pl.BlockSpec((None, MLA_Q_LORA, MLA_HEADS * hw), lambda i: (layer, 0, 0)),
                  pl.BlockSpec((None, MLA_KV_LORA, MLA_HEADS * (MLA_NOPE + MLA_V)),
                               lambda i: (layer, 0, 0)),
                  const((MLA_NOPE, 1)), const((MLA_ROPE, 1)), const((1, MLA_NOPE)),
                  const((1, MLA_ROPE)), const((1, MLA_ROPE)),
                  pl.BlockSpec((half, tm), lambda i: (0, i)),
                  pl.BlockSpec((half, tm), lambda i: (0, i)),
                  pl.BlockSpec((tm, MLA_ROPE), lambda i: (i, 0)),
                  pl.BlockSpec((tm, MLA_ROPE), lambda i: (i, 0))],
        out_specs=[pl.BlockSpec((MLA_HEADS, MLA_QK_PAD, tm), lambda i: (0, 0, i)),
                   pl.BlockSpec((MLA_HEADS, tm, MLA_QK_PAD), lambda i: (0, i, 0)),
                   pl.BlockSpec((MLA_HEADS, tm // tk, MLA_V, tk), lambda i: (0, i, 0, 0))],
        out_shape=[jax.ShapeDtypeStruct((MLA_HEADS, MLA_QK_PAD, s), BF),
                   jax.ShapeDtypeStruct((MLA_HEADS, s, MLA_QK_PAD), BF),
                   jax.ShapeDtypeStruct((MLA_HEADS, s // tk, MLA_V, tk), BF)],
        compiler_params=_cparams("arbitrary"),
        name="mla_up",
    )(cb, gq, gkv, wq, wkv, gqn, gqp, gkn, gkp, gkps, cos_t, sin_t, cosf, sinf)


def _softmax_block(kb, nblk, biases, qts, k_rows, vt_ref, m_ref, l_ref, acc_ref, tk):
    hg = len(qts)
    rows = pl.ds(pl.multiple_of(kb * tk, tk), nblk * tk)
    ss = [jnp.dot(k_rows(g, rows), qts[g], preferred_element_type=F32) for g in range(hg)]
    alphas, ps = [], []
    for g in range(hg):
        s = ss[g] if biases is None else ss[g] + biases[g]
        m_old = m_ref[g]
        m_new = jnp.maximum(m_old, jnp.max(s, axis=0, keepdims=True))
        alpha = jnp.exp2(m_old - m_new)
        p = jnp.exp2(s - m_new)
        l_ref[g] = alpha * l_ref[g] + jnp.sum(p, axis=0, keepdims=True)
        m_ref[g] = m_new
        alphas.append(alpha)
        ps.append(p.astype(BF))
    pvs = []
    for g in range(hg):
        pv = jnp.dot(vt_ref[g, kb], ps[g][0:tk], preferred_element_type=F32)
        for j in range(1, nblk):
            pv += jnp.dot(vt_ref[g, kb + j], ps[g][j * tk:(j + 1) * tk], preferred_element_type=F32)
        pvs.append(pv)
    for g in range(hg):
        acc_ref[g] = alphas[g] * acc_ref[g] + pvs[g]


def _far_sweep(n_far, step, far_blocks):
    n_big = n_far // far_blocks

    def big(i, carry):
        step(i * far_blocks, far_blocks)
        return carry

    lax.fori_loop(0, n_big, big, 0)

    def single(kb, carry):
        step(kb, 1)
        return carry

    lax.fori_loop(n_big * far_blocks, n_far, single, 0)


def _resident_spec(shape, index_map):
    return pl.BlockSpec(shape, index_map, pipeline_mode=pl.Buffered(1))


def _init_softmax_state(m_ref, l_ref, acc_ref):
    m_ref[...] = jnp.full(m_ref.shape, NEG_INF, F32)
    l_ref[...] = jnp.zeros(l_ref.shape, F32)
    acc_ref[...] = jnp.zeros(acc_ref.shape, F32)


def _diff_attn_kernel(q_ref, k_ref, vt_ref, bd_ref, bp_ref, lam_ref, g_ref, o_ref,
                      m_ref, l_ref, acc_ref, *, tq, tk, hg, lam_init):
    qb = pl.program_id(1)
    qts = [jnp.concatenate([q_ref[2 * g], q_ref[2 * g + 1]], axis=1) for g in range(hg)]

    def k_rows(g, rows):
        return k_ref[rows, g * HEAD_W:(g + 1) * HEAD_W]

    def step(kb, nblk, biases=None):
        _softmax_block(kb, nblk, biases, qts, k_rows, vt_ref, m_ref, l_ref, acc_ref, tk)

    _init_softmax_state(m_ref, l_ref, acc_ref)
    _far_sweep(jnp.maximum(qb - 1, 0), step, DIFF_FAR_BLOCKS)

    @pl.when(qb >= 1)
    def _():
        step(qb - 1, 1, [bp_ref[g] for g in range(hg)])

    step(qb, 1, [bd_ref[g] for g in range(hg)])

    lp = lam_ref[...]
    lam = (jnp.exp(jnp.sum(lp[0:1] * lp[1:2], axis=1, keepdims=True))
           - jnp.exp(jnp.sum(lp[2:3] * lp[3:4], axis=1, keepdims=True)) + lam_init)
    for g in range(hg):
        o = acc_ref[g] / l_ref[g]
        d = o[:, :tq] - lam * o[:, tq:]
        ms = jnp.mean(d * d, axis=0, keepdims=True)
        d = d * lax.rsqrt(ms + EPS) * g_ref[...] * (1.0 - lam_init)
        o_ref[:, g * HEAD_W:(g + 1) * HEAD_W] = d.T.astype(o_ref.dtype)


def _diff_attn_call(qpad, k, vt4, bias_diag, bias_prev, lam_p, sub_g, *, head0, lam_init):
    s = k.shape[0]
    tq = tk = min(ATT_TQ, s)
    nkb = s // tk
    n = 2 * tq
    hg = DIFF_HEAD_GROUP
    return pl.pallas_call(
        functools.partial(_diff_attn_kernel, tq=tq, tk=tk, hg=hg, lam_init=lam_init),
        grid=(DA_HEADS // hg, s // tq),
        in_specs=[pl.BlockSpec((2 * hg, HEAD_W, tq), lambda h, i: (h, 0, i)),
                  _resident_spec((s, hg * HEAD_W), lambda h, i: (0, h)),
                  _resident_spec((hg, nkb, HEAD_W, tk), lambda h, i: (head0 // hg + h, 0, 0, 0)),
                  _resident_spec((hg, tk, n), lambda h, i: (h, 0, 0)),
                  _resident_spec((hg, tk, n), lambda h, i: (h, 0, 0)),
                  pl.BlockSpec((4, DA_QK), lambda h, i: (0, 0)),
                  pl.BlockSpec((HEAD_W, 1), lambda h, i: (0, 0))],
        out_specs=pl.BlockSpec((tq, hg * HEAD_W), lambda h, i: (i, h)),
        out_shape=jax.ShapeDtypeStruct((s, DA_HEADS * HEAD_W), BF),
        scratch_shapes=[pltpu.VMEM((hg, 1, n), F32), pltpu.VMEM((hg, 1, n), F32),
                        pltpu.VMEM((hg, HEAD_W, n), F32)],
        compiler_params=_cparams("arbitrary", "arbitrary"),
        name="diff_attention",
    )(qpad, k, vt4, bias_diag, bias_prev, lam_p, sub_g)


def _mla_attn_kernel(q_ref, k_ref, vt_ref, mask_ref, o_ref, m_ref, l_ref, acc_ref, *, tk, hg):
    qb = pl.program_id(1)
    qts = [q_ref[g] for g in range(hg)]

    def k_rows(g, rows):
        return k_ref[g, rows, :]

    def step(kb, nblk, biases=None):
        _softmax_block(kb, nblk, biases, qts, k_rows, vt_ref, m_ref, l_ref, acc_ref, tk)

    _init_softmax_state(m_ref, l_ref, acc_ref)
    _far_sweep(qb, step, MLA_FAR_BLOCKS)
    step(qb, 1, [mask_ref[...]] * hg)
    for g in range(hg):
        o = acc_ref[g] / l_ref[g]
        o_ref[:, g * HEAD_W:(g + 1) * HEAD_W] = o.T.astype(o_ref.dtype)


def _mla_attn_call(qt, k, vt4, mask_diag):
    s = k.shape[1]
    tq = tk = min(ATT_TQ, s)
    nkb = s // tk
    hg = MLA_HEAD_GROUP
    return pl.pallas_call(
        functools.partial(_mla_attn_kernel, tk=tk, hg=hg),
        grid=(MLA_HEADS // hg, s // tq),
        in_specs=[pl.BlockSpec((hg, MLA_QK_PAD, tq), lambda h, i: (h, 0, i)),
                  _resident_spec((hg, s, MLA_QK_PAD), lambda h, i: (h, 0, 0)),
                  _resident_spec((hg, nkb, HEAD_W, tk), lambda h, i: (h, 0, 0, 0)),
                  pl.BlockSpec((tk, tq), lambda h, i: (0, 0))],
        out_specs=pl.BlockSpec((tq, hg * HEAD_W), lambda h, i: (i, h)),
        out_shape=jax.ShapeDtypeStruct((s, MLA_HEADS * HEAD_W), BF),
        scratch_shapes=[pltpu.VMEM((hg, 1, tq), F32), pltpu.VMEM((hg, 1, tq), F32),
                        pltpu.VMEM((hg, HEAD_W, tq), F32)],
        compiler_params=_cparams("arbitrary", "arbitrary"),
        name="mla_attention",
    )(qt, k, vt4, mask_diag)


def _stick_block(kb, mask, qts, k_ref, vt_ref, tri, carry_ref, acc_ref, tk):
    hg = len(qts)
    rows = pl.ds(pl.multiple_of(kb * tk, tk), tk)
    zs = [jnp.dot(k_ref[rows, g * HEAD_W:(g + 1) * HEAD_W], qts[g], preferred_element_type=F32)
          for g in range(hg)]
    logsigs, his, los = [], [], []
    for g in range(hg):
        z = zs[g]
        sp = jnp.maximum(z, 0.0) + jnp.log1p(jnp.exp(-jnp.abs(z)))
        log1m = -sp
        if mask is not None:
            log1m = jnp.where(mask, log1m, 0.0)
        hi = log1m.astype(BF)
        his.append(hi)
        los.append((log1m - hi.astype(F32)).astype(BF))
        logsigs.append(z - sp + carry_ref[g])
        carry_ref[g] += jnp.sum(log1m, axis=0, keepdims=True)
    betweens = [jnp.dot(tri, his[g], preferred_element_type=F32)
                + jnp.dot(tri, los[g], preferred_element_type=F32) for g in range(hg)]
    ws = []
    for g in range(hg):
        w = jnp.exp(logsigs[g] + betweens[g])
        if mask is not None:
            w = jnp.where(mask, w, 0.0)
        ws.append(w.astype(BF))
    pvs = [jnp.dot(vt_ref[g, kb], ws[g], preferred_element_type=F32) for g in range(hg)]
    for g in range(hg):
        acc_ref[g] += pvs[g]


def _stick_kernel(q_ref, k_ref, vt_ref, tri_ref, o_ref, carry_ref, acc_ref, *, tq, tk, hg):
    qb = pl.program_id(1)
    qts = [q_ref[g * HEAD_W:(g + 1) * HEAD_W, :] for g in range(hg)]
    tri = tri_ref[...]
    carry_ref[...] = jnp.zeros(carry_ref.shape, F32)
    acc_ref[...] = jnp.zeros(acc_ref.shape, F32)
    kpos = lax.broadcasted_iota(jnp.int32, (tk, tq), 0)
    qpos = lax.broadcasted_iota(jnp.int32, (tk, tq), 1)
    _stick_block(qb, kpos < qpos, qts, k_ref, vt_ref, tri, carry_ref, acc_ref, tk)

    def more(state):
        i, cmax = state
        return jnp.logical_and(i < qb, cmax > SB_LOGW_UNDERFLOW)

    def earlier(state):
        i, _ = state
        _stick_block(qb - 1 - i, None, qts, k_ref, vt_ref, tri, carry_ref, acc_ref, tk)
        return i + 1, jnp.max(carry_ref[...])

    lax.while_loop(more, earlier, (jnp.int32(0), jnp.max(carry_ref[...])))
    for g in range(hg):
        o_ref[:, g * HEAD_W:(g + 1) * HEAD_W] = acc_ref[g].T.astype(o_ref.dtype)


def _stick_call(qt, k, vt4, tri, *, head0):
    s = k.shape[0]
    tq = tk = min(ATT_TQ, s)
    nkb = s // tk
    hg = SB_HEAD_GROUP
    return pl.pallas_call(
        functools.partial(_stick_kernel, tq=tq, tk=tk, hg=hg),
        grid=(SB_HEADS // hg, s // tq),
        in_specs=[pl.BlockSpec((hg * HEAD_W, tq), lambda h, i: (h, i)),
                  _resident_spec((s, hg * HEAD_W), lambda h, i: (0, h)),
                  _resident_spec((hg, nkb, HEAD_W, tk), lambda h, i: (head0 // hg + h, 0, 0, 0)),
                  pl.BlockSpec((tk, tk), lambda h, i: (0, 0))],
        out_specs=pl.BlockSpec((tq, hg * HEAD_W), lambda h, i: (i, h)),
        out_shape=jax.ShapeDtypeStruct((s, SB_HEADS * HEAD_W), BF),
        scratch_shapes=[pltpu.VMEM((hg, 1, tq), F32), pltpu.VMEM((hg, HEAD_W, tq), F32)],
        compiler_params=_cparams("arbitrary", "arbitrary"),
        name="stick_breaking",
    )(qt, k, vt4, tri)


def _merge_kernel(h_ref, ya_ref, yb_ref, yc_ref, wga_ref, wgb_ref, wgc_ref, wb_ref, o_ref):
    h = h_ref[...]
    acc = None
    for n, (y_ref, wg_ref) in enumerate(((ya_ref, wga_ref), (yb_ref, wgb_ref), (yc_ref, wgc_ref))):
        gate = jax.nn.sigmoid(jnp.dot(h, wg_ref[...], preferred_element_type=F32))
        up = jnp.dot(y_ref[...], wb_ref[n], preferred_element_type=F32)
        acc = gate * up if acc is None else acc + gate * up
    o_ref[...] = acc.astype(o_ref.dtype)


def _merge_call(h, ya, yb, yc, w_all, wb, layer):
    s, d = h.shape
    tm = min(512, s)
    tn = 512
    ysp = pl.BlockSpec((tm, BRANCH_WIDTH), lambda i, j: (i, 0))
    gate0 = SEG_GATE * SEG // tn

    def gate_spec(n):
        return pl.BlockSpec((None, d, tn), lambda i, j: (layer, 0, gate0 + n * (d // tn) + j))

    return pl.pallas_call(
        _merge_kernel,
        grid=(s // tm, d // tn),
        in_specs=[pl.BlockSpec((tm, d), lambda i, j: (i, 0)), ysp, ysp, ysp,
                  gate_spec(0), gate_spec(1), gate_spec(2),
                  pl.BlockSpec((None, N_BRANCHES, BRANCH_WIDTH, tn), lambda i, j: (layer, 0, 0, j))],
        out_specs=pl.BlockSpec((tm, tn), lambda i, j: (i, j)),
        out_shape=jax.ShapeDtypeStruct((s, d), BF),
        compiler_params=_cparams("arbitrary", "arbitrary"),
        name="gated_merge",
    )(h, ya, yb, yc, w_all, w_all, w_all, wb)


def _mlp_kernel(h_ref, w1_ref, w2_ref, x_ref, g_ref, o_ref, acc_ref):
    f = pl.program_id(1)

    @pl.when(f == 0)
    def _():
        acc_ref[...] = jnp.zeros(acc_ref.shape, F32)

    u = jnp.dot(h_ref[...], w1_ref[...], preferred_element_type=F32)
    u = jnp.square(jnp.maximum(u, 0.0)).astype(BF)
    acc_ref[...] += jnp.dot(u, w2_ref[...], preferred_element_type=F32)

    @pl.when(f == pl.num_programs(1) - 1)
    def _():
        o_ref[...] = x_ref[...] + g_ref[...] * acc_ref[...]


def _mlp_call(h, w1, w2, layer, x, g):
    s, d = h.shape
    ff = w1.shape[2]
    tm = min(512, s)
    tf = 1024
    return pl.pallas_call(
        _mlp_kernel,
        grid=(s // tm, ff // tf),
        in_specs=[pl.BlockSpec((tm, d), lambda i, f: (i, 0)),
                  pl.BlockSpec((None, d, tf), lambda i, f: (layer, 0, f)),
                  pl.BlockSpec((None, tf, d), lambda i, f: (layer, f, 0)),
                  pl.BlockSpec((tm, d), lambda i, f: (i, 0)),
                  pl.BlockSpec((1, d), lambda i, f: (0, 0))],
        out_specs=pl.BlockSpec((tm, d), lambda i, f: (i, 0)),
        out_shape=jax.ShapeDtypeStruct((s, d), F32),
        scratch_shapes=[pltpu.VMEM((tm, d), F32)],
        compiler_params=_cparams("arbitrary", "arbitrary"),
        name="sqrelu_mlp",
    )(h, w1, w2, x, g)


def _t5_bucket(rel):
    nb = T5_BUCKETS // 2
    max_exact = nb // 2
    n = jnp.abs(rel)
    large = max_exact + (jnp.log(jnp.maximum(n, 1).astype(F32) / max_exact)
                         / math.log(T5_MAX_DIST / max_exact) * (nb - max_exact)).astype(jnp.int32)
    large = jnp.minimum(large, nb - 1)
    return jnp.where(rel > 0, nb, 0) + jnp.where(n < max_exact, n, large)


def _bias_tiles(t5_bias, tq, tk):
    table = t5_bias.astype(F32)
    kl = jnp.arange(tk)[:, None]
    ql = jnp.arange(tq)[None, :]
    far = table[_t5_bucket(jnp.array(-(tk + tq), jnp.int32))]
    diag = jnp.transpose(table[_t5_bucket(kl - ql)], (2, 0, 1)) - far[:, None, None]
    prev = jnp.transpose(table[_t5_bucket(kl - tk - ql)], (2, 0, 1)) - far[:, None, None]
    allowed = (kl // CHUNK) <= (ql // CHUNK)
    diag = jnp.where(allowed[None], diag * LOG2E, NEG_INF)
    prev = prev * LOG2E
    return (jnp.concatenate([diag, diag], axis=2), jnp.concatenate([prev, prev], axis=2))


def _rope_tables(s):
    half = MLA_ROPE // 2
    inv = ROPE_BASE ** (-jnp.arange(half, dtype=F32) / half)
    ang = jnp.arange(s).astype(F32)[:, None] * inv[None, :]
    cos, sin = jnp.cos(ang), jnp.sin(ang)
    return (cos.T, sin.T, jnp.concatenate([cos, cos], axis=1), jnp.concatenate([sin, sin], axis=1))


def _repack_w_in(w_in):
    depth, d, _ = w_in.shape
    half = MLA_ROPE // 2
    o_kpe = 3 * 1024 + MLA_Q_LORA + MLA_KV_LORA
    o_sq = o_kpe + MLA_ROPE
    w_kpe = w_in[:, :, o_kpe:o_sq]
    w_kpe_rot = jnp.concatenate([-w_kpe[:, :, half:], w_kpe[:, :, :half]], axis=2)
    zcol = jnp.zeros((depth, d, 128 - MLA_ROPE), w_in.dtype)
    return jnp.concatenate([w_in[:, :, :o_sq], zcol, w_kpe_rot, zcol, w_in[:, :, o_sq:]],
                           axis=2).astype(BF)


def kernel(x, c, w_ada, b_ada, norm_mix_g, norm_mlp_g, w_in, diff_qk_g, diff_lambda, diff_subln_g,
           t5_bias, mla_q_norm_g, mla_kv_norm_g, w_q_up, w_kv_up, mla_qk_g, w_branch, w_out,
           w_mlp_in, w_mlp_out):
    b, s, d = x.shape
    assert b == 1 and d == D_MODEL
    depth = w_ada.shape[0]
    tq = tk = min(ATT_TQ, s)
    hw = MLA_NOPE + MLA_ROPE
    half = MLA_ROPE // 2

    mod = _mod_call(jnp.broadcast_to(c, (8, d)).astype(BF), w_ada, b_ada[:, None, :])[:, 0, :]
    mod = mod.reshape(depth, N_MOD, 1, d)

    w_all = _repack_w_in(w_in)
    w_br = w_branch.astype(BF)
    w_o = w_out.astype(BF)
    w_1 = w_mlp_in.astype(BF)
    w_2 = w_mlp_out.astype(BF)
    w_qu = w_q_up.astype(BF)
    w_kvu = w_kv_up.astype(BF)

    bias_diag, bias_prev = _bias_tiles(t5_bias, tq, tk)
    kl = jnp.arange(tk)[:, None]
    ql = jnp.arange(tq)[None, :]
    mask_diag = jnp.where((kl // CHUNK) <= (ql // CHUNK), 0.0, NEG_INF).astype(F32)
    tri = (jnp.arange(tk)[None, :] > jnp.arange(tk)[:, None]).astype(BF)
    gidx = jnp.arange(MXU_TILE) // DA_QK
    gmat = (gidx[:, None] == gidx[None, :]).astype(BF)
    cos_t, sin_t, cosf, sinf = _rope_tables(s)

    ones_v = jnp.ones((2 * BRANCH_WIDTH, 1), F32)
    ones_lat = jnp.ones((1, SEG), F32)
    sb_scale = jnp.full((SB_HEADS * SB_DIM, 1), SB_DIM ** -0.5, F32)
    da_scale = DA_QK ** -0.5 * LOG2E
    mla_scale = hw ** -0.5 * LOG2E

    x2 = x[0]
    for l in range(depth):
        sh1, sc1, g1, sh2, sc2, g2 = (mod[l, i] for i in range(N_MOD))
        lam_init = 0.8 - 0.6 * math.exp(-0.3 * l)

        h = _norm_call(x2, norm_mix_g[l][None], sc1, sh1)

        gq_rows = jnp.tile(diff_qk_g[l, 0], 2 * DA_HEADS)[:, None] * da_scale
        gk_cols = jnp.tile(diff_qk_g[l, 1], 2 * DA_HEADS)[None, :]
        qa = _proj_call(h, w_all, l, (SEG_DQ, 1, 1), gq_rows, layout="qpad", group=DA_QK,
                        name="proj_diff_q")
        ka = _proj_call(h, w_all, l, (SEG_DK, 1, 1), gk_cols, layout="n_gnorm", group=DA_QK,
                        gmat=gmat, name="proj_diff_k")
        vt = _proj_call(h, w_all, l, (SEG_DV, 2, SEG_SV - SEG_DV), ones_v, layout="t4",
                        name="proj_v")
        qc = _proj_call(h, w_all, l, (SEG_SQ, 1, 1), sb_scale, layout="t", name="proj_sb_q")
        kc = _proj_call(h, w_all, l, (SEG_SK, 1, 1), ones_lat, layout="n", name="proj_sb_k")
        cb = _proj_call(h, w_all, l, (SEG_LAT, 1, 1), ones_lat, layout="n", out_dtype=F32,
                        name="proj_mla_latent")
        gqk = mla_qk_g[l]
        qb_t, kb, vb_t = _mla_up_call(
            cb, l, mla_q_norm_g[l][None], mla_kv_norm_g[l][None], w_qu, w_kvu,
            gqk[0, :MLA_NOPE, None] * mla_scale, gqk[0, MLA_NOPE:, None] * mla_scale,
            gqk[1, None, :MLA_NOPE], gqk[1, None, MLA_NOPE:],
            jnp.concatenate([gqk[1, MLA_NOPE + half:], gqk[1, MLA_NOPE:MLA_NOPE + half]])[None],
            cos_t, sin_t, cosf, sinf)

        ya = _diff_attn_call(qa, ka, vt, bias_diag, bias_prev, diff_lambda[l],
                             diff_subln_g[l][:, None], head0=0, lam_init=lam_init)
        yb = _mla_attn_call(qb_t, kb, vb_t, mask_diag)
        yc = _stick_call(qc, kc, vt, tri, head0=DA_HEADS)

        merged = _merge_call(h, ya, yb, yc, w_all, w_br, l)
        x2 = _resid_proj_call(merged, w_o, l, x2, g1)

        h2 = _norm_call(x2, norm_mlp_g[l][None], sc2, sh2)
        x2 = _mlp_call(h2, w_1, w_2, l, x2, g2)
    return x2[None]
```

```python
import functools
import math

import jax
import jax.numpy as jnp
from jax import lax
from jax.experimental import pallas as pl
from jax.experimental.pallas import tpu as pltpu

BF = jnp.bfloat16
F32 = jnp.float32

D_MODEL = 2048
CHUNK = 64
DA_HEADS = 8
DA_QK = 64
DA_V = 128
MLA_HEADS = 8
MLA_Q_LORA = 512
MLA_KV_LORA = 256
MLA_NOPE = 128
MLA_ROPE = 64
MLA_V = 128
MLA_QK_PAD = 256
ROPE_BASE = 10000.0
SB_HEADS = 8
SB_DIM = 128
BRANCH_WIDTH = 1024
N_BRANCHES = 3
T5_BUCKETS = 32
T5_MAX_DIST = 128
D_FF = 4 * D_MODEL
N_MOD = 6
EPS = 1e-6
NEG_INF = -1e30
LOG2E = 1.4426950408889634

HEAD_W = 128
ATT_TQ = 256
ATT_TK = 256
DIFF_FAR_BLOCKS = 2
MLA_FAR_BLOCKS = 4
DIFF_HEAD_GROUP = 4
MLA_HEAD_GROUP = 4
SB_HEAD_GROUP = 4
SB_LOGW_UNDERFLOW = -105.0
V7X_VMEM_LIMIT = 56 * 1024 * 1024

SEG = 1024
HEAD_ROWS = 4 * SEG
TAIL_ROW0 = 3 * SEG + MLA_Q_LORA + MLA_KV_LORA + MLA_ROPE
SEG_DQ, SEG_DK, SEG_DV, SEG_LAT = 0, 1, 2, 3
SEG_SQ, SEG_SK, SEG_SV, SEG_GATE = 0, 1, 2, 3
LAT_KPE = MLA_Q_LORA + MLA_KV_LORA


def _cparams(*sem):
    return pltpu.CompilerParams(dimension_semantics=sem, vmem_limit_bytes=V7X_VMEM_LIMIT)


def _mod_kernel(c_ref, w_ref, b_ref, o_ref):
    w = w_ref[0].astype(BF)
    o_ref[0] = jnp.dot(c_ref[...], w, preferred_element_type=F32) + b_ref[0]


def _mod_call(c8, w_ada, b_ada3):
    depth, d, n = w_ada.shape
    tn = 1024
    return pl.pallas_call(
        _mod_kernel,
        grid=(depth, n // tn),
        in_specs=[pl.BlockSpec((8, d), lambda l, j: (0, 0)),
                  pl.BlockSpec((1, d, tn), lambda l, j: (l, 0, j)),
                  pl.BlockSpec((1, 1, tn), lambda l, j: (l, 0, j))],
        out_specs=pl.BlockSpec((1, 8, tn), lambda l, j: (l, 0, j)),
        out_shape=jax.ShapeDtypeStruct((depth, 8, n), F32),
        compiler_params=_cparams("arbitrary", "arbitrary"),
        name="adaln_mod",
    )(c8, w_ada, b_ada3)


def _norm_kernel(x_ref, g_ref, sc_ref, sh_ref, h_ref, *, transpose):
    x = x_ref[...]
    ms = jnp.mean(x * x, axis=-1, keepdims=True)
    y = x * lax.rsqrt(ms + EPS) * g_ref[...]
    h = y * (1.0 + sc_ref[...]) + sh_ref[...]
    h_ref[...] = (h.T if transpose else h).astype(BF)


def _norm_call(x, g, sc, sh, *, transpose):
    s, d = x.shape
    tm = min(512, s)
    row = pl.BlockSpec((1, d), lambda i: (0, 0))
    if transpose:
        out_shape = jax.ShapeDtypeStruct((d, s), BF)
        out_spec = pl.BlockSpec((d, tm), lambda i: (0, i))
    else:
        out_shape = jax.ShapeDtypeStruct((s, d), BF)
        out_spec = pl.BlockSpec((tm, d), lambda i: (i, 0))
    return pl.pallas_call(
        functools.partial(_norm_kernel, transpose=transpose),
        grid=(s // tm,),
        in_specs=[pl.BlockSpec((tm, d), lambda i: (i, 0)), row, row, row],
        out_specs=out_spec,
        out_shape=out_shape,
        compiler_params=_cparams("arbitrary"),
        name="adaln_rmsnorm",
    )(x, g, sc, sh)


def _proj_kernel(w_ref, xt_ref, rs_ref, o_ref, *, group, layout, tk):
    y = jnp.dot(w_ref[...], xt_ref[...], preferred_element_type=F32)
    tn, tm = y.shape
    if group:
        y3 = y.reshape(tn // group, group, tm)
        ms = jnp.mean(y3 * y3, axis=1, keepdims=True)
        y = (y3 * lax.rsqrt(ms + EPS)).reshape(tn, tm)
    y = y * rs_ref[...]
    if layout == "t":
        o_ref[...] = y.astype(o_ref.dtype)
    elif layout == "n":
        o_ref[...] = y.T.astype(o_ref.dtype)
    elif layout == "t4":
        for a in range(tn // HEAD_W):
            for b in range(tm // tk):
                o_ref[a, b] = y[a * HEAD_W:(a + 1) * HEAD_W, b * tk:(b + 1) * tk].astype(o_ref.dtype)
    elif layout == "qpad":
        first = lax.broadcasted_iota(jnp.int32, (HEAD_W, tm), 0) < DA_QK
        for a in range(tn // HEAD_W):
            ya = y[a * HEAD_W:(a + 1) * HEAD_W]
            o_ref[2 * a] = jnp.where(first, ya, 0.0).astype(o_ref.dtype)
            o_ref[2 * a + 1] = jnp.where(first, 0.0, ya).astype(o_ref.dtype)


def _proj_call(wt, layer, seg0, nseg, ht, rowscale, *, group=0, layout="t", out_dtype=BF, name):
    k, s = ht.shape
    n = nseg * SEG
    tn = SEG
    tm = min(1024, s)
    tk = min(ATT_TK, s)
    if layout == "t":
        out_shape = jax.ShapeDtypeStruct((n, s), out_dtype)
        out_spec = pl.BlockSpec((tn, tm), lambda i, j: (i, j))
    elif layout == "n":
        out_shape = jax.ShapeDtypeStruct((s, n), out_dtype)
        out_spec = pl.BlockSpec((tm, tn), lambda i, j: (j, i))
    elif layout == "t4":
        out_shape = jax.ShapeDtypeStruct((n // HEAD_W, s // tk, HEAD_W, tk), out_dtype)
        out_spec = pl.BlockSpec((tn // HEAD_W, tm // tk, HEAD_W, tk), lambda i, j: (i, j, 0, 0))
    else:
        out_shape = jax.ShapeDtypeStruct((2 * n // HEAD_W, HEAD_W, s), out_dtype)
        out_spec = pl.BlockSpec((2 * tn // HEAD_W, HEAD_W, tm), lambda i, j: (i, 0, j))
    return pl.pallas_call(
        functools.partial(_proj_kernel, group=group, layout=layout, tk=tk),
        grid=(nseg, s // tm),
        in_specs=[pl.BlockSpec((None, tn, k), lambda i, j: (layer, seg0 + i, 0)),
                  pl.BlockSpec((k, tm), lambda i, j: (0, j)),
                  pl.BlockSpec((tn, 1), lambda i, j: (i, 0))],
        out_specs=out_spec,
        out_shape=out_shape,
        compiler_params=_cparams("arbitrary", "arbitrary"),
        name=name,
    )(wt, ht, rowscale)


def _resid_proj_kernel(a_ref, w_ref, x_ref, g_ref, o_ref):
    y = jnp.dot(a_ref[...], w_ref[...], preferred_element_type=F32)
    o_ref[...] = x_ref[...] + g_ref[...] * y


def _resid_proj_call(a, w, layer, x, g):
    s, k = a.shape
    n = w.shape[2]
    tm = min(1024, s)
    tn = min(1024, n)
    return pl.pallas_call(
        _resid_proj_kernel,
        grid=(s // tm, n // tn),
        in_specs=[pl.BlockSpec((tm, k), lambda i, j: (i, 0)),
                  pl.BlockSpec((None, k, tn), lambda i, j: (layer, 0, j)),
                  pl.BlockSpec((tm, tn), lambda i, j: (i, j)),
                  pl.BlockSpec((1, tn), lambda i, j: (0, j))],
        out_specs=pl.BlockSpec((tm, tn), lambda i, j: (i, j)),
        out_shape=jax.ShapeDtypeStruct((s, n), F32),
        compiler_params=_cparams("arbitrary", "arbitrary"),
        name="out_proj_residual",
    )(a, w, x, g)


def _rms_rows(x, g):
    return x * lax.rsqrt(jnp.mean(x * x, axis=0, keepdims=True) + EPS) * g


def _rope_rows(x, cos_t, sin_t):
    half = MLA_ROPE // 2
    x1, x2 = x[:half], x[half:]
    return x1 * cos_t - x2 * sin_t, x1 * sin_t + x2 * cos_t


def _mla_up_kernel(cb_ref, gq_ref, gkv_ref, wq_ref, wk_ref, wv_ref, gqn_ref, gqp_ref, gkn_ref,
                   gkp_ref, cost_ref, sint_ref, qt_ref, k_ref, vt_ref, *, tk):
    tm = cb_ref.shape[1]
    hw = MLA_NOPE + MLA_ROPE
    cos_t = cost_ref[...]
    sin_t = sint_ref[...]
    cqn = _rms_rows(cb_ref[0:MLA_Q_LORA], gq_ref[...]).astype(BF)
    qt = jnp.dot(wq_ref[...], cqn, preferred_element_type=F32)
    qzero = jnp.zeros((MLA_QK_PAD - hw, tm), BF)
    for h in range(MLA_HEADS):
        nope = _rms_rows(qt[h * hw:h * hw + MLA_NOPE], gqn_ref[...])
        pe1, pe2 = _rope_rows(_rms_rows(qt[h * hw + MLA_NOPE:(h + 1) * hw], gqp_ref[...]), cos_t, sin_t)
        qt_ref[h, 0:MLA_NOPE] = nope.astype(BF)
        qt_ref[h, MLA_NOPE:MLA_NOPE + MLA_ROPE // 2] = pe1.astype(BF)
        qt_ref[h, MLA_NOPE + MLA_ROPE // 2:hw] = pe2.astype(BF)
        qt_ref[h, hw:MLA_QK_PAD] = qzero
    ckvn = _rms_rows(cb_ref[MLA_Q_LORA:LAT_KPE], gkv_ref[...]).astype(BF)
    kt = jnp.dot(wk_ref[...], ckvn, preferred_element_type=F32)
    vt = jnp.dot(wv_ref[...], ckvn, preferred_element_type=F32)
    kpe1, kpe2 = _rope_rows(_rms_rows(cb_ref[LAT_KPE:LAT_KPE + MLA_ROPE], gkp_ref[...]), cos_t, sin_t)
    kzero = jnp.zeros((MLA_QK_PAD - hw, tm), F32)
    for h in range(MLA_HEADS):
        kn = _rms_rows(kt[h * MLA_NOPE:(h + 1) * MLA_NOPE], gkn_ref[...])
        k_ref[h] = jnp.concatenate([kn, kpe1, kpe2, kzero], axis=0).T.astype(BF)
        for b in range(tm // tk):
            vt_ref[h, b] = vt[h * MLA_V:(h + 1) * MLA_V, b * tk:(b + 1) * tk].astype(BF)


def _mla_up_call(cbt, layer, gq, gkv, wq, wk, wv, gqn, gqp, gkn, gkp, cos_t, sin_t):
    s = cbt.shape[1]
    tm = min(512, s)
    tk = min(ATT_TK, s)
    hw = MLA_NOPE + MLA_ROPE
    half = MLA_ROPE // 2

    def const(shape):
        return pl.BlockSpec(shape, lambda i: tuple(0 for _ in shape))

    def layer_w(shape):
        return pl.BlockSpec((None,) + shape, lambda i: (layer, 0, 0))

    return pl.pallas_call(
        functools.partial(_mla_up_kernel, tk=tk),
        grid=(s // tm,),
        in_specs=[pl.BlockSpec((cbt.shape[0], tm), lambda i: (0, i)),
                  const((MLA_Q_LORA, 1)), const((MLA_KV_LORA, 1)),
                  layer_w((MLA_HEADS * hw, MLA_Q_LORA)),
                  layer_w((MLA_HEADS * MLA_NOPE, MLA_KV_LORA)),
                  layer_w((MLA_HEADS * MLA_V, MLA_KV_LORA)),
                  const((MLA_NOPE, 1)), const((MLA_ROPE, 1)), const((MLA_NOPE, 1)),
                  const((MLA_ROPE, 1)),
                  pl.BlockSpec((half, tm), lambda i: (0, i)),
                  pl.BlockSpec((half, tm), lambda i: (0, i))],
        out_specs=[pl.BlockSpec((MLA_HEADS, MLA_QK_PAD, tm), lambda i: (0, 0, i)),
                   pl.BlockSpec((MLA_HEADS, tm, MLA_QK_PAD), lambda i: (0, i, 0)),
                   pl.BlockSpec((MLA_HEADS, tm // tk, MLA_V, tk), lambda i: (0, i, 0, 0))],
        out_shape=[jax.ShapeDtypeStruct((MLA_HEADS, MLA_QK_PAD, s), BF),
                   jax.ShapeDtypeStruct((MLA_HEADS, s, MLA_QK_PAD), BF),
                   jax.ShapeDtypeStruct((MLA_HEADS, s // tk, MLA_V, tk), BF)],
        compiler_params=_cparams("arbitrary"),
        name="mla_up",
    )(cbt, gq, gkv, wq, wk, wv, gqn, gqp, gkn, gkp, cos_t, sin_t)


def _logits_phase(kb, nblk, qts, k_rows, tk):
    rows = pl.ds(pl.multiple_of(kb * tk, tk), nblk * tk)
    return [jnp.dot(k_rows(g, rows), qts[g], preferred_element_type=F32) for g in range(len(qts))]


def _update_phase(kb, nblk, logits, colmax, vt_ref, m_ref, l_ref, acc_ref, tk):
    hg = len(logits)
    alphas, ps = [], []
    for g in range(hg):
        m_old = m_ref[g]
        m_new = jnp.maximum(m_old, colmax[g])
        alpha = jnp.exp2(m_old - m_new)
        p = jnp.exp2(logits[g] - m_new)
        l_ref[g] = alpha * l_ref[g] + jnp.sum(p, axis=0, keepdims=True)
        m_ref[g] = m_new
        alphas.append(alpha)
        ps.append(p.astype(BF))
    pvs = []
    for g in range(hg):
        pv = jnp.dot(vt_ref[g, kb], ps[g][0:tk], preferred_element_type=F32)
        for j in range(1, nblk):
            pv += jnp.dot(vt_ref[g, kb + j], ps[g][j * tk:(j + 1) * tk], preferred_element_type=F32)
        pvs.append(pv)
    for g in range(hg):
        acc_ref[g] = alphas[g] * acc_ref[g] + pvs[g]


def _softmax_block(kb, nblk, biases, qts, k_rows, vt_ref, m_ref, l_ref, acc_ref, tk):
    ss = _logits_phase(kb, nblk, qts, k_rows, tk)
    if biases is not None:
        ss = [s + b for s, b in zip(ss, biases)]
    colmax = [jnp.max(s, axis=0, keepdims=True) for s in ss]
    _update_phase(kb, nblk, ss, colmax, vt_ref, m_ref, l_ref, acc_ref, tk)


def _far_sweep(n_far, step, far_blocks, park, consume):
    n_big = n_far // far_blocks

    @pl.when(n_big > 0)
    def _():
        park(0, 0)
        n_pairs = (n_big - 1) // 2

        def pair(j, carry):
            kb = 2 * j * far_blocks
            park(kb + far_blocks, 1)
            consume(kb, 0)
            park(kb + 2 * far_blocks, 0)
            consume(kb + far_blocks, 1)
            return carry

        lax.fori_loop(0, n_pairs, pair, 0)
        kb = 2 * n_pairs * far_blocks
        left = n_big - 2 * n_pairs

        @pl.when(left == 2)
        def _():
            park(kb + far_blocks, 1)
            consume(kb, 0)
            consume(kb + far_blocks, 1)

        @pl.when(left == 1)
        def _():
            consume(kb, 0)

    done = n_big * far_blocks
    size = far_blocks // 2
    while size >= 1:
        take = ((n_far - done) // size) * size

        @pl.when(take > 0)
        def _(done=done, size=size):
            step(done, size)

        done = done + take
        size //= 2


def _pipeline_fns(far_blocks, qts, k_rows, vt_ref, m_ref, l_ref, acc_ref, s_refs, mx_refs, tk):
    def park(kb, slot):
        for g, s in enumerate(_logits_phase(kb, far_blocks, qts, k_rows, tk)):
            s_refs[slot][g] = s
            mx_refs[slot][g] = jnp.max(s, axis=0, keepdims=True)

    def consume(kb, slot):
        hg = len(qts)
        _update_phase(kb, far_blocks, [s_refs[slot][g] for g in range(hg)],
                      [mx_refs[slot][g] for g in range(hg)], vt_ref, m_ref, l_ref, acc_ref, tk)

    return park, consume


def _resident_spec(shape, index_map):
    return pl.BlockSpec(shape, index_map, pipeline_mode=pl.Buffered(1))


def _pipeline_scratch(hg, rows, n):
    return [pltpu.VMEM((hg, rows, n), F32), pltpu.VMEM((hg, rows, n), F32),
            pltpu.VMEM((hg, 1, n), F32), pltpu.VMEM((hg, 1, n), F32)]


def _init_softmax_state(m_ref, l_ref, acc_ref):
    m_ref[...] = jnp.full(m_ref.shape, NEG_INF, F32)
    l_ref[...] = jnp.zeros(l_ref.shape, F32)
    acc_ref[...] = jnp.zeros(acc_ref.shape, F32)


def _diff_attn_kernel(q_ref, k_ref, vt_ref, bn_ref, lam_ref, g_ref, o_ref,
                      m_ref, l_ref, acc_ref, sa_ref, sb_ref, mxa_ref, mxb_ref,
                      *, tq, tk, hg, lam_init):
    qb = pl.program_id(1)
    qts = [jnp.concatenate([q_ref[2 * g], q_ref[2 * g + 1]], axis=1) for g in range(hg)]

    def k_rows(g, rows):
        return k_ref[rows, g * HEAD_W:(g + 1) * HEAD_W]

    def step(kb, nblk, biases=None):
        _softmax_block(kb, nblk, biases, qts, k_rows, vt_ref, m_ref, l_ref, acc_ref, tk)

    _init_softmax_state(m_ref, l_ref, acc_ref)
    park, consume = _pipeline_fns(DIFF_FAR_BLOCKS, qts, k_rows, vt_ref, m_ref, l_ref, acc_ref,
                                  (sa_ref, sb_ref), (mxa_ref, mxb_ref), tk)
    _far_sweep(jnp.maximum(qb - 1, 0), step, DIFF_FAR_BLOCKS, park, consume)

    @pl.when(qb >= 1)
    def _():
        step(qb - 1, 2, [bn_ref[g] for g in range(hg)])

    @pl.when(qb == 0)
    def _():
        step(0, 1, [bn_ref[g, tk:, :] for g in range(hg)])

    lp = lam_ref[...]
    lam = (jnp.exp(jnp.sum(lp[0:1] * lp[1:2], axis=1, keepdims=True))
           - jnp.exp(jnp.sum(lp[2:3] * lp[3:4], axis=1, keepdims=True)) + lam_init)
    for g in range(hg):
        o = acc_ref[g] / l_ref[g]
        d = o[:, :tq] - lam * o[:, tq:]
        ms = jnp.mean(d * d, axis=0, keepdims=True)
        d = d * lax.rsqrt(ms + EPS) * g_ref[...] * (1.0 - lam_init)
        o_ref[:, g * HEAD_W:(g + 1) * HEAD_W] = d.T.astype(o_ref.dtype)


def _diff_attn_call(qpad, k, vt4, bias_near, lam_p, sub_g, *, lam_init):
    s = k.shape[0]
    tq = tk = min(ATT_TQ, s)
    nkb = s // tk
    n = 2 * tq
    hg = DIFF_HEAD_GROUP
    return pl.pallas_call(
        functools.partial(_diff_attn_kernel, tq=tq, tk=tk, hg=hg, lam_init=lam_init),
        grid=(DA_HEADS // hg, s // tq),
        in_specs=[pl.BlockSpec((2 * hg, HEAD_W, tq), lambda h, i: (h, 0, i)),
                  _resident_spec((s, hg * HEAD_W), lambda h, i: (0, h)),
                  _resident_spec((hg, nkb, HEAD_W, tk), lambda h, i: (h, 0, 0, 0)),
                  _resident_spec((hg, 2 * tk, n), lambda h, i: (h, 0, 0)),
                  pl.BlockSpec((4, DA_QK), lambda h, i: (0, 0)),
                  pl.BlockSpec((HEAD_W, 1), lambda h, i: (0, 0))],
        out_specs=pl.BlockSpec((tq, hg * HEAD_W), lambda h, i: (i, h)),
        out_shape=jax.ShapeDtypeStruct((s, DA_HEADS * HEAD_W), BF),
        scratch_shapes=[pltpu.VMEM((hg, 1, n), F32), pltpu.VMEM((hg, 1, n), F32),
                        pltpu.VMEM((hg, HEAD_W, n), F32)] + _pipeline_scratch(hg, DIFF_FAR_BLOCKS * tk, n),
        compiler_params=_cparams("arbitrary", "arbitrary"),
        name="diff_attention",
    )(qpad, k, vt4, bias_near, lam_p, sub_g)


def _mla_attn_kernel(q_ref, k_ref, vt_ref, mask_ref, o_ref, m_ref, l_ref, acc_ref,
                     sa_ref, sb_ref, mxa_ref, mxb_ref, *, tk, hg):
    qb = pl.program_id(1)
    qts = [q_ref[g] for g in range(hg)]

    def k_rows(g, rows):
        return k_ref[g, rows, :]

    def step(kb, nblk, biases=None):
        _softmax_block(kb, nblk, biases, qts, k_rows, vt_ref, m_ref, l_ref, acc_ref, tk)

    _init_softmax_state(m_ref, l_ref, acc_ref)
    park, consume = _pipeline_fns(MLA_FAR_BLOCKS, qts, k_rows, vt_ref, m_ref, l_ref, acc_ref,
                                  (sa_ref, sb_ref), (mxa_ref, mxb_ref), tk)
    _far_sweep(qb, step, MLA_FAR_BLOCKS, park, consume)
    step(qb, 1, [mask_ref[...]] * hg)
    for g in range(hg):
        o = acc_ref[g] / l_ref[g]
        o_ref[:, g * HEAD_W:(g + 1) * HEAD_W] = o.T.astype(o_ref.dtype)


def _mla_attn_call(qt, k, vt4, mask_diag):
    s = k.shape[1]
    tq = tk = min(ATT_TQ, s)
    nkb = s // tk
    hg = MLA_HEAD_GROUP
    return pl.pallas_call(
        functools.partial(_mla_attn_kernel, tk=tk, hg=hg),
        grid=(MLA_HEADS // hg, s // tq),
        in_specs=[pl.BlockSpec((hg, MLA_QK_PAD, tq), lambda h, i: (h, 0, i)),
                  _resident_spec((hg, s, MLA_QK_PAD), lambda h, i: (h, 0, 0)),
                  _resident_spec((hg, nkb, HEAD_W, tk), lambda h, i: (h, 0, 0, 0)),
                  pl.BlockSpec((tk, tq), lambda h, i: (0, 0))],
        out_specs=pl.BlockSpec((tq, hg * HEAD_W), lambda h, i: (i, h)),
        out_shape=jax.ShapeDtypeStruct((s, MLA_HEADS * HEAD_W), BF),
        scratch_shapes=[pltpu.VMEM((hg, 1, tq), F32), pltpu.VMEM((hg, 1, tq), F32),
                        pltpu.VMEM((hg, HEAD_W, tq), F32)] + _pipeline_scratch(hg, MLA_FAR_BLOCKS * tk, tq),
        compiler_params=_cparams("arbitrary", "arbitrary"),
        name="mla_attention",
    )(qt, k, vt4, mask_diag)


def _stick_block(kb, mask, qts, k_ref, vt_ref, tri, carry_ref, acc_ref, tk):
    hg = len(qts)
    rows = pl.ds(pl.multiple_of(kb * tk, tk), tk)
    zs = [jnp.dot(k_ref[rows, g * HEAD_W:(g + 1) * HEAD_W], qts[g], preferred_element_type=F32)
          for g in range(hg)]
    logsigs, his, los = [], [], []
    for g in range(hg):
        z = zs[g]
        sp = jnp.maximum(z, 0.0) + jnp.log1p(jnp.exp(-jnp.abs(z)))
        log1m = -sp
        if mask is not None:
            log1m = jnp.where(mask, log1m, 0.0)
        hi = log1m.astype(BF)
        his.append(hi)
        los.append((log1m - hi.astype(F32)).astype(BF))
        logsigs.append(z - sp + carry_ref[g])
        carry_ref[g] += jnp.sum(log1m, axis=0, keepdims=True)
    betweens = [jnp.dot(tri, his[g], preferred_element_type=F32)
                + jnp.dot(tri, los[g], preferred_element_type=F32) for g in range(hg)]
    ws = []
    for g in range(hg):
        w = jnp.exp(logsigs[g] + betweens[g])
        if mask is not None:
            w = jnp.where(mask, w, 0.0)
        ws.append(w.astype(BF))
    pvs = [jnp.dot(vt_ref[g, kb], ws[g], preferred_element_type=F32) for g in range(hg)]
    for g in range(hg):
        acc_ref[g] += pvs[g]


def _stick_kernel(q_ref, k_ref, vt_ref, tri_ref, o_ref, carry_ref, acc_ref, *, tq, tk, hg):
    qb = pl.program_id(1)
    qts = [q_ref[g * HEAD_W:(g + 1) * HEAD_W, :] for g in range(hg)]
    tri = tri_ref[...]
    carry_ref[...] = jnp.zeros(carry_ref.shape, F32)
    acc_ref[...] = jnp.zeros(acc_ref.shape, F32)
    kpos = lax.broadcasted_iota(jnp.int32, (tk, tq), 0)
    qpos = lax.broadcasted_iota(jnp.int32, (tk, tq), 1)
    _stick_block(qb, kpos < qpos, qts, k_ref, vt_ref, tri, carry_ref, acc_ref, tk)

    def more(state):
        i, cmax = state
        return jnp.logical_and(i < qb, cmax > SB_LOGW_UNDERFLOW)

    def earlier(state):
        i, _ = state
        _stick_block(qb - 1 - i, None, qts, k_ref, vt_ref, tri, carry_ref, acc_ref, tk)
        return i + 1, jnp.max(carry_ref[...])

    lax.while_loop(more, earlier, (jnp.int32(0), jnp.max(carry_ref[...])))
    for g in range(hg):
        o_ref[:, g * HEAD_W:(g + 1) * HEAD_W] = acc_ref[g].T.astype(o_ref.dtype)


def _stick_call(qt, k, vt4, tri):
    s = k.shape[0]
    tq = tk = min(ATT_TQ, s)
    nkb = s // tk
    hg = SB_HEAD_GROUP
    return pl.pallas_call(
        functools.partial(_stick_kernel, tq=tq, tk=tk, hg=hg),
        grid=(SB_HEADS // hg, s // tq),
        in_specs=[pl.BlockSpec((hg * HEAD_W, tq), lambda h, i: (h, i)),
                  _resident_spec((s, hg * HEAD_W), lambda h, i: (0, h)),
                  _resident_spec((hg, nkb, HEAD_W, tk), lambda h, i: (h, 0, 0, 0)),
                  pl.BlockSpec((tk, tk), lambda h, i: (0, 0))],
        out_specs=pl.BlockSpec((tq, hg * HEAD_W), lambda h, i: (i, h)),
        out_shape=jax.ShapeDtypeStruct((s, SB_HEADS * HEAD_W), BF),
        scratch_shapes=[pltpu.VMEM((hg, 1, tq), F32), pltpu.VMEM((hg, HEAD_W, tq), F32)],
        compiler_params=_cparams("arbitrary", "arbitrary"),
        name="stick_breaking",
    )(qt, k, vt4, tri)


def _merge_kernel(ht_ref, ya_ref, yb_ref, yc_ref, wga_ref, wgb_ref, wgc_ref, wb_ref, o_ref):
    ht = ht_ref[...]
    acc = None
    for n, (y_ref, wg_ref) in enumerate(((ya_ref, wga_ref), (yb_ref, wgb_ref), (yc_ref, wgc_ref))):
        gate = jax.nn.sigmoid(jnp.dot(wg_ref[...], ht, preferred_element_type=F32)).T
        up = jnp.dot(y_ref[...], wb_ref[n], preferred_element_type=F32)
        acc = gate * up if acc is None else acc + gate * up
    o_ref[...] = acc.astype(o_ref.dtype)


def _merge_call(ht, ya, yb, yc, wt_tail, wb, layer):
    d, s = ht.shape
    tm = min(512, s)
    tn = 512
    ysp = pl.BlockSpec((tm, BRANCH_WIDTH), lambda i, j: (i, 0))

    def gate_spec(n):
        blk0 = (SEG_GATE * SEG + n * d) // tn
        return pl.BlockSpec((None, tn, d), lambda i, j: (layer, blk0 + j, 0))

    return pl.pallas_call(
        _merge_kernel,
        grid=(s // tm, d // tn),
        in_specs=[pl.BlockSpec((d, tm), lambda i, j: (0, i)), ysp, ysp, ysp,
                  gate_spec(0), gate_spec(1), gate_spec(2),
                  pl.BlockSpec((None, N_BRANCHES, BRANCH_WIDTH, tn), lambda i, j: (layer, 0, 0, j))],
        out_specs=pl.BlockSpec((tm, tn), lambda i, j: (i, j)),
        out_shape=jax.ShapeDtypeStruct((s, d), BF),
        compiler_params=_cparams("arbitrary", "arbitrary"),
        name="gated_merge",
    )(ht, ya, yb, yc, wt_tail, wt_tail, wt_tail, wb)


def _mlp_kernel(h_ref, w1_ref, w2_ref, x_ref, g_ref, o_ref, acc_ref):
    f = pl.program_id(1)

    @pl.when(f == 0)
    def _():
        acc_ref[...] = jnp.zeros(acc_ref.shape, F32)

    u = jnp.dot(h_ref[...], w1_ref[...], preferred_element_type=F32)
    u = jnp.square(jnp.maximum(u, 0.0)).astype(BF)
    acc_ref[...] += jnp.dot(u, w2_ref[...], preferred_element_type=F32)

    @pl.when(f == pl.num_programs(1) - 1)
    def _():
        o_ref[...] = x_ref[...] + g_ref[...] * acc_ref[...]


def _mlp_call(h, w1, w2, layer, x, g):
    s, d = h.shape
    ff = w1.shape[2]
    tm = min(512, s)
    tf = 1024
    return pl.pallas_call(
        _mlp_kernel,
        grid=(s // tm, ff // tf),
        in_specs=[pl.BlockSpec((tm, d), lambda i, f: (i, 0)),
                  pl.BlockSpec((None, d, tf), lambda i, f: (layer, 0, f)),
                  pl.BlockSpec((None, tf, d), lambda i, f: (layer, f, 0)),
                  pl.BlockSpec((tm, d), lambda i, f: (i, 0)),
                  pl.BlockSpec((1, d), lambda i, f: (0, 0))],
        out_specs=pl.BlockSpec((tm, d), lambda i, f: (i, 0)),
        out_shape=jax.ShapeDtypeStruct((s, d), F32),
        scratch_shapes=[pltpu.VMEM((tm, d), F32)],
        compiler_params=_cparams("arbitrary", "arbitrary"),
        name="sqrelu_mlp",
    )(h, w1, w2, x, g)


def _t5_bucket(rel):
    nb = T5_BUCKETS // 2
    max_exact = nb // 2
    n = jnp.abs(rel)
    large = max_exact + (jnp.log(jnp.maximum(n, 1).astype(F32) / max_exact)
                         / math.log(T5_MAX_DIST / max_exact) * (nb - max_exact)).astype(jnp.int32)
    large = jnp.minimum(large, nb - 1)
    return jnp.where(rel > 0, nb, 0) + jnp.where(n < max_exact, n, large)


def _bias_tiles(t5_bias, tq, tk):
    table = t5_bias.astype(F32)

    def lookup(bucket):
        hit = bucket[..., None, None] == jnp.arange(T5_BUCKETS)[:, None]
        return jnp.sum(jnp.where(hit, table, 0.0), axis=-2)

    kl = jnp.arange(tk)[:, None]
    ql = jnp.arange(tq)[None, :]
    far = lookup(_t5_bucket(jnp.array(-(tk + tq), jnp.int32)))
    diag = jnp.transpose(lookup(_t5_bucket(kl - ql)) - far, (2, 0, 1))
    prev = jnp.transpose(lookup(_t5_bucket(kl - tk - ql)) - far, (2, 0, 1))
    allowed = (kl // CHUNK) <= (ql // CHUNK)
    diag = jnp.where(allowed[None], diag * LOG2E, NEG_INF)
    near = jnp.concatenate([prev * LOG2E, diag], axis=1)
    return jnp.concatenate([near, near], axis=2)


def _rope_tables(s):
    half = MLA_ROPE // 2
    inv = ROPE_BASE ** (-jnp.arange(half, dtype=F32) / half)
    ang = jnp.arange(s).astype(F32)[:, None] * inv[None, :]
    return jnp.cos(ang).T, jnp.sin(ang).T


def kernel(x, c, w_ada, b_ada, norm_mix_g, norm_mlp_g, w_in, diff_qk_g, diff_lambda, diff_subln_g,
           t5_bias, mla_q_norm_g, mla_kv_norm_g, w_q_up, w_kv_up, mla_qk_g, w_branch, w_out,
           w_mlp_in, w_mlp_out):
    b, s, d = x.shape
    assert b == 1 and d == D_MODEL
    depth = w_ada.shape[0]
    tq = tk = min(ATT_TQ, s)
    hw = MLA_NOPE + MLA_ROPE

    mod = _mod_call(jnp.broadcast_to(c, (8, d)).astype(BF), w_ada, b_ada[:, None, :])[:, 0, :]
    mod = mod.reshape(depth, N_MOD, 1, d)

    wt_head = jnp.swapaxes(w_in[:, :, :HEAD_ROWS], 1, 2).astype(BF)
    wt_tail = jnp.swapaxes(w_in[:, :, TAIL_ROW0:], 1, 2).astype(BF)
    w_br = w_branch.astype(BF)
    w_o = w_out.astype(BF)
    w_1 = w_mlp_in.astype(BF)
    w_2 = w_mlp_out.astype(BF)
    wt_qu = jnp.swapaxes(w_q_up, 1, 2).astype(BF)
    wkv = w_kv_up.reshape(depth, MLA_KV_LORA, MLA_HEADS, MLA_NOPE + MLA_V)
    wt_kn = jnp.swapaxes(wkv[..., :MLA_NOPE].reshape(depth, MLA_KV_LORA, -1), 1, 2).astype(BF)
    wt_v = jnp.swapaxes(wkv[..., MLA_NOPE:].reshape(depth, MLA_KV_LORA, -1), 1, 2).astype(BF)

    bias_near = _bias_tiles(t5_bias, tq, tk)
    kl = jnp.arange(tk)[:, None]
    ql = jnp.arange(tq)[None, :]
    mask_diag = jnp.where((kl // CHUNK) <= (ql // CHUNK), 0.0, NEG_INF).astype(F32)
    tri = (jnp.arange(tk)[None, :] > jnp.arange(tk)[:, None]).astype(BF)
    cos_t, sin_t = _rope_tables(s)

    ones_col = jnp.ones((SEG, 1), F32)
    sb_scale = jnp.full((SB_HEADS * SB_DIM, 1), SB_DIM ** -0.5, F32)
    da_scale = DA_QK ** -0.5 * LOG2E
    mla_scale = hw ** -0.5 * LOG2E

    x2 = x[0]
    for l in range(depth):
        sh1, sc1, g1, sh2, sc2, g2 = (mod[l, i] for i in range(N_MOD))
        lam_init = 0.8 - 0.6 * math.exp(-0.3 * l)

        ht = _norm_call(x2, norm_mix_g[l][None], sc1, sh1, transpose=True)

        gq_rows = jnp.tile(diff_qk_g[l, 0], 2 * DA_HEADS)[:, None] * da_scale
        gk_rows = jnp.tile(diff_qk_g[l, 1], 2 * DA_HEADS)[:, None]
        qa = _proj_call(wt_head, l, SEG_DQ, 1, ht, gq_rows, group=DA_QK, layout="qpad",
                        name="proj_diff_q")
        ka = _proj_call(wt_head, l, SEG_DK, 1, ht, gk_rows, group=DA_QK, layout="n",
                        name="proj_diff_k")
        va = _proj_call(wt_head, l, SEG_DV, 1, ht, ones_col, layout="t4", name="proj_diff_v")
        qc = _proj_call(wt_tail, l, SEG_SQ, 1, ht, sb_scale, layout="t", name="proj_sb_q")
        kc = _proj_call(wt_tail, l, SEG_SK, 1, ht, ones_col, layout="n", name="proj_sb_k")
        vc = _proj_call(wt_tail, l, SEG_SV, 1, ht, ones_col, layout="t4", name="proj_sb_v")
        cbt = _proj_call(wt_head, l, SEG_LAT, 1, ht, ones_col, layout="t", out_dtype=F32,
                         name="proj_mla_latent")
        gqk = mla_qk_g[l]
        qb_t, kb, vb_t = _mla_up_call(
            cbt, l, mla_q_norm_g[l][:, None], mla_kv_norm_g[l][:, None], wt_qu, wt_kn, wt_v,
            gqk[0, :MLA_NOPE, None] * mla_scale, gqk[0, MLA_NOPE:, None] * mla_scale,
            gqk[1, :MLA_NOPE, None], gqk[1, MLA_NOPE:, None], cos_t, sin_t)

        ya = _diff_attn_call(qa, ka, va, bias_near, diff_lambda[l], diff_subln_g[l][:, None],
                             lam_init=lam_init)
        yb = _mla_attn_call(qb_t, kb, vb_t, mask_diag)
        yc = _stick_call(qc, kc, vc, tri)

        merged = _merge_call(ht, ya, yb, yc, wt_tail, w_br, l)
        x2 = _resid_proj_call(merged, w_o, l, x2, g1)

        h2 = _norm_call(x2, norm_mlp_g[l][None], sc2, sh2, transpose=False)
        x2 = _mlp_call(h2, w_1, w_2, l, x2, g2)
    return x2[None]
```

```python
import functools
import math

import jax
import jax.numpy as jnp
from jax import lax
from jax.experimental import pallas as pl
from jax.experimental.pallas import tpu as pltpu

BF = jnp.bfloat16
F32 = jnp.float32

D_MODEL = 2048
CHUNK = 64
DA_HEADS = 8
DA_QK = 64
DA_V = 128
MLA_HEADS = 8
MLA_Q_LORA = 512
MLA_KV_LORA = 256
MLA_NOPE = 128
MLA_ROPE = 64
MLA_V = 128
MLA_QK_PAD = 256
ROPE_BASE = 10000.0
SB_HEADS = 8
SB_DIM = 128
BRANCH_WIDTH = 1024
N_BRANCHES = 3
T5_BUCKETS = 32
T5_MAX_DIST = 128
D_FF = 4 * D_MODEL
N_MOD = 6
EPS = 1e-6
NEG_INF = -1e30
LOG2E = 1.4426950408889634

HEAD_W = 128
ATT_TQ = 256
ATT_TK = 256
DIFF_FAR_BLOCKS = 2
MLA_FAR_BLOCKS = 4
DIFF_HEAD_GROUP = 4
MLA_HEAD_GROUP = 4
SB_HEAD_GROUP = 4
SB_LOG2W_UNDERFLOW = -152.0
V7X_VMEM_LIMIT = 56 * 1024 * 1024

SEG = 1024
HEAD_ROWS = 4 * SEG
TAIL_ROW0 = 3 * SEG + MLA_Q_LORA + MLA_KV_LORA + MLA_ROPE
SEG_DQ, SEG_DK, SEG_DV, SEG_LAT = 0, 1, 2, 3
SEG_SQ, SEG_SK, SEG_SV, SEG_GATE = 0, 1, 2, 3
LAT_KPE = MLA_Q_LORA + MLA_KV_LORA


def _cparams(*sem):
    return pltpu.CompilerParams(dimension_semantics=sem, vmem_limit_bytes=V7X_VMEM_LIMIT)


def _mod_kernel(c_ref, w_ref, b_ref, o_ref):
    w = w_ref[0].astype(BF)
    o_ref[0] = jnp.dot(c_ref[...], w, preferred_element_type=F32) + b_ref[0]


def _mod_call(c8, w_ada, b_ada3):
    depth, d, n = w_ada.shape
    tn = 1024
    return pl.pallas_call(
        _mod_kernel,
        grid=(depth, n // tn),
        in_specs=[pl.BlockSpec((8, d), lambda l, j: (0, 0)),
                  pl.BlockSpec((1, d, tn), lambda l, j: (l, 0, j)),
                  pl.BlockSpec((1, 1, tn), lambda l, j: (l, 0, j))],
        out_specs=pl.BlockSpec((1, 8, tn), lambda l, j: (l, 0, j)),
        out_shape=jax.ShapeDtypeStruct((depth, 8, n), F32),
        compiler_params=_cparams("arbitrary", "arbitrary"),
        name="adaln_mod",
    )(c8, w_ada, b_ada3)


def _norm_kernel(x_ref, g_ref, sc_ref, sh_ref, h_ref, *, transpose):
    x = x_ref[...]
    ms = jnp.mean(x * x, axis=-1, keepdims=True)
    y = x * lax.rsqrt(ms + EPS) * g_ref[...]
    h = y * (1.0 + sc_ref[...]) + sh_ref[...]
    h_ref[...] = (h.T if transpose else h).astype(BF)


def _norm_call(x, g, sc, sh, *, transpose):
    s, d = x.shape
    tm = min(512, s)
    row = pl.BlockSpec((1, d), lambda i: (0, 0))
    if transpose:
        out_shape = jax.ShapeDtypeStruct((d, s), BF)
        out_spec = pl.BlockSpec((d, tm), lambda i: (0, i))
    else:
        out_shape = jax.ShapeDtypeStruct((s, d), BF)
        out_spec = pl.BlockSpec((tm, d), lambda i: (i, 0))
    return pl.pallas_call(
        functools.partial(_norm_kernel, transpose=transpose),
        grid=(s // tm,),
        in_specs=[pl.BlockSpec((tm, d), lambda i: (i, 0)), row, row, row],
        out_specs=out_spec,
        out_shape=out_shape,
        compiler_params=_cparams("arbitrary"),
        name="adaln_rmsnorm",
    )(x, g, sc, sh)


def _proj_kernel(w_ref, xt_ref, rs_ref, o_ref, *, group, layout, tk):
    y = jnp.dot(w_ref[...], xt_ref[...], preferred_element_type=F32)
    tn, tm = y.shape
    if group:
        y3 = y.reshape(tn // group, group, tm)
        ms = jnp.mean(y3 * y3, axis=1, keepdims=True)
        y = (y3 * lax.rsqrt(ms + EPS)).reshape(tn, tm)
    y = y * rs_ref[...]
    if layout == "t":
        o_ref[...] = y.astype(o_ref.dtype)
    elif layout == "n":
        o_ref[...] = y.T.astype(o_ref.dtype)
    elif layout == "t4":
        for a in range(tn // HEAD_W):
            for b in range(tm // tk):
                o_ref[a, b] = y[a * HEAD_W:(a + 1) * HEAD_W, b * tk:(b + 1) * tk].astype(o_ref.dtype)
    elif layout == "qpad":
        first = lax.broadcasted_iota(jnp.int32, (HEAD_W, tm), 0) < DA_QK
        for a in range(tn // HEAD_W):
            ya = y[a * HEAD_W:(a + 1) * HEAD_W]
            o_ref[2 * a] = jnp.where(first, ya, 0.0).astype(o_ref.dtype)
            o_ref[2 * a + 1] = jnp.where(first, 0.0, ya).astype(o_ref.dtype)


def _proj_call(wt, layer, seg0, nseg, ht, rowscale, *, group=0, layout="t", out_dtype=BF, name):
    k, s = ht.shape
    n = nseg * SEG
    tn = SEG
    tm = min(1024, s)
    tk = min(ATT_TK, s)
    if layout == "t":
        out_shape = jax.ShapeDtypeStruct((n, s), out_dtype)
        out_spec = pl.BlockSpec((tn, tm), lambda i, j: (i, j))
    elif layout == "n":
        out_shape = jax.ShapeDtypeStruct((s, n), out_dtype)
        out_spec = pl.BlockSpec((tm, tn), lambda i, j: (j, i))
    elif layout == "t4":
        out_shape = jax.ShapeDtypeStruct((n // HEAD_W, s // tk, HEAD_W, tk), out_dtype)
        out_spec = pl.BlockSpec((tn // HEAD_W, tm // tk, HEAD_W, tk), lambda i, j: (i, j, 0, 0))
    else:
        out_shape = jax.ShapeDtypeStruct((2 * n // HEAD_W, HEAD_W, s), out_dtype)
        out_spec = pl.BlockSpec((2 * tn // HEAD_W, HEAD_W, tm), lambda i, j: (i, 0, j))
    return pl.pallas_call(
        functools.partial(_proj_kernel, group=group, layout=layout, tk=tk),
        grid=(nseg, s // tm),
        in_specs=[pl.BlockSpec((None, tn, k), lambda i, j: (layer, seg0 + i, 0)),
                  pl.BlockSpec((k, tm), lambda i, j: (0, j)),
                  pl.BlockSpec((tn, 1), lambda i, j: (i, 0))],
        out_specs=out_spec,
        out_shape=out_shape,
        compiler_params=_cparams("arbitrary", "arbitrary"),
        name=name,
    )(wt, ht, rowscale)


def _resid_proj_kernel(a_ref, w_ref, x_ref, g_ref, o_ref):
    y = jnp.dot(a_ref[...], w_ref[...], preferred_element_type=F32)
    o_ref[...] = x_ref[...] + g_ref[...] * y


def _resid_proj_call(a, w, layer, x, g):
    s, k = a.shape
    n = w.shape[2]
    tm = min(1024, s)
    tn = min(1024, n)
    return pl.pallas_call(
        _resid_proj_kernel,
        grid=(s // tm, n // tn),
        in_specs=[pl.BlockSpec((tm, k), lambda i, j: (i, 0)),
                  pl.BlockSpec((None, k, tn), lambda i, j: (layer, 0, j)),
                  pl.BlockSpec((tm, tn), lambda i, j: (i, j)),
                  pl.BlockSpec((1, tn), lambda i, j: (0, j))],
        out_specs=pl.BlockSpec((tm, tn), lambda i, j: (i, j)),
        out_shape=jax.ShapeDtypeStruct((s, n), F32),
        compiler_params=_cparams("arbitrary", "arbitrary"),
        name="out_proj_residual",
    )(a, w, x, g)


def _rms_rows(x, g):
    return x * lax.rsqrt(jnp.mean(x * x, axis=0, keepdims=True) + EPS) * g


def _rope_rows(x, cos_t, sin_t):
    half = MLA_ROPE // 2
    x1, x2 = x[:half], x[half:]
    return x1 * cos_t - x2 * sin_t, x1 * sin_t + x2 * cos_t


def _mla_up_kernel(cb_ref, gq_ref, gkv_ref, wq_ref, wk_ref, wv_ref, gqn_ref, gqp_ref, gkn_ref,
                   gkp_ref, cost_ref, sint_ref, qt_ref, k_ref, vt_ref, *, tk):
    tm = cb_ref.shape[1]
    hw = MLA_NOPE + MLA_ROPE
    cos_t = cost_ref[...]
    sin_t = sint_ref[...]
    cqn = _rms_rows(cb_ref[0:MLA_Q_LORA], gq_ref[...]).astype(BF)
    qt = jnp.dot(wq_ref[...], cqn, preferred_element_type=F32)
    qzero = jnp.zeros((MLA_QK_PAD - hw, tm), BF)
    for h in range(MLA_HEADS):
        nope = _rms_rows(qt[h * hw:h * hw + MLA_NOPE], gqn_ref[...])
        pe1, pe2 = _rope_rows(_rms_rows(qt[h * hw + MLA_NOPE:(h + 1) * hw], gqp_ref[...]), cos_t, sin_t)
        qt_ref[h, 0:MLA_NOPE] = nope.astype(BF)
        qt_ref[h, MLA_NOPE:MLA_NOPE + MLA_ROPE // 2] = pe1.astype(BF)
        qt_ref[h, MLA_NOPE + MLA_ROPE // 2:hw] = pe2.astype(BF)
        qt_ref[h, hw:MLA_QK_PAD] = qzero
    ckvn = _rms_rows(cb_ref[MLA_Q_LORA:LAT_KPE], gkv_ref[...]).astype(BF)
    kt = jnp.dot(wk_ref[...], ckvn, preferred_element_type=F32)
    vt = jnp.dot(wv_ref[...], ckvn, preferred_element_type=F32)
    kpe1, kpe2 = _rope_rows(_rms_rows(cb_ref[LAT_KPE:LAT_KPE + MLA_ROPE], gkp_ref[...]), cos_t, sin_t)
    kzero = jnp.zeros((MLA_QK_PAD - hw, tm), F32)
    for h in range(MLA_HEADS):
        kn = _rms_rows(kt[h * MLA_NOPE:(h + 1) * MLA_NOPE], gkn_ref[...])
        k_ref[h] = jnp.concatenate([kn, kpe1, kpe2, kzero], axis=0).T.astype(BF)
        for b in range(tm // tk):
            vt_ref[h, b] = vt[h * MLA_V:(h + 1) * MLA_V, b * tk:(b + 1) * tk].astype(BF)


def _mla_up_call(cbt, layer, gq, gkv, wq, wk, wv, gqn, gqp, gkn, gkp, cos_t, sin_t):
    s = cbt.shape[1]
    tm = min(512, s)
    tk = min(ATT_TK, s)
    hw = MLA_NOPE + MLA_ROPE
    half = MLA_ROPE // 2

    def const(shape):
        return pl.BlockSpec(shape, lambda i: tuple(0 for _ in shape))

    def layer_w(shape):
        return pl.BlockSpec((None,) + shape, lambda i: (layer, 0, 0))

    return pl.pallas_call(
        functools.partial(_mla_up_kernel, tk=tk),
        grid=(s // tm,),
        in_specs=[pl.BlockSpec((cbt.shape[0], tm), lambda i: (0, i)),
                  const((MLA_Q_LORA, 1)), const((MLA_KV_LORA, 1)),
                  layer_w((MLA_HEADS * hw, MLA_Q_LORA)),
                  layer_w((MLA_HEADS * MLA_NOPE, MLA_KV_LORA)),
                  layer_w((MLA_HEADS * MLA_V, MLA_KV_LORA)),
                  const((MLA_NOPE, 1)), const((MLA_ROPE, 1)), const((MLA_NOPE, 1)),
                  const((MLA_ROPE, 1)),
                  pl.BlockSpec((half, tm), lambda i: (0, i)),
                  pl.BlockSpec((half, tm), lambda i: (0, i))],
        out_specs=[pl.BlockSpec((MLA_HEADS, MLA_QK_PAD, tm), lambda i: (0, 0, i)),
                   pl.BlockSpec((MLA_HEADS, tm, MLA_QK_PAD), lambda i: (0, i, 0)),
                   pl.BlockSpec((MLA_HEADS, tm // tk, MLA_V, tk), lambda i: (0, i, 0, 0))],
        out_shape=[jax.ShapeDtypeStruct((MLA_HEADS, MLA_QK_PAD, s), BF),
                   jax.ShapeDtypeStruct((MLA_HEADS, s, MLA_QK_PAD), BF),
                   jax.ShapeDtypeStruct((MLA_HEADS, s // tk, MLA_V, tk), BF)],
        compiler_params=_cparams("arbitrary"),
        name="mla_up",
    )(cbt, gq, gkv, wq, wk, wv, gqn, gqp, gkn, gkp, cos_t, sin_t)


def _logits_phase(kb, nblk, qts, k_rows, tk):
    rows = pl.ds(pl.multiple_of(kb * tk, tk), nblk * tk)
    return [jnp.dot(k_rows(g, rows), qts[g], preferred_element_type=F32) for g in range(len(qts))]


def _update_phase(kb, nblk, logits, colmax, vt_ref, m_ref, l_ref, acc_ref, tk):
    hg = len(logits)
    alphas, ps = [], []
    for g in range(hg):
        m_old = m_ref[g]
        m_new = jnp.maximum(m_old, colmax[g])
        alpha = jnp.exp2(m_old - m_new)
        p = jnp.exp2(logits[g] - m_new)
        l_ref[g] = alpha * l_ref[g] + jnp.sum(p, axis=0, keepdims=True)
        m_ref[g] = m_new
        alphas.append(alpha)
        ps.append(p.astype(BF))
    pvs = []
    for g in range(hg):
        pv = jnp.dot(vt_ref[g, kb], ps[g][0:tk], preferred_element_type=F32)
        for j in range(1, nblk):
            pv += jnp.dot(vt_ref[g, kb + j], ps[g][j * tk:(j + 1) * tk], preferred_element_type=F32)
        pvs.append(pv)
    for g in range(hg):
        acc_ref[g] = alphas[g] * acc_ref[g] + pvs[g]


def _softmax_block(kb, nblk, biases, qts, k_rows, vt_ref, m_ref, l_ref, acc_ref, tk):
    ss = _logits_phase(kb, nblk, qts, k_rows, tk)
    if biases is not None:
        ss = [s + b for s, b in zip(ss, biases)]
    colmax = [jnp.max(s, axis=0, keepdims=True) for s in ss]
    _update_phase(kb, nblk, ss, colmax, vt_ref, m_ref, l_ref, acc_ref, tk)


def _key_sweep(n_far, step, far_blocks, park, consume, near_logits, near_update):
    n_big = n_far // far_blocks
    done = n_big * far_blocks
    size = far_blocks // 2
    while size >= 1:
        take = ((n_far - done) // size) * size

        @pl.when(take > 0)
        def _(done=done, size=size):
            step(done, size)

        done = done + take
        size //= 2

    @pl.when(n_big > 0)
    def _():
        park(0, 0)
        n_pairs = (n_big - 1) // 2

        def pair(j, carry):
            kb = 2 * j * far_blocks
            park(kb + far_blocks, 1)
            consume(kb, 0)
            park(kb + 2 * far_blocks, 0)
            consume(kb + far_blocks, 1)
            return carry

        lax.fori_loop(0, n_pairs, pair, 0)
        kb = 2 * n_pairs * far_blocks
        left = n_big - 2 * n_pairs

        @pl.when(left == 2)
        def _():
            park(kb + far_blocks, 1)
            consume(kb, 0)
            near = near_logits()
            consume(kb + far_blocks, 1)
            near_update(near)

        @pl.when(left == 1)
        def _():
            near = near_logits()
            consume(kb, 0)
            near_update(near)

    @pl.when(n_big == 0)
    def _():
        near_update(near_logits())


def _pipeline_fns(far_blocks, qts, k_rows, vt_ref, m_ref, l_ref, acc_ref, s_refs, mx_refs, tk):
    def park(kb, slot):
        for g, s in enumerate(_logits_phase(kb, far_blocks, qts, k_rows, tk)):
            s_refs[slot][g] = s
            mx_refs[slot][g] = jnp.max(s, axis=0, keepdims=True)

    def consume(kb, slot):
        hg = len(qts)
        _update_phase(kb, far_blocks, [s_refs[slot][g] for g in range(hg)],
                      [mx_refs[slot][g] for g in range(hg)], vt_ref, m_ref, l_ref, acc_ref, tk)

    return park, consume


def _resident_spec(shape, index_map):
    return pl.BlockSpec(shape, index_map, pipeline_mode=pl.Buffered(1))


def _pipeline_scratch(hg, rows, n):
    return [pltpu.VMEM((hg, rows, n), F32), pltpu.VMEM((hg, rows, n), F32),
            pltpu.VMEM((hg, 1, n), F32), pltpu.VMEM((hg, 1, n), F32)]


def _init_softmax_state(m_ref, l_ref, acc_ref):
    m_ref[...] = jnp.full(m_ref.shape, NEG_INF, F32)
    l_ref[...] = jnp.zeros(l_ref.shape, F32)
    acc_ref[...] = jnp.zeros(acc_ref.shape, F32)


def _diff_attn_kernel(q_ref, k_ref, vt_ref, bn_ref, lam_ref, g_ref, o_ref,
                      m_ref, l_ref, acc_ref, sa_ref, sb_ref, mxa_ref, mxb_ref,
                      *, tq, tk, hg, lam_init):
    qb = pl.program_id(1)
    qts = [jnp.concatenate([q_ref[2 * g], q_ref[2 * g + 1]], axis=1) for g in range(hg)]

    def k_rows(g, rows):
        return k_ref[rows, g * HEAD_W:(g + 1) * HEAD_W]

    def step(kb, nblk, biases=None):
        _softmax_block(kb, nblk, biases, qts, k_rows, vt_ref, m_ref, l_ref, acc_ref, tk)

    _init_softmax_state(m_ref, l_ref, acc_ref)
    park, consume = _pipeline_fns(DIFF_FAR_BLOCKS, qts, k_rows, vt_ref, m_ref, l_ref, acc_ref,
                                  (sa_ref, sb_ref), (mxa_ref, mxb_ref), tk)

    first = (qb == 0).astype(jnp.int32)
    kb_near = qb - 1 + first

    def near_logits():
        ss = _logits_phase(kb_near, 2, qts, k_rows, tk)
        return [s + bn_ref[g, first] for g, s in enumerate(ss)]

    def near_update(ss):
        colmax = [jnp.max(s, axis=0, keepdims=True) for s in ss]
        _update_phase(kb_near, 2, ss, colmax, vt_ref, m_ref, l_ref, acc_ref, tk)

    _key_sweep(jnp.maximum(qb - 1, 0), step, DIFF_FAR_BLOCKS, park, consume,
               near_logits, near_update)

    lp = lam_ref[...]
    lam = (jnp.exp(jnp.sum(lp[0:1] * lp[1:2], axis=1, keepdims=True))
           - jnp.exp(jnp.sum(lp[2:3] * lp[3:4], axis=1, keepdims=True)) + lam_init)
    for g in range(hg):
        o = acc_ref[g] / l_ref[g]
        d = o[:, :tq] - lam * o[:, tq:]
        ms = jnp.mean(d * d, axis=0, keepdims=True)
        d = d * lax.rsqrt(ms + EPS) * g_ref[...] * (1.0 - lam_init)
        o_ref[:, g * HEAD_W:(g + 1) * HEAD_W] = d.T.astype(o_ref.dtype)


def _diff_attn_call(qpad, k, vt4, bias_near, lam_p, sub_g, *, lam_init):
    s = k.shape[0]
    tq = tk = min(ATT_TQ, s)
    nkb = s // tk
    n = 2 * tq
    hg = DIFF_HEAD_GROUP
    return pl.pallas_call(
        functools.partial(_diff_attn_kernel, tq=tq, tk=tk, hg=hg, lam_init=lam_init),
        grid=(DA_HEADS // hg, s // tq),
        in_specs=[pl.BlockSpec((2 * hg, HEAD_W, tq), lambda h, i: (h, 0, i)),
                  _resident_spec((s, hg * HEAD_W), lambda h, i: (0, h)),
                  _resident_spec((hg, nkb, HEAD_W, tk), lambda h, i: (h, 0, 0, 0)),
                  _resident_spec((hg, 2, 2 * tk, n), lambda h, i: (h, 0, 0, 0)),
                  pl.BlockSpec((4, DA_QK), lambda h, i: (0, 0)),
                  pl.BlockSpec((HEAD_W, 1), lambda h, i: (0, 0))],
        out_specs=pl.BlockSpec((tq, hg * HEAD_W), lambda h, i: (i, h)),
        out_shape=jax.ShapeDtypeStruct((s, DA_HEADS * HEAD_W), BF),
        scratch_shapes=[pltpu.VMEM((hg, 1, n), F32), pltpu.VMEM((hg, 1, n), F32),
                        pltpu.VMEM((hg, HEAD_W, n), F32)] + _pipeline_scratch(hg, DIFF_FAR_BLOCKS * tk, n),
        compiler_params=_cparams("arbitrary", "arbitrary"),
        name="diff_attention",
    )(qpad, k, vt4, bias_near, lam_p, sub_g)


def _mla_attn_kernel(q_ref, k_ref, vt_ref, mask_ref, o_ref, m_ref, l_ref, acc_ref,
                     sa_ref, sb_ref, mxa_ref, mxb_ref, *, tk, hg):
    qb = pl.program_id(1)
    qts = [q_ref[g] for g in range(hg)]

    def k_rows(g, rows):
        return k_ref[g, rows, :]

    def step(kb, nblk, biases=None):
        _softmax_block(kb, nblk, biases, qts, k_rows, vt_ref, m_ref, l_ref, acc_ref, tk)

    _init_softmax_state(m_ref, l_ref, acc_ref)
    park, consume = _pipeline_fns(MLA_FAR_BLOCKS, qts, k_rows, vt_ref, m_ref, l_ref, acc_ref,
                                  (sa_ref, sb_ref), (mxa_ref, mxb_ref), tk)

    def near_logits():
        return [s + mask_ref[...] for s in _logits_phase(qb, 1, qts, k_rows, tk)]

    def near_update(ss):
        colmax = [jnp.max(s, axis=0, keepdims=True) for s in ss]
        _update_phase(qb, 1, ss, colmax, vt_ref, m_ref, l_ref, acc_ref, tk)

    _key_sweep(qb, step, MLA_FAR_BLOCKS, park, consume, near_logits, near_update)
    for g in range(hg):
        o = acc_ref[g] / l_ref[g]
        o_ref[:, g * HEAD_W:(g + 1) * HEAD_W] = o.T.astype(o_ref.dtype)


def _mla_attn_call(qt, k, vt4, mask_diag):
    s = k.shape[1]
    tq = tk = min(ATT_TQ, s)
    nkb = s // tk
    hg = MLA_HEAD_GROUP
    return pl.pallas_call(
        functools.partial(_mla_attn_kernel, tk=tk, hg=hg),
        grid=(MLA_HEADS // hg, s // tq),
        in_specs=[pl.BlockSpec((hg, MLA_QK_PAD, tq), lambda h, i: (h, 0, i)),
                  _resident_spec((hg, s, MLA_QK_PAD), lambda h, i: (h, 0, 0)),
                  _resident_spec((hg, nkb, HEAD_W, tk), lambda h, i: (h, 0, 0, 0)),
                  pl.BlockSpec((tk, tq), lambda h, i: (0, 0))],
        out_specs=pl.BlockSpec((tq, hg * HEAD_W), lambda h, i: (i, h)),
        out_shape=jax.ShapeDtypeStruct((s, MLA_HEADS * HEAD_W), BF),
        scratch_shapes=[pltpu.VMEM((hg, 1, tq), F32), pltpu.VMEM((hg, 1, tq), F32),
                        pltpu.VMEM((hg, HEAD_W, tq), F32)] + _pipeline_scratch(hg, MLA_FAR_BLOCKS * tk, tq),
        compiler_params=_cparams("arbitrary", "arbitrary"),
        name="mla_attention",
    )(qt, k, vt4, mask_diag)


def _stick_block(kb, mask, qts, k_ref, vt_ref, tri_neg, carry_ref, acc_ref, tk):
    hg = len(qts)
    rows = pl.ds(pl.multiple_of(kb * tk, tk), tk)
    zs = [jnp.dot(k_ref[rows, g * HEAD_W:(g + 1) * HEAD_W], qts[g], preferred_element_type=F32)
          for g in range(hg)]
    logsigs, his, los = [], [], []
    for g in range(hg):
        z = zs[g]
        sp = jnp.maximum(z, 0.0) + jnp.log2(1.0 + jnp.exp2(-jnp.abs(z)))
        logsigs.append(z - sp + carry_ref[g])
        if mask is not None:
            sp = jnp.where(mask, sp, 0.0)
        hi = sp.astype(BF)
        his.append(hi)
        los.append((sp - hi.astype(F32)).astype(BF))
        carry_ref[g] -= jnp.sum(sp, axis=0, keepdims=True)
    betweens = [jnp.dot(tri_neg, his[g], preferred_element_type=F32)
                + jnp.dot(tri_neg, los[g], preferred_element_type=F32) for g in range(hg)]
    ws = []
    for g in range(hg):
        w = jnp.exp2(logsigs[g] + betweens[g])
        if mask is not None:
            w = jnp.where(mask, w, 0.0)
        ws.append(w.astype(BF))
    pvs = [jnp.dot(vt_ref[g, kb], ws[g], preferred_element_type=F32) for g in range(hg)]
    for g in range(hg):
        acc_ref[g] += pvs[g]


def _stick_kernel(q_ref, k_ref, vt_ref, tri_ref, o_ref, carry_ref, acc_ref, *, tq, tk, hg):
    qb = pl.program_id(1)
    qts = [q_ref[g * HEAD_W:(g + 1) * HEAD_W, :] for g in range(hg)]
    tri = tri_ref[...]
    carry_ref[...] = jnp.zeros(carry_ref.shape, F32)
    acc_ref[...] = jnp.zeros(acc_ref.shape, F32)
    kpos = lax.broadcasted_iota(jnp.int32, (tk, tq), 0)
    qpos = lax.broadcasted_iota(jnp.int32, (tk, tq), 1)
    _stick_block(qb, kpos < qpos, qts, k_ref, vt_ref, tri, carry_ref, acc_ref, tk)

    def more(state):
        i, cmax = state
        return jnp.logical_and(i < qb, cmax > SB_LOG2W_UNDERFLOW)

    def earlier(state):
        i, _ = state
        _stick_block(qb - 1 - i, None, qts, k_ref, vt_ref, tri, carry_ref, acc_ref, tk)
        return i + 1, jnp.max(carry_ref[...])

    lax.while_loop(more, earlier, (jnp.int32(0), jnp.max(carry_ref[...])))
    for g in range(hg):
        o_ref[:, g * HEAD_W:(g + 1) * HEAD_W] = acc_ref[g].T.astype(o_ref.dtype)


def _stick_call(qt, k, vt4, tri):
    s = k.shape[0]
    tq = tk = min(ATT_TQ, s)
    nkb = s // tk
    hg = SB_HEAD_GROUP
    return pl.pallas_call(
        functools.partial(_stick_kernel, tq=tq, tk=tk, hg=hg),
        grid=(SB_HEADS // hg, s // tq),
        in_specs=[pl.BlockSpec((hg * HEAD_W, tq), lambda h, i: (h, i)),
                  _resident_spec((s, hg * HEAD_W), lambda h, i: (0, h)),
                  _resident_spec((hg, nkb, HEAD_W, tk), lambda h, i: (h, 0, 0, 0)),
                  pl.BlockSpec((tk, tk), lambda h, i: (0, 0))],
        out_specs=pl.BlockSpec((tq, hg * HEAD_W), lambda h, i: (i, h)),
        out_shape=jax.ShapeDtypeStruct((s, SB_HEADS * HEAD_W), BF),
        scratch_shapes=[pltpu.VMEM((hg, 1, tq), F32), pltpu.VMEM((hg, HEAD_W, tq), F32)],
        compiler_params=_cparams("arbitrary", "arbitrary"),
        name="stick_breaking",
    )(qt, k, vt4, tri)


def _merge_kernel(ht_ref, ya_ref, yb_ref, yc_ref, wga_ref, wgb_ref, wgc_ref, wb_ref, o_ref):
    ht = ht_ref[...]
    acc = None
    for n, (y_ref, wg_ref) in enumerate(((ya_ref, wga_ref), (yb_ref, wgb_ref), (yc_ref, wgc_ref))):
        gate = jax.nn.sigmoid(jnp.dot(wg_ref[...], ht, preferred_element_type=F32)).T
        up = jnp.dot(y_ref[...], wb_ref[n], preferred_element_type=F32)
        acc = gate * up if acc is None else acc + gate * up
    o_ref[...] = acc.astype(o_ref.dtype)


def _merge_call(ht, ya, yb, yc, wt_tail, wb, layer):
    d, s = ht.shape
    tm = min(512, s)
    tn = 512
    ysp = pl.BlockSpec((tm, BRANCH_WIDTH), lambda i, j: (i, 0))

    def gate_spec(n):
        blk0 = (SEG_GATE * SEG + n * d) // tn
        return pl.BlockSpec((None, tn, d), lambda i, j: (layer, blk0 + j, 0))

    return pl.pallas_call(
        _merge_kernel,
        grid=(s // tm, d // tn),
        in_specs=[pl.BlockSpec((d, tm), lambda i, j: (0, i)), ysp, ysp, ysp,
                  gate_spec(0), gate_spec(1), gate_spec(2),
                  pl.BlockSpec((None, N_BRANCHES, BRANCH_WIDTH, tn), lambda i, j: (layer, 0, 0, j))],
        out_specs=pl.BlockSpec((tm, tn), lambda i, j: (i, j)),
        out_shape=jax.ShapeDtypeStruct((s, d), BF),
        compiler_params=_cparams("arbitrary", "arbitrary"),
        name="gated_merge",
    )(ht, ya, yb, yc, wt_tail, wt_tail, wt_tail, wb)


def _mlp_kernel(h_ref, w1_ref, w2_ref, x_ref, g_ref, o_ref, acc_ref, ua_ref, ub_ref):
    f = pl.program_id(1)
    last = pl.num_programs(1) - 1

    def up(u_ref):
        u = jnp.dot(h_ref[...], w1_ref[...], preferred_element_type=F32)
        u_ref[...] = jnp.square(jnp.maximum(u, 0.0)).astype(BF)

    def down(u_ref):
        acc_ref[...] += jnp.dot(u_ref[...], w2_ref[...], preferred_element_type=F32)

    @pl.when(f == 0)
    def _():
        acc_ref[...] = jnp.zeros(acc_ref.shape, F32)
        up(ua_ref)

    middle = jnp.logical_and(f > 0, f < last)

    @pl.when(jnp.logical_and(middle, f % 2 == 1))
    def _():
        up(ub_ref)
        down(ua_ref)

    @pl.when(jnp.logical_and(middle, f % 2 == 0))
    def _():
        up(ua_ref)
        down(ub_ref)

    @pl.when(f == last)
    def _():
        down(ub_ref)
        o_ref[...] = x_ref[...] + g_ref[...] * acc_ref[...]


def _mlp_call(h, w1, w2, layer, x, g):
    s, d = h.shape
    ff = w1.shape[2]
    tm = min(512, s)
    tf = 1024
    nf = ff // tf
    assert nf % 2 == 0
    return pl.pallas_call(
        _mlp_kernel,
        grid=(s // tm, nf + 1),
        in_specs=[pl.BlockSpec((tm, d), lambda i, f: (i, 0)),
                  pl.BlockSpec((None, d, tf), lambda i, f: (layer, 0, jnp.minimum(f, nf - 1))),
                  pl.BlockSpec((None, tf, d), lambda i, f: (layer, jnp.maximum(f - 1, 0), 0)),
                  pl.BlockSpec((tm, d), lambda i, f: (i, 0)),
                  pl.BlockSpec((1, d), lambda i, f: (0, 0))],
        out_specs=pl.BlockSpec((tm, d), lambda i, f: (i, 0)),
        out_shape=jax.ShapeDtypeStruct((s, d), F32),
        scratch_shapes=[pltpu.VMEM((tm, d), F32), pltpu.VMEM((tm, tf), BF),
                        pltpu.VMEM((tm, tf), BF)],
        compiler_params=_cparams("arbitrary", "arbitrary"),
        name="sqrelu_mlp",
    )(h, w1, w2, x, g)


def _t5_bucket(rel):
    nb = T5_BUCKETS // 2
    max_exact = nb // 2
    n = jnp.abs(rel)
    large = max_exact + (jnp.log(jnp.maximum(n, 1).astype(F32) / max_exact)
                         / math.log(T5_MAX_DIST / max_exact) * (nb - max_exact)).astype(jnp.int32)
    large = jnp.minimum(large, nb - 1)
    return jnp.where(rel > 0, nb, 0) + jnp.where(n < max_exact, n, large)


def _bias_tiles(t5_bias, tq, tk):
    table = t5_bias.astype(F32)

    def lookup(bucket):
        hit = bucket[..., None, None] == jnp.arange(T5_BUCKETS)[:, None]
        return jnp.sum(jnp.where(hit, table, 0.0), axis=-2)

    kl = jnp.arange(tk)[:, None]
    ql = jnp.arange(tq)[None, :]
    far = lookup(_t5_bucket(jnp.array(-(tk + tq), jnp.int32)))
    diag = jnp.transpose(lookup(_t5_bucket(kl - ql)) - far, (2, 0, 1))
    prev = jnp.transpose(lookup(_t5_bucket(kl - tk - ql)) - far, (2, 0, 1))
    allowed = (kl // CHUNK) <= (ql // CHUNK)
    diag = jnp.where(allowed[None], diag * LOG2E, NEG_INF)
    near = jnp.stack([jnp.concatenate([prev * LOG2E, diag], axis=1),
                      jnp.concatenate([diag, jnp.full_like(diag, NEG_INF)], axis=1)], axis=1)
    return jnp.concatenate([near, near], axis=3)


def _rope_tables(s):
    half = MLA_ROPE // 2
    inv = ROPE_BASE ** (-jnp.arange(half, dtype=F32) / half)
    ang = jnp.arange(s).astype(F32)[:, None] * inv[None, :]
    return jnp.cos(ang).T, jnp.sin(ang).T


def kernel(x, c, w_ada, b_ada, norm_mix_g, norm_mlp_g, w_in, diff_qk_g, diff_lambda, diff_subln_g,
           t5_bias, mla_q_norm_g, mla_kv_norm_g, w_q_up, w_kv_up, mla_qk_g, w_branch, w_out,
           w_mlp_in, w_mlp_out):
    b, s, d = x.shape
    assert b == 1 and d == D_MODEL
    depth = w_ada.shape[0]
    tq = tk = min(ATT_TQ, s)
    hw = MLA_NOPE + MLA_ROPE

    mod = _mod_call(jnp.broadcast_to(c, (8, d)).astype(BF), w_ada, b_ada[:, None, :])[:, 0, :]
    mod = mod.reshape(depth, N_MOD, 1, d)

    wt_head = jnp.swapaxes(w_in[:, :, :HEAD_ROWS], 1, 2).astype(BF)
    wt_tail = jnp.swapaxes(w_in[:, :, TAIL_ROW0:], 1, 2).astype(BF)
    w_br = w_branch.astype(BF)
    w_o = w_out.astype(BF)
    w_1 = w_mlp_in.astype(BF)
    w_2 = w_mlp_out.astype(BF)
    wt_qu = jnp.swapaxes(w_q_up, 1, 2).astype(BF)
    wkv = w_kv_up.reshape(depth, MLA_KV_LORA, MLA_HEADS, MLA_NOPE + MLA_V)
    wt_kn = jnp.swapaxes(wkv[..., :MLA_NOPE].reshape(depth, MLA_KV_LORA, -1), 1, 2).astype(BF)
    wt_v = jnp.swapaxes(wkv[..., MLA_NOPE:].reshape(depth, MLA_KV_LORA, -1), 1, 2).astype(BF)

    bias_near = _bias_tiles(t5_bias, tq, tk)
    kl = jnp.arange(tk)[:, None]
    ql = jnp.arange(tq)[None, :]
    mask_diag = jnp.where((kl // CHUNK) <= (ql // CHUNK), 0.0, NEG_INF).astype(F32)
    tri = -(jnp.arange(tk)[None, :] > jnp.arange(tk)[:, None]).astype(BF)
    cos_t, sin_t = _rope_tables(s)

    ones_col = jnp.ones((SEG, 1), F32)
    sb_scale = jnp.full((SB_HEADS * SB_DIM, 1), SB_DIM ** -0.5 * LOG2E, F32)
    da_scale = DA_QK ** -0.5 * LOG2E
    mla_scale = hw ** -0.5 * LOG2E

    x2 = x[0]
    for l in range(depth):
        sh1, sc1, g1, sh2, sc2, g2 = (mod[l, i] for i in range(N_MOD))
        lam_init = 0.8 - 0.6 * math.exp(-0.3 * l)

        ht = _norm_call(x2, norm_mix_g[l][None], sc1, sh1, transpose=True)

        gq_rows = jnp.tile(diff_qk_g[l, 0], 2 * DA_HEADS)[:, None] * da_scale
        gk_rows = jnp.tile(diff_qk_g[l, 1], 2 * DA_HEADS)[:, None]
        qa = _proj_call(wt_head, l, SEG_DQ, 1, ht, gq_rows, group=DA_QK, layout="qpad",
                        name="proj_diff_q")
        ka = _proj_call(wt_head, l, SEG_DK, 1, ht, gk_rows, group=DA_QK, layout="n",
                        name="proj_diff_k")
        va = _proj_call(wt_head, l, SEG_DV, 1, ht, ones_col, layout="t4", name="proj_diff_v")
        qc = _proj_call(wt_tail, l, SEG_SQ, 1, ht, sb_scale, layout="t", name="proj_sb_q")
        kc = _proj_call(wt_tail, l, SEG_SK, 1, ht, ones_col, layout="n", name="proj_sb_k")
        vc = _proj_call(wt_tail, l, SEG_SV, 1, ht, ones_col, layout="t4", name="proj_sb_v")
        cbt = _proj_call(wt_head, l, SEG_LAT, 1, ht, ones_col, layout="t", out_dtype=F32,
                         name="proj_mla_latent")
        gqk = mla_qk_g[l]
        qb_t, kb, vb_t = _mla_up_call(
            cbt, l, mla_q_norm_g[l][:, None], mla_kv_norm_g[l][:, None], wt_qu, wt_kn, wt_v,
            gqk[0, :MLA_NOPE, None] * mla_scale, gqk[0, MLA_NOPE:, None] * mla_scale,
            gqk[1, :MLA_NOPE, None], gqk[1, MLA_NOPE:, None], cos_t, sin_t)

        ya = _diff_attn_call(qa, ka, va, bias_near, diff_lambda[l], diff_subln_g[l][:, None],
                             lam_init=lam_init)
        yb = _mla_attn_call(qb_t, kb, vb_t, mask_diag)
        yc = _stick_call(qc, kc, vc, tri)

        merged = _merge_call(ht, ya, yb, yc, wt_tail, w_br, l)
        x2 = _resid_proj_call(merged, w_o, l, x2, g1)

        h2 = _norm_call(x2, norm_mlp_g[l][None], sc2, sh2, transpose=False)
        x2 = _mlp_call(h2, w_1, w_2, l, x2, g2)
    return x2[None]
```

```python
import functools
import math

import jax
import jax.numpy as jnp
from jax import lax
from jax.experimental import pallas as pl
from jax.experimental.pallas import tpu as pltpu

BF = jnp.bfloat16
F32 = jnp.float32

D_MODEL = 2048
CHUNK = 64
DA_HEADS = 8
DA_QK = 64
DA_V = 128
MLA_HEADS = 8
MLA_Q_LORA = 512
MLA_KV_LORA = 256
MLA_NOPE = 128
MLA_ROPE = 64
MLA_V = 128
MLA_QK_PAD = 256
ROPE_BASE = 10000.0
SB_HEADS = 8
SB_DIM = 128
BRANCH_WIDTH = 1024
N_BRANCHES = 3
T5_BUCKETS = 32
T5_MAX_DIST = 128
D_FF = 4 * D_MODEL
N_MOD = 6
EPS = 1e-6
NEG_INF = -1e30
LOG2E = 1.4426950408889634

HEAD_W = 128
ATT_TQ = 256
ATT_TK = 256
DIFF_FAR_BLOCKS = 2
MLA_FAR_BLOCKS = 4
DIFF_HEAD_GROUP = 4
MLA_HEAD_GROUP = 4
SB_HEAD_GROUP = 4
SB_LOG2W_UNDERFLOW = -152.0
MERGE_TN = 512
MLP_TF = 1024
V7X_VMEM_LIMIT = 56 * 1024 * 1024

SEG = 1024
HEAD_ROWS = 4 * SEG
TAIL_ROW0 = 3 * SEG + MLA_Q_LORA + MLA_KV_LORA + MLA_ROPE
SEG_DQ, SEG_DK, SEG_DV, SEG_LAT = 0, 1, 2, 3
SEG_SQ, SEG_SK, SEG_SV, SEG_GATE = 0, 1, 2, 3
LAT_KPE = MLA_Q_LORA + MLA_KV_LORA


def _cparams(*sem):
    return pltpu.CompilerParams(dimension_semantics=sem, vmem_limit_bytes=V7X_VMEM_LIMIT)


def _mod_kernel(c_ref, w_ref, b_ref, o_ref):
    w = w_ref[0].astype(BF)
    o_ref[0] = jnp.dot(c_ref[...], w, preferred_element_type=F32) + b_ref[0]


def _mod_call(c8, w_ada, b_ada3):
    depth, d, n = w_ada.shape
    tn = 1024
    return pl.pallas_call(
        _mod_kernel,
        grid=(depth, n // tn),
        in_specs=[pl.BlockSpec((8, d), lambda l, j: (0, 0)),
                  pl.BlockSpec((1, d, tn), lambda l, j: (l, 0, j)),
                  pl.BlockSpec((1, 1, tn), lambda l, j: (l, 0, j))],
        out_specs=pl.BlockSpec((1, 8, tn), lambda l, j: (l, 0, j)),
        out_shape=jax.ShapeDtypeStruct((depth, 8, n), F32),
        compiler_params=_cparams("arbitrary", "arbitrary"),
        name="adaln_mod",
    )(c8, w_ada, b_ada3)


def _norm_kernel(x_ref, g_ref, sc_ref, sh_ref, h_ref, *, transpose):
    x = x_ref[...]
    ms = jnp.mean(x * x, axis=-1, keepdims=True)
    y = x * lax.rsqrt(ms + EPS) * g_ref[...]
    h = y * (1.0 + sc_ref[...]) + sh_ref[...]
    h_ref[...] = (h.T if transpose else h).astype(BF)


def _norm_call(x, g, sc, sh, *, transpose):
    s, d = x.shape
    tm = min(512, s)
    row = pl.BlockSpec((1, d), lambda i: (0, 0))
    if transpose:
        out_shape = jax.ShapeDtypeStruct((d, s), BF)
        out_spec = pl.BlockSpec((d, tm), lambda i: (0, i))
    else:
        out_shape = jax.ShapeDtypeStruct((s, d), BF)
        out_spec = pl.BlockSpec((tm, d), lambda i: (i, 0))
    return pl.pallas_call(
        functools.partial(_norm_kernel, transpose=transpose),
        grid=(s // tm,),
        in_specs=[pl.BlockSpec((tm, d), lambda i: (i, 0)), row, row, row],
        out_specs=out_spec,
        out_shape=out_shape,
        compiler_params=_cparams("arbitrary"),
        name="adaln_rmsnorm",
    )(x, g, sc, sh)


def _proj_kernel(w_ref, xt_ref, rs_ref, o_ref, *, group, layout, tk):
    y = jnp.dot(w_ref[...], xt_ref[...], preferred_element_type=F32)
    tn, tm = y.shape
    if group:
        y3 = y.reshape(tn // group, group, tm)
        ms = jnp.mean(y3 * y3, axis=1, keepdims=True)
        y = (y3 * lax.rsqrt(ms + EPS)).reshape(tn, tm)
    y = y * rs_ref[...]
    if layout == "t":
        o_ref[...] = y.astype(o_ref.dtype)
    elif layout == "n":
        o_ref[...] = y.T.astype(o_ref.dtype)
    elif layout == "t4":
        for a in range(tn // HEAD_W):
            for b in range(tm // tk):
                o_ref[a, b] = y[a * HEAD_W:(a + 1) * HEAD_W, b * tk:(b + 1) * tk].astype(o_ref.dtype)
    elif layout == "qpad":
        first = lax.broadcasted_iota(jnp.int32, (HEAD_W, tm), 0) < DA_QK
        for a in range(tn // HEAD_W):
            ya = y[a * HEAD_W:(a + 1) * HEAD_W]
            o_ref[2 * a] = jnp.where(first, ya, 0.0).astype(o_ref.dtype)
            o_ref[2 * a + 1] = jnp.where(first, 0.0, ya).astype(o_ref.dtype)


def _proj_call(wt, layer, seg0, nseg, ht, rowscale, *, group=0, layout="t", out_dtype=BF, name):
    k, s = ht.shape
    n = nseg * SEG
    tn = SEG
    tm = min(1024, s)
    tk = min(ATT_TK, s)
    if layout == "t":
        out_shape = jax.ShapeDtypeStruct((n, s), out_dtype)
        out_spec = pl.BlockSpec((tn, tm), lambda i, j: (i, j))
    elif layout == "n":
        out_shape = jax.ShapeDtypeStruct((s, n), out_dtype)
        out_spec = pl.BlockSpec((tm, tn), lambda i, j: (j, i))
    elif layout == "t4":
        out_shape = jax.ShapeDtypeStruct((n // HEAD_W, s // tk, HEAD_W, tk), out_dtype)
        out_spec = pl.BlockSpec((tn // HEAD_W, tm // tk, HEAD_W, tk), lambda i, j: (i, j, 0, 0))
    else:
        out_shape = jax.ShapeDtypeStruct((2 * n // HEAD_W, HEAD_W, s), out_dtype)
        out_spec = pl.BlockSpec((2 * tn // HEAD_W, HEAD_W, tm), lambda i, j: (i, 0, j))
    return pl.pallas_call(
        functools.partial(_proj_kernel, group=group, layout=layout, tk=tk),
        grid=(nseg, s // tm),
        in_specs=[pl.BlockSpec((None, tn, k), lambda i, j: (layer, seg0 + i, 0)),
                  pl.BlockSpec((k, tm), lambda i, j: (0, j)),
                  pl.BlockSpec((tn, 1), lambda i, j: (i, 0))],
        out_specs=out_spec,
        out_shape=out_shape,
        compiler_params=_cparams("arbitrary", "arbitrary"),
        name=name,
    )(wt, ht, rowscale)


def _resid_proj_kernel(a_ref, w_ref, x_ref, g_ref, o_ref):
    y = jnp.dot(a_ref[...], w_ref[...], preferred_element_type=F32)
    o_ref[...] = x_ref[...] + g_ref[...] * y


def _resid_proj_call(a, w, layer, x, g):
    s, k = a.shape
    n = w.shape[2]
    tm = min(1024, s)
    tn = min(1024, n)
    return pl.pallas_call(
        _resid_proj_kernel,
        grid=(s // tm, n // tn),
        in_specs=[pl.BlockSpec((tm, k), lambda i, j: (i, 0)),
                  pl.BlockSpec((None, k, tn), lambda i, j: (layer, 0, j)),
                  pl.BlockSpec((tm, tn), lambda i, j: (i, j)),
                  pl.BlockSpec((1, tn), lambda i, j: (0, j))],
        out_specs=pl.BlockSpec((tm, tn), lambda i, j: (i, j)),
        out_shape=jax.ShapeDtypeStruct((s, n), F32),
        compiler_params=_cparams("arbitrary", "arbitrary"),
        name="out_proj_residual",
    )(a, w, x, g)


def _rms_rows(x, g):
    return x * lax.rsqrt(jnp.mean(x * x, axis=0, keepdims=True) + EPS) * g


def _rope_rows(x, cos_t, sin_t):
    half = MLA_ROPE // 2
    x1, x2 = x[:half], x[half:]
    return x1 * cos_t - x2 * sin_t, x1 * sin_t + x2 * cos_t


def _mla_up_kernel(cb_ref, gq_ref, gkv_ref, wq_ref, wk_ref, wv_ref, gqn_ref, gqp_ref, gkn_ref,
                   gkp_ref, cost_ref, sint_ref, qt_ref, k_ref, vt_ref, *, tk):
    tm = cb_ref.shape[1]
    hw = MLA_NOPE + MLA_ROPE
    cos_t = cost_ref[...]
    sin_t = sint_ref[...]
    cqn = _rms_rows(cb_ref[0:MLA_Q_LORA], gq_ref[...]).astype(BF)
    qt = jnp.dot(wq_ref[...], cqn, preferred_element_type=F32)
    qzero = jnp.zeros((MLA_QK_PAD - hw, tm), BF)
    for h in range(MLA_HEADS):
        nope = _rms_rows(qt[h * hw:h * hw + MLA_NOPE], gqn_ref[...])
        pe1, pe2 = _rope_rows(_rms_rows(qt[h * hw + MLA_NOPE:(h + 1) * hw], gqp_ref[...]), cos_t, sin_t)
        qt_ref[h, 0:MLA_NOPE] = nope.astype(BF)
        qt_ref[h, MLA_NOPE:MLA_NOPE + MLA_ROPE // 2] = pe1.astype(BF)
        qt_ref[h, MLA_NOPE + MLA_ROPE // 2:hw] = pe2.astype(BF)
        qt_ref[h, hw:MLA_QK_PAD] = qzero
    ckvn = _rms_rows(cb_ref[MLA_Q_LORA:LAT_KPE], gkv_ref[...]).astype(BF)
    kt = jnp.dot(wk_ref[...], ckvn, preferred_element_type=F32)
    vt = jnp.dot(wv_ref[...], ckvn, preferred_element_type=F32)
    kpe1, kpe2 = _rope_rows(_rms_rows(cb_ref[LAT_KPE:LAT_KPE + MLA_ROPE], gkp_ref[...]), cos_t, sin_t)
    kzero = jnp.zeros((MLA_QK_PAD - hw, tm), F32)
    for h in range(MLA_HEADS):
        kn = _rms_rows(kt[h * MLA_NOPE:(h + 1) * MLA_NOPE], gkn_ref[...])
        k_ref[h] = jnp.concatenate([kn, kpe1, kpe2, kzero], axis=0).T.astype(BF)
        for b in range(tm // tk):
            vt_ref[h, b] = vt[h * MLA_V:(h + 1) * MLA_V, b * tk:(b + 1) * tk].astype(BF)


def _mla_up_call(cbt, layer, gq, gkv, wq, wk, wv, gqn, gqp, gkn, gkp, cos_t, sin_t):
    s = cbt.shape[1]
    tm = min(512, s)
    tk = min(ATT_TK, s)
    hw = MLA_NOPE + MLA_ROPE
    half = MLA_ROPE // 2

    def const(shape):
        return pl.BlockSpec(shape, lambda i: tuple(0 for _ in shape))

    def layer_w(shape):
        return pl.BlockSpec((None,) + shape, lambda i: (layer, 0, 0))

    return pl.pallas_call(
        functools.partial(_mla_up_kernel, tk=tk),
        grid=(s // tm,),
        in_specs=[pl.BlockSpec((cbt.shape[0], tm), lambda i: (0, i)),
                  const((MLA_Q_LORA, 1)), const((MLA_KV_LORA, 1)),
                  layer_w((MLA_HEADS * hw, MLA_Q_LORA)),
                  layer_w((MLA_HEADS * MLA_NOPE, MLA_KV_LORA)),
                  layer_w((MLA_HEADS * MLA_V, MLA_KV_LORA)),
                  const((MLA_NOPE, 1)), const((MLA_ROPE, 1)), const((MLA_NOPE, 1)),
                  const((MLA_ROPE, 1)),
                  pl.BlockSpec((half, tm), lambda i: (0, i)),
                  pl.BlockSpec((half, tm), lambda i: (0, i))],
        out_specs=[pl.BlockSpec((MLA_HEADS, MLA_QK_PAD, tm), lambda i: (0, 0, i)),
                   pl.BlockSpec((MLA_HEADS, tm, MLA_QK_PAD), lambda i: (0, i, 0)),
                   pl.BlockSpec((MLA_HEADS, tm // tk, MLA_V, tk), lambda i: (0, i, 0, 0))],
        out_shape=[jax.ShapeDtypeStruct((MLA_HEADS, MLA_QK_PAD, s), BF),
                   jax.ShapeDtypeStruct((MLA_HEADS, s, MLA_QK_PAD), BF),
                   jax.ShapeDtypeStruct((MLA_HEADS, s // tk, MLA_V, tk), BF)],
        compiler_params=_cparams("arbitrary"),
        name="mla_up",
    )(cbt, gq, gkv, wq, wk, wv, gqn, gqp, gkn, gkp, cos_t, sin_t)


def _logits_phase(kb, nblk, qts, k_rows, tk):
    rows = pl.ds(pl.multiple_of(kb * tk, tk), nblk * tk)
    return [jnp.dot(k_rows(g, rows), qts[g], preferred_element_type=F32) for g in range(len(qts))]


def _update_phase(kb, nblk, logits, colmax, vt_ref, m_ref, l_ref, acc_ref, tk):
    hg = len(logits)
    alphas, ps = [], []
    for g in range(hg):
        m_old = m_ref[g]
        m_new = jnp.maximum(m_old, colmax[g])
        alpha = jnp.exp2(m_old - m_new)
        p = jnp.exp2(logits[g] - m_new)
        l_ref[g] = alpha * l_ref[g] + jnp.sum(p, axis=0, keepdims=True)
        m_ref[g] = m_new
        alphas.append(alpha)
        ps.append(p.astype(BF))
    pvs = []
    for g in range(hg):
        pv = jnp.dot(vt_ref[g, kb], ps[g][0:tk], preferred_element_type=F32)
        for j in range(1, nblk):
            pv += jnp.dot(vt_ref[g, kb + j], ps[g][j * tk:(j + 1) * tk], preferred_element_type=F32)
        pvs.append(pv)
    for g in range(hg):
        acc_ref[g] = alphas[g] * acc_ref[g] + pvs[g]


def _softmax_block(kb, nblk, biases, qts, k_rows, vt_ref, m_ref, l_ref, acc_ref, tk):
    ss = _logits_phase(kb, nblk, qts, k_rows, tk)
    if biases is not None:
        ss = [s + b for s, b in zip(ss, biases)]
    colmax = [jnp.max(s, axis=0, keepdims=True) for s in ss]
    _update_phase(kb, nblk, ss, colmax, vt_ref, m_ref, l_ref, acc_ref, tk)


def _key_sweep(n_far, step, far_blocks, park, consume, near_logits, near_update):
    n_big = n_far // far_blocks
    done = n_big * far_blocks
    size = far_blocks // 2
    while size >= 1:
        take = ((n_far - done) // size) * size

        @pl.when(take > 0)
        def _(done=done, size=size):
            step(done, size)

        done = done + take
        size //= 2

    @pl.when(n_big > 0)
    def _():
        park(0, 0)
        n_pairs = (n_big - 1) // 2

        def pair(j, carry):
            kb = 2 * j * far_blocks
            park(kb + far_blocks, 1)
            consume(kb, 0)
            park(kb + 2 * far_blocks, 0)
            consume(kb + far_blocks, 1)
            return carry

        lax.fori_loop(0, n_pairs, pair, 0)
        kb = 2 * n_pairs * far_blocks
        left = n_big - 2 * n_pairs

        @pl.when(left == 2)
        def _():
            park(kb + far_blocks, 1)
            consume(kb, 0)
            near = near_logits()
            consume(kb + far_blocks, 1)
            near_update(near)

        @pl.when(left == 1)
        def _():
            near = near_logits()
            consume(kb, 0)
            near_update(near)

    @pl.when(n_big == 0)
    def _():
        near_update(near_logits())


def _pipeline_fns(far_blocks, qts, k_rows, vt_ref, m_ref, l_ref, acc_ref, s_refs, mx_refs, tk):
    def park(kb, slot):
        for g, s in enumerate(_logits_phase(kb, far_blocks, qts, k_rows, tk)):
            s_refs[slot][g] = s
            mx_refs[slot][g] = jnp.max(s, axis=0, keepdims=True)

    def consume(kb, slot):
        hg = len(qts)
        _update_phase(kb, far_blocks, [s_refs[slot][g] for g in range(hg)],
                      [mx_refs[slot][g] for g in range(hg)], vt_ref, m_ref, l_ref, acc_ref, tk)

    return park, consume


def _resident_spec(shape, index_map):
    return pl.BlockSpec(shape, index_map, pipeline_mode=pl.Buffered(1))


def _pipeline_scratch(hg, rows, n):
    return [pltpu.VMEM((hg, rows, n), F32), pltpu.VMEM((hg, rows, n), F32),
            pltpu.VMEM((hg, 1, n), F32), pltpu.VMEM((hg, 1, n), F32)]


def _init_softmax_state(m_ref, l_ref, acc_ref):
    m_ref[...] = jnp.full(m_ref.shape, NEG_INF, F32)
    l_ref[...] = jnp.zeros(l_ref.shape, F32)
    acc_ref[...] = jnp.zeros(acc_ref.shape, F32)


def _diff_attn_kernel(q_ref, k_ref, vt_ref, bn_ref, lam_ref, g_ref, o_ref,
                      m_ref, l_ref, acc_ref, sa_ref, sb_ref, mxa_ref, mxb_ref,
                      *, tq, tk, hg, lam_init):
    qb = pl.program_id(1)
    qts = [jnp.concatenate([q_ref[2 * g], q_ref[2 * g + 1]], axis=1) for g in range(hg)]

    def k_rows(g, rows):
        return k_ref[rows, g * HEAD_W:(g + 1) * HEAD_W]

    def step(kb, nblk, biases=None):
        _softmax_block(kb, nblk, biases, qts, k_rows, vt_ref, m_ref, l_ref, acc_ref, tk)

    _init_softmax_state(m_ref, l_ref, acc_ref)
    park, consume = _pipeline_fns(DIFF_FAR_BLOCKS, qts, k_rows, vt_ref, m_ref, l_ref, acc_ref,
                                  (sa_ref, sb_ref), (mxa_ref, mxb_ref), tk)

    first = (qb == 0).astype(jnp.int32)
    kb_near = qb - 1 + first

    def near_logits():
        ss = _logits_phase(kb_near, 2, qts, k_rows, tk)
        return [s + bn_ref[g, first] for g, s in enumerate(ss)]

    def near_update(ss):
        colmax = [jnp.max(s, axis=0, keepdims=True) for s in ss]
        _update_phase(kb_near, 2, ss, colmax, vt_ref, m_ref, l_ref, acc_ref, tk)

    _key_sweep(jnp.maximum(qb - 1, 0), step, DIFF_FAR_BLOCKS, park, consume,
               near_logits, near_update)

    lp = lam_ref[...]
    lam = (jnp.exp(jnp.sum(lp[0:1] * lp[1:2], axis=1, keepdims=True))
           - jnp.exp(jnp.sum(lp[2:3] * lp[3:4], axis=1, keepdims=True)) + lam_init)
    for g in range(hg):
        o = acc_ref[g] / l_ref[g]
        d = o[:, :tq] - lam * o[:, tq:]
        ms = jnp.mean(d * d, axis=0, keepdims=True)
        d = d * lax.rsqrt(ms + EPS) * g_ref[...] * (1.0 - lam_init)
        o_ref[:, g * HEAD_W:(g + 1) * HEAD_W] = d.T.astype(o_ref.dtype)


def _diff_attn_call(qpad, k, vt4, bias_near, lam_p, sub_g, *, lam_init):
    s = k.shape[0]
    tq = tk = min(ATT_TQ, s)
    nkb = s // tk
    n = 2 * tq
    hg = DIFF_HEAD_GROUP
    return pl.pallas_call(
        functools.partial(_diff_attn_kernel, tq=tq, tk=tk, hg=hg, lam_init=lam_init),
        grid=(DA_HEADS // hg, s // tq),
        in_specs=[pl.BlockSpec((2 * hg, HEAD_W, tq), lambda h, i: (h, 0, i)),
                  _resident_spec((s, hg * HEAD_W), lambda h, i: (0, h)),
                  _resident_spec((hg, nkb, HEAD_W, tk), lambda h, i: (h, 0, 0, 0)),
                  _resident_spec((hg, 2, 2 * tk, n), lambda h, i: (h, 0, 0, 0)),
                  pl.BlockSpec((4, DA_QK), lambda h, i: (0, 0)),
                  pl.BlockSpec((HEAD_W, 1), lambda h, i: (0, 0))],
        out_specs=pl.BlockSpec((tq, hg * HEAD_W), lambda h, i: (i, h)),
        out_shape=jax.ShapeDtypeStruct((s, DA_HEADS * HEAD_W), BF),
        scratch_shapes=[pltpu.VMEM((hg, 1, n), F32), pltpu.VMEM((hg, 1, n), F32),
                        pltpu.VMEM((hg, HEAD_W, n), F32)] + _pipeline_scratch(hg, DIFF_FAR_BLOCKS * tk, n),
        compiler_params=_cparams("arbitrary", "arbitrary"),
        name="diff_attention",
    )(qpad, k, vt4, bias_near, lam_p, sub_g)


def _mla_attn_kernel(q_ref, k_ref, vt_ref, mask_ref, o_ref, m_ref, l_ref, acc_ref,
                     sa_ref, sb_ref, mxa_ref, mxb_ref, *, tk, hg):
    qb = pl.program_id(1)
    qts = [q_ref[g] for g in range(hg)]

    def k_rows(g, rows):
        return k_ref[g, rows, :]

    def step(kb, nblk, biases=None):
        _softmax_block(kb, nblk, biases, qts, k_rows, vt_ref, m_ref, l_ref, acc_ref, tk)

    _init_softmax_state(m_ref, l_ref, acc_ref)
    park, consume = _pipeline_fns(MLA_FAR_BLOCKS, qts, k_rows, vt_ref, m_ref, l_ref, acc_ref,
                                  (sa_ref, sb_ref), (mxa_ref, mxb_ref), tk)

    def near_logits():
        return [s + mask_ref[...] for s in _logits_phase(qb, 1, qts, k_rows, tk)]

    def near_update(ss):
        colmax = [jnp.max(s, axis=0, keepdims=True) for s in ss]
        _update_phase(qb, 1, ss, colmax, vt_ref, m_ref, l_ref, acc_ref, tk)

    _key_sweep(qb, step, MLA_FAR_BLOCKS, park, consume, near_logits, near_update)
    for g in range(hg):
        o = acc_ref[g] / l_ref[g]
        o_ref[:, g * HEAD_W:(g + 1) * HEAD_W] = o.T.astype(o_ref.dtype)


def _mla_attn_call(qt, k, vt4, mask_diag):
    s = k.shape[1]
    tq = tk = min(ATT_TQ, s)
    nkb = s // tk
    hg = MLA_HEAD_GROUP
    return pl.pallas_call(
        functools.partial(_mla_attn_kernel, tk=tk, hg=hg),
        grid=(MLA_HEADS // hg, s // tq),
        in_specs=[pl.BlockSpec((hg, MLA_QK_PAD, tq), lambda h, i: (h, 0, i)),
                  _resident_spec((hg, s, MLA_QK_PAD), lambda h, i: (h, 0, 0)),
                  _resident_spec((hg, nkb, HEAD_W, tk), lambda h, i: (h, 0, 0, 0)),
                  pl.BlockSpec((tk, tq), lambda h, i: (0, 0))],
        out_specs=pl.BlockSpec((tq, hg * HEAD_W), lambda h, i: (i, h)),
        out_shape=jax.ShapeDtypeStruct((s, MLA_HEADS * HEAD_W), BF),
        scratch_shapes=[pltpu.VMEM((hg, 1, tq), F32), pltpu.VMEM((hg, 1, tq), F32),
                        pltpu.VMEM((hg, HEAD_W, tq), F32)] + _pipeline_scratch(hg, MLA_FAR_BLOCKS * tk, tq),
        compiler_params=_cparams("arbitrary", "arbitrary"),
        name="mla_attention",
    )(qt, k, vt4, mask_diag)


def _stick_block(kb, mask, qts, k_ref, vt_ref, tri_neg, carry_ref, acc_ref, tk):
    hg = len(qts)
    rows = pl.ds(pl.multiple_of(kb * tk, tk), tk)
    zs = [jnp.dot(k_ref[rows, g * HEAD_W:(g + 1) * HEAD_W], qts[g], preferred_element_type=F32)
          for g in range(hg)]
    logsigs, his, los = [], [], []
    for g in range(hg):
        z = zs[g]
        sp = jnp.maximum(z, 0.0) + jnp.log2(1.0 + jnp.exp2(-jnp.abs(z)))
        logsigs.append(z - sp + carry_ref[g])
        if mask is not None:
            sp = jnp.where(mask, sp, 0.0)
        hi = sp.astype(BF)
        his.append(hi)
        los.append((sp - hi.astype(F32)).astype(BF))
        carry_ref[g] -= jnp.sum(sp, axis=0, keepdims=True)
    betweens = [jnp.dot(tri_neg, his[g], preferred_element_type=F32)
                + jnp.dot(tri_neg, los[g], preferred_element_type=F32) for g in range(hg)]
    ws = []
    for g in range(hg):
        w = jnp.exp2(logsigs[g] + betweens[g])
        if mask is not None:
            w = jnp.where(mask, w, 0.0)
        ws.append(w.astype(BF))
    pvs = [jnp.dot(vt_ref[g, kb], ws[g], preferred_element_type=F32) for g in range(hg)]
    for g in range(hg):
        acc_ref[g] += pvs[g]


def _stick_kernel(q_ref, k_ref, vt_ref, tri_ref, o_ref, carry_ref, acc_ref, *, tq, tk, hg):
    qb = pl.program_id(1)
    qts = [q_ref[g * HEAD_W:(g + 1) * HEAD_W, :] for g in range(hg)]
    tri = tri_ref[...]
    carry_ref[...] = jnp.zeros(carry_ref.shape, F32)
    acc_ref[...] = jnp.zeros(acc_ref.shape, F32)
    kpos = lax.broadcasted_iota(jnp.int32, (tk, tq), 0)
    qpos = lax.broadcasted_iota(jnp.int32, (tk, tq), 1)
    _stick_block(qb, kpos < qpos, qts, k_ref, vt_ref, tri, carry_ref, acc_ref, tk)

    def more(state):
        i, cmax = state
        return jnp.logical_and(i < qb, cmax > SB_LOG2W_UNDERFLOW)

    def earlier(state):
        i, _ = state
        _stick_block(qb - 1 - i, None, qts, k_ref, vt_ref, tri, carry_ref, acc_ref, tk)
        return i + 1, jnp.max(carry_ref[...])

    lax.while_loop(more, earlier, (jnp.int32(0), jnp.max(carry_ref[...])))
    for g in range(hg):
        o_ref[:, g * HEAD_W:(g + 1) * HEAD_W] = acc_ref[g].T.astype(o_ref.dtype)


def _stick_call(qt, k, vt4, tri):
    s = k.shape[0]
    tq = tk = min(ATT_TQ, s)
    nkb = s // tk
    hg = SB_HEAD_GROUP
    return pl.pallas_call(
        functools.partial(_stick_kernel, tq=tq, tk=tk, hg=hg),
        grid=(SB_HEADS // hg, s // tq),
        in_specs=[pl.BlockSpec((hg * HEAD_W, tq), lambda h, i: (h, i)),
                  _resident_spec((s, hg * HEAD_W), lambda h, i: (0, h)),
                  _resident_spec((hg, nkb, HEAD_W, tk), lambda h, i: (h, 0, 0, 0)),
                  pl.BlockSpec((tk, tk), lambda h, i: (0, 0))],
        out_specs=pl.BlockSpec((tq, hg * HEAD_W), lambda h, i: (i, h)),
        out_shape=jax.ShapeDtypeStruct((s, SB_HEADS * HEAD_W), BF),
        scratch_shapes=[pltpu.VMEM((hg, 1, tq), F32), pltpu.VMEM((hg, HEAD_W, tq), F32)],
        compiler_params=_cparams("arbitrary", "arbitrary"),
        name="stick_breaking",
    )(qt, k, vt4, tri)


def _merge_kernel(ht_ref, ya_ref, yb_ref, yc_ref, wga_ref, wgb_ref, wgc_ref, wb_ref, o_ref):
    ht = ht_ref[...]
    acc = None
    for n, (y_ref, wg_ref) in enumerate(((ya_ref, wga_ref), (yb_ref, wgb_ref), (yc_ref, wgc_ref))):
        gate = jax.nn.sigmoid(jnp.dot(wg_ref[...], ht, preferred_element_type=F32)).T
        up = jnp.dot(y_ref[...], wb_ref[n], preferred_element_type=F32)
        acc = gate * up if acc is None else acc + gate * up
    o_ref[...] = acc.astype(o_ref.dtype)


def _merge_call(ht, ya, yb, yc, wt_tail, wb, layer):
    d, s = ht.shape
    tm = min(512, s)
    tn = MERGE_TN
    ysp = pl.BlockSpec((tm, BRANCH_WIDTH), lambda i, j: (i, 0))

    def gate_spec(n):
        blk0 = (SEG_GATE * SEG + n * d) // tn
        return pl.BlockSpec((None, tn, d), lambda i, j: (layer, blk0 + j, 0))

    return pl.pallas_call(
        _merge_kernel,
        grid=(s // tm, d // tn),
        in_specs=[pl.BlockSpec((d, tm), lambda i, j: (0, i)), ysp, ysp, ysp,
                  gate_spec(0), gate_spec(1), gate_spec(2),
                  pl.BlockSpec((None, None, N_BRANCHES, BRANCH_WIDTH, tn),
                               lambda i, j: (layer, j, 0, 0, 0))],
        out_specs=pl.BlockSpec((tm, tn), lambda i, j: (i, j)),
        out_shape=jax.ShapeDtypeStruct((s, d), BF),
        compiler_params=_cparams("arbitrary", "arbitrary"),
        name="gated_merge",
    )(ht, ya, yb, yc, wt_tail, wt_tail, wt_tail, wb)


def _mlp_kernel(h_ref, w1_ref, w2_ref, x_ref, g_ref, o_ref, acc_ref, ua_ref, ub_ref):
    f = pl.program_id(1)
    last = pl.num_programs(1) - 1

    def up(u_ref):
        u = jnp.dot(h_ref[...], w1_ref[...], preferred_element_type=F32)
        u_ref[...] = jnp.square(jnp.maximum(u, 0.0)).astype(BF)

    def down(u_ref):
        acc_ref[...] += jnp.dot(u_ref[...], w2_ref[...], preferred_element_type=F32)

    @pl.when(f == 0)
    def _():
        acc_ref[...] = jnp.zeros(acc_ref.shape, F32)
        up(ua_ref)

    middle = jnp.logical_and(f > 0, f < last)

    @pl.when(jnp.logical_and(middle, f % 2 == 1))
    def _():
        up(ub_ref)
        down(ua_ref)

    @pl.when(jnp.logical_and(middle, f % 2 == 0))
    def _():
        up(ua_ref)
        down(ub_ref)

    @pl.when(f == last)
    def _():
        down(ub_ref)
        o_ref[...] = x_ref[...] + g_ref[...] * acc_ref[...]


def _mlp_call(h, w1, w2, layer, x, g):
    s, d = h.shape
    tm = min(512, s)
    nf, _, tf = w1.shape[1:]
    assert nf % 2 == 0
    return pl.pallas_call(
        _mlp_kernel,
        grid=(s // tm, nf + 1),
        in_specs=[pl.BlockSpec((tm, d), lambda i, f: (i, 0)),
                  pl.BlockSpec((None, None, d, tf), lambda i, f: (layer, jnp.minimum(f, nf - 1), 0, 0)),
                  pl.BlockSpec((None, tf, d), lambda i, f: (layer, jnp.maximum(f - 1, 0), 0)),
                  pl.BlockSpec((tm, d), lambda i, f: (i, 0)),
                  pl.BlockSpec((1, d), lambda i, f: (0, 0))],
        out_specs=pl.BlockSpec((tm, d), lambda i, f: (i, 0)),
        out_shape=jax.ShapeDtypeStruct((s, d), F32),
        scratch_shapes=[pltpu.VMEM((tm, d), F32), pltpu.VMEM((tm, tf), BF),
                        pltpu.VMEM((tm, tf), BF)],
        compiler_params=_cparams("arbitrary", "arbitrary"),
        name="sqrelu_mlp",
    )(h, w1, w2, x, g)


def _t5_bucket(rel):
    nb = T5_BUCKETS // 2
    max_exact = nb // 2
    n = jnp.abs(rel)
    large = max_exact + (jnp.log(jnp.maximum(n, 1).astype(F32) / max_exact)
                         / math.log(T5_MAX_DIST / max_exact) * (nb - max_exact)).astype(jnp.int32)
    large = jnp.minimum(large, nb - 1)
    return jnp.where(rel > 0, nb, 0) + jnp.where(n < max_exact, n, large)


def _bias_tiles(t5_bias, tq, tk):
    table = t5_bias.astype(F32)

    def lookup(bucket):
        hit = bucket[..., None, None] == jnp.arange(T5_BUCKETS)[:, None]
        return jnp.sum(jnp.where(hit, table, 0.0), axis=-2)

    kl = jnp.arange(tk)[:, None]
    ql = jnp.arange(tq)[None, :]
    far = lookup(_t5_bucket(jnp.array(-(tk + tq), jnp.int32)))
    diag = jnp.transpose(lookup(_t5_bucket(kl - ql)) - far, (2, 0, 1))
    prev = jnp.transpose(lookup(_t5_bucket(kl - tk - ql)) - far, (2, 0, 1))
    allowed = (kl // CHUNK) <= (ql // CHUNK)
    diag = jnp.where(allowed[None], diag * LOG2E, NEG_INF)
    near = jnp.stack([jnp.concatenate([prev * LOG2E, diag], axis=1),
                      jnp.concatenate([diag, jnp.full_like(diag, NEG_INF)], axis=1)], axis=1)
    return jnp.concatenate([near, near], axis=3)


def _rope_tables(s):
    half = MLA_ROPE // 2
    inv = ROPE_BASE ** (-jnp.arange(half, dtype=F32) / half)
    ang = jnp.arange(s).astype(F32)[:, None] * inv[None, :]
    return jnp.cos(ang).T, jnp.sin(ang).T


def kernel(x, c, w_ada, b_ada, norm_mix_g, norm_mlp_g, w_in, diff_qk_g, diff_lambda, diff_subln_g,
           t5_bias, mla_q_norm_g, mla_kv_norm_g, w_q_up, w_kv_up, mla_qk_g, w_branch, w_out,
           w_mlp_in, w_mlp_out):
    b, s, d = x.shape
    assert b == 1 and d == D_MODEL
    depth = w_ada.shape[0]
    tq = tk = min(ATT_TQ, s)
    hw = MLA_NOPE + MLA_ROPE

    mod = _mod_call(jnp.broadcast_to(c, (8, d)).astype(BF), w_ada, b_ada[:, None, :])[:, 0, :]
    mod = mod.reshape(depth, N_MOD, 1, d)

    wt_head = jnp.swapaxes(w_in[:, :, :HEAD_ROWS], 1, 2).astype(BF)
    wt_tail = jnp.swapaxes(w_in[:, :, TAIL_ROW0:], 1, 2).astype(BF)
    w_br = jnp.transpose(w_branch.reshape(depth, N_BRANCHES, BRANCH_WIDTH, d // MERGE_TN, MERGE_TN),
                         (0, 3, 1, 2, 4)).astype(BF)
    w_o = w_out.astype(BF)
    w_1 = jnp.transpose(w_mlp_in.reshape(depth, d, D_FF // MLP_TF, MLP_TF), (0, 2, 1, 3)).astype(BF)
    w_2 = w_mlp_out.astype(BF)
    wt_qu = jnp.swapaxes(w_q_up, 1, 2).astype(BF)
    wkv = w_kv_up.reshape(depth, MLA_KV_LORA, MLA_HEADS, MLA_NOPE + MLA_V)
    wt_kn = jnp.swapaxes(wkv[..., :MLA_NOPE].reshape(depth, MLA_KV_LORA, -1), 1, 2).astype(BF)
    wt_v = jnp.swapaxes(wkv[..., MLA_NOPE:].reshape(depth, MLA_KV_LORA, -1), 1, 2).astype(BF)

    bias_near = _bias_tiles(t5_bias, tq, tk)
    kl = jnp.arange(tk)[:, None]
    ql = jnp.arange(tq)[None, :]
    mask_diag = jnp.where((kl // CHUNK) <= (ql // CHUNK), 0.0, NEG_INF).astype(F32)
    tri = -(jnp.arange(tk)[None, :] > jnp.arange(tk)[:, None]).astype(BF)
    cos_t, sin_t = _rope_tables(s)

    ones_col = jnp.ones((SEG, 1), F32)
    sb_scale = jnp.full((SB_HEADS * SB_DIM, 1), SB_DIM ** -0.5 * LOG2E, F32)
    da_scale = DA_QK ** -0.5 * LOG2E
    mla_scale = hw ** -0.5 * LOG2E

    x2 = x[0]
    for l in range(depth):
        sh1, sc1, g1, sh2, sc2, g2 = (mod[l, i] for i in range(N_MOD))
        lam_init = 0.8 - 0.6 * math.exp(-0.3 * l)

        ht = _norm_call(x2, norm_mix_g[l][None], sc1, sh1, transpose=True)

        gq_rows = jnp.tile(diff_qk_g[l, 0], 2 * DA_HEADS)[:, None] * da_scale
        gk_rows = jnp.tile(diff_qk_g[l, 1], 2 * DA_HEADS)[:, None]
        qa = _proj_call(wt_head, l, SEG_DQ, 1, ht, gq_rows, group=DA_QK, layout="qpad",
                        name="proj_diff_q")
        ka = _proj_call(wt_head, l, SEG_DK, 1, ht, gk_rows, group=DA_QK, layout="n",
                        name="proj_diff_k")
        va = _proj_call(wt_head, l, SEG_DV, 1, ht, ones_col, layout="t4", name="proj_diff_v")
        qc = _proj_call(wt_tail, l, SEG_SQ, 1, ht, sb_scale, layout="t", name="proj_sb_q")
        kc = _proj_call(wt_tail, l, SEG_SK, 1, ht, ones_col, layout="n", name="proj_sb_k")
        vc = _proj_call(wt_tail, l, SEG_SV, 1, ht, ones_col, layout="t4", name="proj_sb_v")
        cbt = _proj_call(wt_head, l, SEG_LAT, 1, ht, ones_col, layout="t", out_dtype=F32,
                         name="proj_mla_latent")
        gqk = mla_qk_g[l]
        qb_t, kb, vb_t = _mla_up_call(
            cbt, l, mla_q_norm_g[l][:, None], mla_kv_norm_g[l][:, None], wt_qu, wt_kn, wt_v,
            gqk[0, :MLA_NOPE, None] * mla_scale, gqk[0, MLA_NOPE:, None] * mla_scale,
            gqk[1, :MLA_NOPE, None], gqk[1, MLA_NOPE:, None], cos_t, sin_t)

        ya = _diff_attn_call(qa, ka, va, bias_near, diff_lambda[l], diff_subln_g[l][:, None],
                             lam_init=lam_init)
        yb = _mla_attn_call(qb_t, kb, vb_t, mask_diag)
        yc = _stick_call(qc, kc, vc, tri)

        merged = _merge_call(ht, ya, yb, yc, wt_tail, w_br, l)
        x2 = _resid_proj_call(merged, w_o, l, x2, g1)

        h2 = _norm_call(x2, norm_mlp_g[l][None], sc2, sh2, transpose=False)
        x2 = _mlp_call(h2, w_1, w_2, l, x2, g2)
    return x2[None]
```

```python
import functools
import math

import jax
import jax.numpy as jnp
from jax import lax
from jax.experimental import pallas as pl
from jax.experimental.pallas import tpu as pltpu

BF = jnp.bfloat16
F32 = jnp.float32

D_MODEL = 2048
CHUNK = 64
DA_HEADS = 8
DA_QK = 64
DA_V = 128
MLA_HEADS = 8
MLA_Q_LORA = 512
MLA_KV_LORA = 256
MLA_NOPE = 128
MLA_ROPE = 64
MLA_V = 128
MLA_QK_PAD = 256
ROPE_BASE = 10000.0
SB_HEADS = 8
SB_DIM = 128
BRANCH_WIDTH = 1024
N_BRANCHES = 3
T5_BUCKETS = 32
T5_MAX_DIST = 128
D_FF = 4 * D_MODEL
N_MOD = 6
EPS = 1e-6
NEG_INF = -1e30
LOG2E = 1.4426950408889634

HEAD_W = 128
ATT_TQ = 256
ATT_TK = 256
DIFF_FAR_BLOCKS = 2
MLA_FAR_BLOCKS = 4
DIFF_HEAD_GROUP = 4
MLA_HEAD_GROUP = 4
SB_HEAD_GROUP = 4
SB_LOG2W_UNDERFLOW = -152.0
V7X_VMEM_LIMIT = 56 * 1024 * 1024

SEG = 1024
HEAD_ROWS = 4 * SEG
TAIL_ROW0 = 3 * SEG + MLA_Q_LORA + MLA_KV_LORA + MLA_ROPE
SEG_DQ, SEG_DK, SEG_DV, SEG_LAT = 0, 1, 2, 3
SEG_SQ, SEG_SK, SEG_SV, SEG_GATE = 0, 1, 2, 3
LAT_KPE = MLA_Q_LORA + MLA_KV_LORA


def _cparams(*sem):
    return pltpu.CompilerParams(dimension_semantics=sem, vmem_limit_bytes=V7X_VMEM_LIMIT)


def _mod_kernel(c_ref, w_ref, b_ref, o_ref):
    w = w_ref[0].astype(BF)
    o_ref[0] = jnp.dot(c_ref[...], w, preferred_element_type=F32) + b_ref[0]


def _mod_call(c8, w_ada, b_ada3):
    depth, d, n = w_ada.shape
    tn = 1024
    return pl.pallas_call(
        _mod_kernel,
        grid=(depth, n // tn),
        in_specs=[pl.BlockSpec((8, d), lambda l, j: (0, 0)),
                  pl.BlockSpec((1, d, tn), lambda l, j: (l, 0, j)),
                  pl.BlockSpec((1, 1, tn), lambda l, j: (l, 0, j))],
        out_specs=pl.BlockSpec((1, 8, tn), lambda l, j: (l, 0, j)),
        out_shape=jax.ShapeDtypeStruct((depth, 8, n), F32),
        compiler_params=_cparams("arbitrary", "arbitrary"),
        name="adaln_mod",
    )(c8, w_ada, b_ada3)


def _norm_kernel(x_ref, g_ref, sc_ref, sh_ref, h_ref, *, transpose):
    x = x_ref[...]
    ms = jnp.mean(x * x, axis=-1, keepdims=True)
    y = x * lax.rsqrt(ms + EPS) * g_ref[...]
    h = y * (1.0 + sc_ref[...]) + sh_ref[...]
    h_ref[...] = (h.T if transpose else h).astype(BF)


def _norm_call(x, g, sc, sh, *, transpose):
    s, d = x.shape
    tm = min(512, s)
    row = pl.BlockSpec((1, d), lambda i: (0, 0))
    if transpose:
        out_shape = jax.ShapeDtypeStruct((d, s), BF)
        out_spec = pl.BlockSpec((d, tm), lambda i: (0, i))
    else:
        out_shape = jax.ShapeDtypeStruct((s, d), BF)
        out_spec = pl.BlockSpec((tm, d), lambda i: (i, 0))
    return pl.pallas_call(
        functools.partial(_norm_kernel, transpose=transpose),
        grid=(s // tm,),
        in_specs=[pl.BlockSpec((tm, d), lambda i: (i, 0)), row, row, row],
        out_specs=out_spec,
        out_shape=out_shape,
        compiler_params=_cparams("arbitrary"),
        name="adaln_rmsnorm",
    )(x, g, sc, sh)


def _proj_tile(w_ref, xt_ref, rs_ref, o_ref, group, layout, tk):
    y = jnp.dot(w_ref[...], xt_ref[...], preferred_element_type=F32)
    tn, tm = y.shape
    if group:
        y3 = y.reshape(tn // group, group, tm)
        ms = jnp.mean(y3 * y3, axis=1, keepdims=True)
        y = (y3 * lax.rsqrt(ms + EPS)).reshape(tn, tm)
    y = y * rs_ref[...]
    if layout == "t":
        o_ref[...] = y.astype(o_ref.dtype)
    elif layout == "n":
        o_ref[...] = y.T.astype(o_ref.dtype)
    elif layout == "t4":
        for a in range(tn // HEAD_W):
            for b in range(tm // tk):
                o_ref[a, b] = y[a * HEAD_W:(a + 1) * HEAD_W, b * tk:(b + 1) * tk].astype(o_ref.dtype)
    elif layout == "qpad":
        first = lax.broadcasted_iota(jnp.int32, (HEAD_W, tm), 0) < DA_QK
        for a in range(tn // HEAD_W):
            ya = y[a * HEAD_W:(a + 1) * HEAD_W]
            o_ref[2 * a] = jnp.where(first, ya, 0.0).astype(o_ref.dtype)
            o_ref[2 * a + 1] = jnp.where(first, 0.0, ya).astype(o_ref.dtype)


def _proj_kernel(w_ref, xt_ref, rs_ref, *o_refs, specs, tk):
    seg = pl.program_id(0)
    for k, (group, layout, _) in enumerate(specs):
        @pl.when(seg == k)
        def _(k=k, group=group, layout=layout):
            _proj_tile(w_ref, xt_ref, rs_ref, o_refs[k], group, layout, tk)


def _proj_call(wt, layer, seg0, ht, rowscale, specs, *, name):
    k, s = ht.shape
    nseg = len(specs)
    tn = SEG
    tm = min(1024, s)
    nj = s // tm
    tk = min(ATT_TK, s)
    out_shapes, out_specs = [], []
    for seg, (_, layout, out_dtype) in enumerate(specs):
        def tile(i, j, seg=seg):
            return jnp.where(i < seg, 0, jnp.where(i > seg, nj - 1, j))

        if layout == "t":
            out_shapes.append(jax.ShapeDtypeStruct((tn, s), out_dtype))
            out_specs.append(pl.BlockSpec((tn, tm), lambda i, j, t=tile: (0, t(i, j))))
        elif layout == "n":
            out_shapes.append(jax.ShapeDtypeStruct((s, tn), out_dtype))
            out_specs.append(pl.BlockSpec((tm, tn), lambda i, j, t=tile: (t(i, j), 0)))
        elif layout == "t4":
            out_shapes.append(jax.ShapeDtypeStruct((tn // HEAD_W, s // tk, HEAD_W, tk), out_dtype))
            out_specs.append(pl.BlockSpec((tn // HEAD_W, tm // tk, HEAD_W, tk),
                                          lambda i, j, t=tile: (0, t(i, j), 0, 0)))
        else:
            out_shapes.append(jax.ShapeDtypeStruct((2 * tn // HEAD_W, HEAD_W, s), out_dtype))
            out_specs.append(pl.BlockSpec((2 * tn // HEAD_W, HEAD_W, tm),
                                          lambda i, j, t=tile: (0, 0, t(i, j))))
    return pl.pallas_call(
        functools.partial(_proj_kernel, specs=tuple(specs), tk=tk),
        grid=(nseg, nj),
        in_specs=[pl.BlockSpec((None, tn, k), lambda i, j: (layer, seg0 + i, 0)),
                  pl.BlockSpec((k, tm), lambda i, j: (0, j)),
                  pl.BlockSpec((tn, 1), lambda i, j: (i, 0))],
        out_specs=out_specs,
        out_shape=out_shapes,
        compiler_params=_cparams("arbitrary", "arbitrary"),
        name=name,
    )(wt, ht, rowscale)


def _resid_proj_kernel(a_ref, w_ref, x_ref, g_ref, o_ref):
    y = jnp.dot(a_ref[...], w_ref[...], preferred_element_type=F32)
    o_ref[...] = x_ref[...] + g_ref[...] * y


def _resid_proj_call(a, w, layer, x, g):
    s, k = a.shape
    n = w.shape[2]
    tm = min(1024, s)
    tn = min(1024, n)
    return pl.pallas_call(
        _resid_proj_kernel,
        grid=(s // tm, n // tn),
        in_specs=[pl.BlockSpec((tm, k), lambda i, j: (i, 0)),
                  pl.BlockSpec((None, k, tn), lambda i, j: (layer, 0, j)),
                  pl.BlockSpec((tm, tn), lambda i, j: (i, j)),
                  pl.BlockSpec((1, tn), lambda i, j: (0, j))],
        out_specs=pl.BlockSpec((tm, tn), lambda i, j: (i, j)),
        out_shape=jax.ShapeDtypeStruct((s, n), F32),
        compiler_params=_cparams("arbitrary", "arbitrary"),
        name="out_proj_residual",
    )(a, w, x, g)


def _rms_rows(x, g):
    return x * lax.rsqrt(jnp.mean(x * x, axis=0, keepdims=True) + EPS) * g


def _rope_rows(x, cos_t, sin_t):
    half = MLA_ROPE // 2
    x1, x2 = x[:half], x[half:]
    return x1 * cos_t - x2 * sin_t, x1 * sin_t + x2 * cos_t


def _mla_up_kernel(cb_ref, gq_ref, gkv_ref, wq_ref, wk_ref, wv_ref, gqn_ref, gqp_ref, gkn_ref,
                   gkp_ref, cost_ref, sint_ref, qt_ref, k_ref, vt_ref, *, tk):
    tm = cb_ref.shape[1]
    hw = MLA_NOPE + MLA_ROPE
    cos_t = cost_ref[...]
    sin_t = sint_ref[...]
    cqn = _rms_rows(cb_ref[0:MLA_Q_LORA], gq_ref[...]).astype(BF)
    qt = jnp.dot(wq_ref[...], cqn, preferred_element_type=F32)
    qzero = jnp.zeros((MLA_QK_PAD - hw, tm), BF)
    for h in range(MLA_HEADS):
        nope = _rms_rows(qt[h * hw:h * hw + MLA_NOPE], gqn_ref[...])
        pe1, pe2 = _rope_rows(_rms_rows(qt[h * hw + MLA_NOPE:(h + 1) * hw], gqp_ref[...]), cos_t, sin_t)
        qt_ref[h, 0:MLA_NOPE] = nope.astype(BF)
        qt_ref[h, MLA_NOPE:MLA_NOPE + MLA_ROPE // 2] = pe1.astype(BF)
        qt_ref[h, MLA_NOPE + MLA_ROPE // 2:hw] = pe2.astype(BF)
        qt_ref[h, hw:MLA_QK_PAD] = qzero
    ckvn = _rms_rows(cb_ref[MLA_Q_LORA:LAT_KPE], gkv_ref[...]).astype(BF)
    kt = jnp.dot(wk_ref[...], ckvn, preferred_element_type=F32)
    vt = jnp.dot(wv_ref[...], ckvn, preferred_element_type=F32)
    kpe1, kpe2 = _rope_rows(_rms_rows(cb_ref[LAT_KPE:LAT_KPE + MLA_ROPE], gkp_ref[...]), cos_t, sin_t)
    kzero = jnp.zeros((MLA_QK_PAD - hw, tm), F32)
    for h in range(MLA_HEADS):
        kn = _rms_rows(kt[h * MLA_NOPE:(h + 1) * MLA_NOPE], gkn_ref[...])
        k_ref[h] = jnp.concatenate([kn, kpe1, kpe2, kzero], axis=0).T.astype(BF)
        for b in range(tm // tk):
            vt_ref[h, b] = vt[h * MLA_V:(h + 1) * MLA_V, b * tk:(b + 1) * tk].astype(BF)


def _mla_up_call(cbt, layer, gq, gkv, wq, wk, wv, gqn, gqp, gkn, gkp, cos_t, sin_t):
    s = cbt.shape[1]
    tm = min(512, s)
    tk = min(ATT_TK, s)
    hw = MLA_NOPE + MLA_ROPE
    half = MLA_ROPE // 2

    def const(shape):
        return pl.BlockSpec(shape, lambda i: tuple(0 for _ in shape))

    def layer_w(shape):
        return pl.BlockSpec((None,) + shape, lambda i: (layer, 0, 0))

    return pl.pallas_call(
        functools.partial(_mla_up_kernel, tk=tk),
        grid=(s // tm,),
        in_specs=[pl.BlockSpec((cbt.shape[0], tm), lambda i: (0, i)),
                  const((MLA_Q_LORA, 1)), const((MLA_KV_LORA, 1)),
                  layer_w((MLA_HEADS * hw, MLA_Q_LORA)),
                  layer_w((MLA_HEADS * MLA_NOPE, MLA_KV_LORA)),
                  layer_w((MLA_HEADS * MLA_V, MLA_KV_LORA)),
                  const((MLA_NOPE, 1)), const((MLA_ROPE, 1)), const((MLA_NOPE, 1)),
                  const((MLA_ROPE, 1)),
                  pl.BlockSpec((half, tm), lambda i: (0, i)),
                  pl.BlockSpec((half, tm), lambda i: (0, i))],
        out_specs=[pl.BlockSpec((MLA_HEADS, MLA_QK_PAD, tm), lambda i: (0, 0, i)),
                   pl.BlockSpec((MLA_HEADS, tm, MLA_QK_PAD), lambda i: (0, i, 0)),
                   pl.BlockSpec((MLA_HEADS, tm // tk, MLA_V, tk), lambda i: (0, i, 0, 0))],
        out_shape=[jax.ShapeDtypeStruct((MLA_HEADS, MLA_QK_PAD, s), BF),
                   jax.ShapeDtypeStruct((MLA_HEADS, s, MLA_QK_PAD), BF),
                   jax.ShapeDtypeStruct((MLA_HEADS, s // tk, MLA_V, tk), BF)],
        compiler_params=_cparams("arbitrary"),
        name="mla_up",
    )(cbt, gq, gkv, wq, wk, wv, gqn, gqp, gkn, gkp, cos_t, sin_t)


def _logits_phase(kb, nblk, qts, k_rows, tk):
    rows = pl.ds(pl.multiple_of(kb * tk, tk), nblk * tk)
    return [jnp.dot(k_rows(g, rows), qts[g], preferred_element_type=F32) for g in range(len(qts))]


def _update_phase(kb, nblk, logits, colmax, vt_ref, m_ref, l_ref, acc_ref, tk):
    hg = len(logits)
    alphas, ps = [], []
    for g in range(hg):
        m_old = m_ref[g]
        m_new = jnp.maximum(m_old, colmax[g])
        alpha = jnp.exp2(m_old - m_new)
        p = jnp.exp2(logits[g] - m_new)
        l_ref[g] = alpha * l_ref[g] + jnp.sum(p, axis=0, keepdims=True)
        m_ref[g] = m_new
        alphas.append(alpha)
        ps.append(p.astype(BF))
    pvs = []
    for g in range(hg):
        pv = jnp.dot(vt_ref[g, kb], ps[g][0:tk], preferred_element_type=F32)
        for j in range(1, nblk):
            pv += jnp.dot(vt_ref[g, kb + j], ps[g][j * tk:(j + 1) * tk], preferred_element_type=F32)
        pvs.append(pv)
    for g in range(hg):
        acc_ref[g] = alphas[g] * acc_ref[g] + pvs[g]


def _softmax_block(kb, nblk, biases, qts, k_rows, vt_ref, m_ref, l_ref, acc_ref, tk):
    ss = _logits_phase(kb, nblk, qts, k_rows, tk)
    if biases is not None:
        ss = [s + b for s, b in zip(ss, biases)]
    colmax = [jnp.max(s, axis=0, keepdims=True) for s in ss]
    _update_phase(kb, nblk, ss, colmax, vt_ref, m_ref, l_ref, acc_ref, tk)


def _key_sweep(n_far, step, far_blocks, park, consume, near_logits, near_update):
    n_big = n_far // far_blocks
    done = n_big * far_blocks
    size = far_blocks // 2
    while size >= 1:
        take = ((n_far - done) // size) * size

        @pl.when(take > 0)
        def _(done=done, size=size):
            step(done, size)

        done = done + take
        size //= 2

    @pl.when(n_big > 0)
    def _():
        park(0, 0)
        n_pairs = (n_big - 1) // 2

        def pair(j, carry):
            kb = 2 * j * far_blocks
            park(kb + far_blocks, 1)
            consume(kb, 0)
            park(kb + 2 * far_blocks, 0)
            consume(kb + far_blocks, 1)
            return carry

        lax.fori_loop(0, n_pairs, pair, 0)
        kb = 2 * n_pairs * far_blocks
        left = n_big - 2 * n_pairs

        @pl.when(left == 2)
        def _():
            park(kb + far_blocks, 1)
            consume(kb, 0)
            near = near_logits()
            consume(kb + far_blocks, 1)
            near_update(near)

        @pl.when(left == 1)
        def _():
            near = near_logits()
            consume(kb, 0)
            near_update(near)

    @pl.when(n_big == 0)
    def _():
        near_update(near_logits())


def _pipeline_fns(far_blocks, qts, k_rows, vt_ref, m_ref, l_ref, acc_ref, s_refs, mx_refs, tk):
    def park(kb, slot):
        for g, s in enumerate(_logits_phase(kb, far_blocks, qts, k_rows, tk)):
            s_refs[slot][g] = s
            mx_refs[slot][g] = jnp.max(s, axis=0, keepdims=True)

    def consume(kb, slot):
        hg = len(qts)
        _update_phase(kb, far_blocks, [s_refs[slot][g] for g in range(hg)],
                      [mx_refs[slot][g] for g in range(hg)], vt_ref, m_ref, l_ref, acc_ref, tk)

    return park, consume


def _resident_spec(shape, index_map):
    return pl.BlockSpec(shape, index_map, pipeline_mode=pl.Buffered(1))


def _pipeline_scratch(hg, rows, n):
    return [pltpu.VMEM((hg, rows, n), F32), pltpu.VMEM((hg, rows, n), F32),
            pltpu.VMEM((hg, 1, n), F32), pltpu.VMEM((hg, 1, n), F32)]


def _init_softmax_state(m_ref, l_ref, acc_ref):
    m_ref[...] = jnp.full(m_ref.shape, NEG_INF, F32)
    l_ref[...] = jnp.zeros(l_ref.shape, F32)
    acc_ref[...] = jnp.zeros(acc_ref.shape, F32)


def _diff_attn_kernel(q_ref, k_ref, vt_ref, bn_ref, lam_ref, g_ref, o_ref,
                      m_ref, l_ref, acc_ref, sa_ref, sb_ref, mxa_ref, mxb_ref,
                      *, tq, tk, hg, lam_init):
    qb = pl.program_id(1)
    qts = [jnp.concatenate([q_ref[2 * g], q_ref[2 * g + 1]], axis=1) for g in range(hg)]

    def k_rows(g, rows):
        return k_ref[rows, g * HEAD_W:(g + 1) * HEAD_W]

    def step(kb, nblk, biases=None):
        _softmax_block(kb, nblk, biases, qts, k_rows, vt_ref, m_ref, l_ref, acc_ref, tk)

    _init_softmax_state(m_ref, l_ref, acc_ref)
    park, consume = _pipeline_fns(DIFF_FAR_BLOCKS, qts, k_rows, vt_ref, m_ref, l_ref, acc_ref,
                                  (sa_ref, sb_ref), (mxa_ref, mxb_ref), tk)

    first = (qb == 0).astype(jnp.int32)
    kb_near = qb - 1 + first

    def near_logits():
        ss = _logits_phase(kb_near, 2, qts, k_rows, tk)
        return [s + bn_ref[g, first] for g, s in enumerate(ss)]

    def near_update(ss):
        colmax = [jnp.max(s, axis=0, keepdims=True) for s in ss]
        _update_phase(kb_near, 2, ss, colmax, vt_ref, m_ref, l_ref, acc_ref, tk)

    _key_sweep(jnp.maximum(qb - 1, 0), step, DIFF_FAR_BLOCKS, park, consume,
               near_logits, near_update)

    lp = lam_ref[...]
    lam = (jnp.exp(jnp.sum(lp[0:1] * lp[1:2], axis=1, keepdims=True))
           - jnp.exp(jnp.sum(lp[2:3] * lp[3:4], axis=1, keepdims=True)) + lam_init)
    for g in range(hg):
        o = acc_ref[g] / l_ref[g]
        d = o[:, :tq] - lam * o[:, tq:]
        ms = jnp.mean(d * d, axis=0, keepdims=True)
        d = d * lax.rsqrt(ms + EPS) * g_ref[...] * (1.0 - lam_init)
        o_ref[:, g * HEAD_W:(g + 1) * HEAD_W] = d.T.astype(o_ref.dtype)


def _diff_attn_call(qpad, k, vt4, bias_near, lam_p, sub_g, *, lam_init):
    s = k.shape[0]
    tq = tk = min(ATT_TQ, s)
    nkb = s // tk
    n = 2 * tq
    hg = DIFF_HEAD_GROUP
    return pl.pallas_call(
        functools.partial(_diff_attn_kernel, tq=tq, tk=tk, hg=hg, lam_init=lam_init),
        grid=(DA_HEADS // hg, s // tq),
        in_specs=[pl.BlockSpec((2 * hg, HEAD_W, tq), lambda h, i: (h, 0, i)),
                  _resident_spec((s, hg * HEAD_W), lambda h, i: (0, h)),
                  _resident_spec((hg, nkb, HEAD_W, tk), lambda h, i: (h, 0, 0, 0)),
                  _resident_spec((hg, 2, 2 * tk, n), lambda h, i: (h, 0, 0, 0)),
                  pl.BlockSpec((4, DA_QK), lambda h, i: (0, 0)),
                  pl.BlockSpec((HEAD_W, 1), lambda h, i: (0, 0))],
        out_specs=pl.BlockSpec((tq, hg * HEAD_W), lambda h, i: (i, h)),
        out_shape=jax.ShapeDtypeStruct((s, DA_HEADS * HEAD_W), BF),
        scratch_shapes=[pltpu.VMEM((hg, 1, n), F32), pltpu.VMEM((hg, 1, n), F32),
                        pltpu.VMEM((hg, HEAD_W, n), F32)] + _pipeline_scratch(hg, DIFF_FAR_BLOCKS * tk, n),
        compiler_params=_cparams("arbitrary", "arbitrary"),
        name="diff_attention",
    )(qpad, k, vt4, bias_near, lam_p, sub_g)


def _mla_attn_kernel(q_ref, k_ref, vt_ref, mask_ref, o_ref, m_ref, l_ref, acc_ref,
                     sa_ref, sb_ref, mxa_ref, mxb_ref, *, tk, hg):
    qb = pl.program_id(1)
    qts = [q_ref[g] for g in range(hg)]

    def k_rows(g, rows):
        return k_ref[g, rows, :]

    def step(kb, nblk, biases=None):
        _softmax_block(kb, nblk, biases, qts, k_rows, vt_ref, m_ref, l_ref, acc_ref, tk)

    _init_softmax_state(m_ref, l_ref, acc_ref)
    park, consume = _pipeline_fns(MLA_FAR_BLOCKS, qts, k_rows, vt_ref, m_ref, l_ref, acc_ref,
                                  (sa_ref, sb_ref), (mxa_ref, mxb_ref), tk)

    def near_logits():
        return [s + mask_ref[...] for s in _logits_phase(qb, 1, qts, k_rows, tk)]

    def near_update(ss):
        colmax = [jnp.max(s, axis=0, keepdims=True) for s in ss]
        _update_phase(qb, 1, ss, colmax, vt_ref, m_ref, l_ref, acc_ref, tk)

    _key_sweep(qb, step, MLA_FAR_BLOCKS, park, consume, near_logits, near_update)
    for g in range(hg):
        o = acc_ref[g] / l_ref[g]
        o_ref[:, g * HEAD_W:(g + 1) * HEAD_W] = o.T.astype(o_ref.dtype)


def _mla_attn_call(qt, k, vt4, mask_diag):
    s = k.shape[1]
    tq = tk = min(ATT_TQ, s)
    nkb = s // tk
    hg = MLA_HEAD_GROUP
    return pl.pallas_call(
        functools.partial(_mla_attn_kernel, tk=tk, hg=hg),
        grid=(MLA_HEADS // hg, s // tq),
        in_specs=[pl.BlockSpec((hg, MLA_QK_PAD, tq), lambda h, i: (h, 0, i)),
                  _resident_spec((hg, s, MLA_QK_PAD), lambda h, i: (h, 0, 0)),
                  _resident_spec((hg, nkb, HEAD_W, tk), lambda h, i: (h, 0, 0, 0)),
                  pl.BlockSpec((tk, tq), lambda h, i: (0, 0))],
        out_specs=pl.BlockSpec((tq, hg * HEAD_W), lambda h, i: (i, h)),
        out_shape=jax.ShapeDtypeStruct((s, MLA_HEADS * HEAD_W), BF),
        scratch_shapes=[pltpu.VMEM((hg, 1, tq), F32), pltpu.VMEM((hg, 1, tq), F32),
                        pltpu.VMEM((hg, HEAD_W, tq), F32)] + _pipeline_scratch(hg, MLA_FAR_BLOCKS * tk, tq),
        compiler_params=_cparams("arbitrary", "arbitrary"),
        name="mla_attention",
    )(qt, k, vt4, mask_diag)


def _stick_block(kb, mask, qts, k_ref, vt_ref, tri_neg, carry_ref, acc_ref, tk):
    hg = len(qts)
    rows = pl.ds(pl.multiple_of(kb * tk, tk), tk)
    zs = [jnp.dot(k_ref[rows, g * HEAD_W:(g + 1) * HEAD_W], qts[g], preferred_element_type=F32)
          for g in range(hg)]
    logsigs, his, los = [], [], []
    for g in range(hg):
        z = zs[g]
        sp = jnp.maximum(z, 0.0) + jnp.log2(1.0 + jnp.exp2(-jnp.abs(z)))
        logsigs.append(z - sp + carry_ref[g])
        if mask is not None:
            sp = jnp.where(mask, sp, 0.0)
        hi = sp.astype(BF)
        his.append(hi)
        los.append((sp - hi.astype(F32)).astype(BF))
        carry_ref[g] -= jnp.sum(sp, axis=0, keepdims=True)
    betweens = [jnp.dot(tri_neg, his[g], preferred_element_type=F32)
                + jnp.dot(tri_neg, los[g], preferred_element_type=F32) for g in range(hg)]
    ws = []
    for g in range(hg):
        w = jnp.exp2(logsigs[g] + betweens[g])
        if mask is not None:
            w = jnp.where(mask, w, 0.0)
        ws.append(w.astype(BF))
    pvs = [jnp.dot(vt_ref[g, kb], ws[g], preferred_element_type=F32) for g in range(hg)]
    for g in range(hg):
        acc_ref[g] += pvs[g]


def _stick_kernel(q_ref, k_ref, vt_ref, tri_ref, o_ref, carry_ref, acc_ref, *, tq, tk, hg):
    qb = pl.program_id(1)
    qts = [q_ref[g * HEAD_W:(g + 1) * HEAD_W, :] for g in range(hg)]
    tri = tri_ref[...]
    carry_ref[...] = jnp.zeros(carry_ref.shape, F32)
    acc_ref[...] = jnp.zeros(acc_ref.shape, F32)
    kpos = lax.broadcasted_iota(jnp.int32, (tk, tq), 0)
    qpos = lax.broadcasted_iota(jnp.int32, (tk, tq), 1)
    _stick_block(qb, kpos < qpos, qts, k_ref, vt_ref, tri, carry_ref, acc_ref, tk)

    def more(state):
        i, cmax = state
        return jnp.logical_and(i < qb, cmax > SB_LOG2W_UNDERFLOW)

    def earlier(state):
        i, _ = state
        _stick_block(qb - 1 - i, None, qts, k_ref, vt_ref, tri, carry_ref, acc_ref, tk)
        return i + 1, jnp.max(carry_ref[...])

    lax.while_loop(more, earlier, (jnp.int32(0), jnp.max(carry_ref[...])))
    for g in range(hg):
        o_ref[:, g * HEAD_W:(g + 1) * HEAD_W] = acc_ref[g].T.astype(o_ref.dtype)


def _stick_call(qt, k, vt4, tri):
    s = k.shape[0]
    tq = tk = min(ATT_TQ, s)
    nkb = s // tk
    hg = SB_HEAD_GROUP
    return pl.pallas_call(
        functools.partial(_stick_kernel, tq=tq, tk=tk, hg=hg),
        grid=(SB_HEADS // hg, s // tq),
        in_specs=[pl.BlockSpec((hg * HEAD_W, tq), lambda h, i: (h, i)),
                  _resident_spec((s, hg * HEAD_W), lambda h, i: (0, h)),
                  _resident_spec((hg, nkb, HEAD_W, tk), lambda h, i: (h, 0, 0, 0)),
                  pl.BlockSpec((tk, tk), lambda h, i: (0, 0))],
        out_specs=pl.BlockSpec((tq, hg * HEAD_W), lambda h, i: (i, h)),
        out_shape=jax.ShapeDtypeStruct((s, SB_HEADS * HEAD_W), BF),
        scratch_shapes=[pltpu.VMEM((hg, 1, tq), F32), pltpu.VMEM((hg, HEAD_W, tq), F32)],
        compiler_params=_cparams("arbitrary", "arbitrary"),
        name="stick_breaking",
    )(qt, k, vt4, tri)


def _merge_kernel(ht_ref, ya_ref, yb_ref, yc_ref, wga_ref, wgb_ref, wgc_ref, wb_ref, o_ref):
    ht = ht_ref[...]
    acc = None
    for n, (y_ref, wg_ref) in enumerate(((ya_ref, wga_ref), (yb_ref, wgb_ref), (yc_ref, wgc_ref))):
        gate = jax.nn.sigmoid(jnp.dot(wg_ref[...], ht, preferred_element_type=F32)).T
        up = jnp.dot(y_ref[...], wb_ref[n], preferred_element_type=F32)
        acc = gate * up if acc is None else acc + gate * up
    o_ref[...] = acc.astype(o_ref.dtype)


def _merge_call(ht, ya, yb, yc, wt_tail, wb, layer):
    d, s = ht.shape
    tm = min(512, s)
    tn = 512
    ysp = pl.BlockSpec((tm, BRANCH_WIDTH), lambda i, j: (i, 0))

    def gate_spec(n):
        blk0 = (SEG_GATE * SEG + n * d) // tn
        return pl.BlockSpec((None, tn, d), lambda i, j: (layer, blk0 + j, 0))

    return pl.pallas_call(
        _merge_kernel,
        grid=(s // tm, d // tn),
        in_specs=[pl.BlockSpec((d, tm), lambda i, j: (0, i)), ysp, ysp, ysp,
                  gate_spec(0), gate_spec(1), gate_spec(2),
                  pl.BlockSpec((None, N_BRANCHES, BRANCH_WIDTH, tn), lambda i, j: (layer, 0, 0, j))],
        out_specs=pl.BlockSpec((tm, tn), lambda i, j: (i, j)),
        out_shape=jax.ShapeDtypeStruct((s, d), BF),
        compiler_params=_cparams("arbitrary", "arbitrary"),
        name="gated_merge",
    )(ht, ya, yb, yc, wt_tail, wt_tail, wt_tail, wb)


def _mlp_kernel(h_ref, w1_ref, w2_ref, x_ref, g_ref, o_ref, acc_ref):
    f = pl.program_id(1)

    @pl.when(f == 0)
    def _():
        acc_ref[...] = jnp.zeros(acc_ref.shape, F32)

    u = jnp.dot(h_ref[...], w1_ref[...], preferred_element_type=F32)
    u = jnp.square(jnp.maximum(u, 0.0)).astype(BF)
    acc_ref[...] += jnp.dot(u, w2_ref[...], preferred_element_type=F32)

    @pl.when(f == pl.num_programs(1) - 1)
    def _():
        o_ref[...] = x_ref[...] + g_ref[...] * acc_ref[...]


def _mlp_call(h, w1, w2, layer, x, g):
    s, d = h.shape
    ff = w1.shape[2]
    tm = min(512, s)
    tf = 1024
    return pl.pallas_call(
        _mlp_kernel,
        grid=(s // tm, ff // tf),
        in_specs=[pl.BlockSpec((tm, d), lambda i, f: (i, 0)),
                  pl.BlockSpec((None, d, tf), lambda i, f: (layer, 0, f)),
                  pl.BlockSpec((None, tf, d), lambda i, f: (layer, f, 0)),
                  pl.BlockSpec((tm, d), lambda i, f: (i, 0)),
                  pl.BlockSpec((1, d), lambda i, f: (0, 0))],
        out_specs=pl.BlockSpec((tm, d), lambda i, f: (i, 0)),
        out_shape=jax.ShapeDtypeStruct((s, d), F32),
        scratch_shapes=[pltpu.VMEM((tm, d), F32)],
        compiler_params=_cparams("arbitrary", "arbitrary"),
        name="sqrelu_mlp",
    )(h, w1, w2, x, g)


def _t5_bucket(rel):
    nb = T5_BUCKETS // 2
    max_exact = nb // 2
    n = jnp.abs(rel)
    large = max_exact + (jnp.log(jnp.maximum(n, 1).astype(F32) / max_exact)
                         / math.log(T5_MAX_DIST / max_exact) * (nb - max_exact)).astype(jnp.int32)
    large = jnp.minimum(large, nb - 1)
    return jnp.where(rel > 0, nb, 0) + jnp.where(n < max_exact, n, large)


def _bias_tiles(t5_bias, tq, tk):
    table = t5_bias.astype(F32)

    def lookup(bucket):
        hit = bucket[..., None, None] == jnp.arange(T5_BUCKETS)[:, None]
        return jnp.sum(jnp.where(hit, table, 0.0), axis=-2)

    kl = jnp.arange(tk)[:, None]
    ql = jnp.arange(tq)[None, :]
    far = lookup(_t5_bucket(jnp.array(-(tk + tq), jnp.int32)))
    diag = jnp.transpose(lookup(_t5_bucket(kl - ql)) - far, (2, 0, 1))
    prev = jnp.transpose(lookup(_t5_bucket(kl - tk - ql)) - far, (2, 0, 1))
    allowed = (kl // CHUNK) <= (ql // CHUNK)
    diag = jnp.where(allowed[None], diag * LOG2E, NEG_INF)
    near = jnp.stack([jnp.concatenate([prev * LOG2E, diag], axis=1),
                      jnp.concatenate([diag, jnp.full_like(diag, NEG_INF)], axis=1)], axis=1)
    return jnp.concatenate([near, near], axis=3)


def _rope_tables(s):
    half = MLA_ROPE // 2
    inv = ROPE_BASE ** (-jnp.arange(half, dtype=F32) / half)
    ang = jnp.arange(s).astype(F32)[:, None] * inv[None, :]
    return jnp.cos(ang).T, jnp.sin(ang).T


def kernel(x, c, w_ada, b_ada, norm_mix_g, norm_mlp_g, w_in, diff_qk_g, diff_lambda, diff_subln_g,
           t5_bias, mla_q_norm_g, mla_kv_norm_g, w_q_up, w_kv_up, mla_qk_g, w_branch, w_out,
           w_mlp_in, w_mlp_out):
    b, s, d = x.shape
    assert b == 1 and d == D_MODEL
    depth = w_ada.shape[0]
    tq = tk = min(ATT_TQ, s)
    hw = MLA_NOPE + MLA_ROPE

    mod = _mod_call(jnp.broadcast_to(c, (8, d)).astype(BF), w_ada, b_ada[:, None, :])[:, 0, :]
    mod = mod.reshape(depth, N_MOD, 1, d)

    wt_head = jnp.swapaxes(w_in[:, :, :HEAD_ROWS], 1, 2).astype(BF)
    wt_tail = jnp.swapaxes(w_in[:, :, TAIL_ROW0:], 1, 2).astype(BF)
    w_br = w_branch.astype(BF)
    w_o = w_out.astype(BF)
    w_1 = w_mlp_in.astype(BF)
    w_2 = w_mlp_out.astype(BF)
    wt_qu = jnp.swapaxes(w_q_up, 1, 2).astype(BF)
    wkv = w_kv_up.reshape(depth, MLA_KV_LORA, MLA_HEADS, MLA_NOPE + MLA_V)
    wt_kn = jnp.swapaxes(wkv[..., :MLA_NOPE].reshape(depth, MLA_KV_LORA, -1), 1, 2).astype(BF)
    wt_v = jnp.swapaxes(wkv[..., MLA_NOPE:].reshape(depth, MLA_KV_LORA, -1), 1, 2).astype(BF)

    bias_near = _bias_tiles(t5_bias, tq, tk)
    kl = jnp.arange(tk)[:, None]
    ql = jnp.arange(tq)[None, :]
    mask_diag = jnp.where((kl // CHUNK) <= (ql // CHUNK), 0.0, NEG_INF).astype(F32)
    tri = -(jnp.arange(tk)[None, :] > jnp.arange(tk)[:, None]).astype(BF)
    cos_t, sin_t = _rope_tables(s)

    ones_col = jnp.ones((SEG, 1), F32)
    sb_scale = jnp.full((SB_HEADS * SB_DIM, 1), SB_DIM ** -0.5 * LOG2E, F32)
    da_scale = DA_QK ** -0.5 * LOG2E
    mla_scale = hw ** -0.5 * LOG2E

    x2 = x[0]
    for l in range(depth):
        sh1, sc1, g1, sh2, sc2, g2 = (mod[l, i] for i in range(N_MOD))
        lam_init = 0.8 - 0.6 * math.exp(-0.3 * l)

        ht = _norm_call(x2, norm_mix_g[l][None], sc1, sh1, transpose=True)

        gq_rows = jnp.tile(diff_qk_g[l, 0], 2 * DA_HEADS)[:, None] * da_scale
        gk_rows = jnp.tile(diff_qk_g[l, 1], 2 * DA_HEADS)[:, None]
        qa, ka, va, cbt = _proj_call(
            wt_head, l, SEG_DQ, ht, jnp.concatenate([gq_rows, gk_rows, ones_col, ones_col]),
            [(DA_QK, "qpad", BF), (DA_QK, "n", BF), (0, "t4", BF), (0, "t", F32)], name="proj_head")
        qc, kc, vc = _proj_call(
            wt_tail, l, SEG_SQ, ht, jnp.concatenate([sb_scale, ones_col, ones_col]),
            [(0, "t", BF), (0, "n", BF), (0, "t4", BF)], name="proj_tail")
        gqk = mla_qk_g[l]
        qb_t, kb, vb_t = _mla_up_call(
            cbt, l, mla_q_norm_g[l][:, None], mla_kv_norm_g[l][:, None], wt_qu, wt_kn, wt_v,
            gqk[0, :MLA_NOPE, None] * mla_scale, gqk[0, MLA_NOPE:, None] * mla_scale,
            gqk[1, :MLA_NOPE, None], gqk[1, MLA_NOPE:, None], cos_t, sin_t)

        ya = _diff_attn_call(qa, ka, va, bias_near, diff_lambda[l], diff_subln_g[l][:, None],
                             lam_init=lam_init)
        yb = _mla_attn_call(qb_t, kb, vb_t, mask_diag)
        yc = _stick_call(qc, kc, vc, tri)

        merged = _merge_call(ht, ya, yb, yc, wt_tail, w_br, l)
        x2 = _resid_proj_call(merged, w_o, l, x2, g1)

        h2 = _norm_call(x2, norm_mlp_g[l][None], sc2, sh2, transpose=False)
        x2 = _mlp_call(h2, w_1, w_2, l, x2, g2)
    return x2[None]
```

```python
import functools
import math

import jax
import jax.numpy as jnp
from jax import lax
from jax.experimental import pallas as pl
from jax.experimental.pallas import tpu as pltpu

BF = jnp.bfloat16
F32 = jnp.float32

D_MODEL = 2048
CHUNK = 64
DA_HEADS = 8
DA_QK = 64
DA_V = 128
MLA_HEADS = 8
MLA_Q_LORA = 512
MLA_KV_LORA = 256
MLA_NOPE = 128
MLA_ROPE = 64
MLA_V = 128
MLA_QK_PAD = 256
ROPE_BASE = 10000.0
SB_HEADS = 8
SB_DIM = 128
BRANCH_WIDTH = 1024
N_BRANCHES = 3
T5_BUCKETS = 32
T5_MAX_DIST = 128
D_FF = 4 * D_MODEL
N_MOD = 6
EPS = 1e-6
NEG_INF = -1e30
LOG2E = 1.4426950408889634

HEAD_W = 128
ATT_TQ = 256
ATT_TK = 256
DIFF_FAR_BLOCKS = 4
MLA_FAR_BLOCKS = 4
DIFF_HEAD_GROUP = 4
MLA_HEAD_GROUP = 4
SB_HEAD_GROUP = 4
SB_LOG2W_UNDERFLOW = -152.0
V7X_VMEM_LIMIT = 56 * 1024 * 1024

SEG = 1024
HEAD_ROWS = 4 * SEG
TAIL_ROW0 = 3 * SEG + MLA_Q_LORA + MLA_KV_LORA + MLA_ROPE
SEG_DQ, SEG_DK, SEG_DV, SEG_LAT = 0, 1, 2, 3
SEG_SQ, SEG_SK, SEG_SV, SEG_GATE = 0, 1, 2, 3
LAT_KPE = MLA_Q_LORA + MLA_KV_LORA


def _cparams(*sem):
    return pltpu.CompilerParams(dimension_semantics=sem, vmem_limit_bytes=V7X_VMEM_LIMIT)


def _mod_kernel(c_ref, w_ref, b_ref, o_ref):
    w = w_ref[0].astype(BF)
    o_ref[0] = jnp.dot(c_ref[...], w, preferred_element_type=F32) + b_ref[0]


def _mod_call(c8, w_ada, b_ada3):
    depth, d, n = w_ada.shape
    tn = 1024
    return pl.pallas_call(
        _mod_kernel,
        grid=(depth, n // tn),
        in_specs=[pl.BlockSpec((8, d), lambda l, j: (0, 0)),
                  pl.BlockSpec((1, d, tn), lambda l, j: (l, 0, j)),
                  pl.BlockSpec((1, 1, tn), lambda l, j: (l, 0, j))],
        out_specs=pl.BlockSpec((1, 8, tn), lambda l, j: (l, 0, j)),
        out_shape=jax.ShapeDtypeStruct((depth, 8, n), F32),
        compiler_params=_cparams("arbitrary", "arbitrary"),
        name="adaln_mod",
    )(c8, w_ada, b_ada3)


def _norm_kernel(x_ref, g_ref, sc_ref, sh_ref, h_ref, *, transpose):
    x = x_ref[...]
    ms = jnp.mean(x * x, axis=-1, keepdims=True)
    y = x * lax.rsqrt(ms + EPS) * g_ref[...]
    h = y * (1.0 + sc_ref[...]) + sh_ref[...]
    h_ref[...] = (h.T if transpose else h).astype(BF)


def _norm_call(x, g, sc, sh, *, transpose):
    s, d = x.shape
    tm = min(512, s)
    row = pl.BlockSpec((1, d), lambda i: (0, 0))
    if transpose:
        out_shape = jax.ShapeDtypeStruct((d, s), BF)
        out_spec = pl.BlockSpec((d, tm), lambda i: (0, i))
    else:
        out_shape = jax.ShapeDtypeStruct((s, d), BF)
        out_spec = pl.BlockSpec((tm, d), lambda i: (i, 0))
    return pl.pallas_call(
        functools.partial(_norm_kernel, transpose=transpose),
        grid=(s // tm,),
        in_specs=[pl.BlockSpec((tm, d), lambda i: (i, 0)), row, row, row],
        out_specs=out_spec,
        out_shape=out_shape,
        compiler_params=_cparams("arbitrary"),
        name="adaln_rmsnorm",
    )(x, g, sc, sh)


def _proj_tile(w_ref, xt_ref, rs_ref, o_ref, group, layout, tk):
    y = jnp.dot(w_ref[...], xt_ref[...], preferred_element_type=F32)
    tn, tm = y.shape
    if group:
        y3 = y.reshape(tn // group, group, tm)
        ms = jnp.mean(y3 * y3, axis=1, keepdims=True)
        y = (y3 * lax.rsqrt(ms + EPS)).reshape(tn, tm)
    y = y * rs_ref[...]
    if layout == "t":
        o_ref[...] = y.astype(o_ref.dtype)
    elif layout == "n":
        o_ref[...] = y.T.astype(o_ref.dtype)
    elif layout == "t4":
        for a in range(tn // HEAD_W):
            for b in range(tm // tk):
                o_ref[a, b] = y[a * HEAD_W:(a + 1) * HEAD_W, b * tk:(b + 1) * tk].astype(o_ref.dtype)
    elif layout == "qpad":
        first = lax.broadcasted_iota(jnp.int32, (HEAD_W, tm), 0) < DA_QK
        for a in range(tn // HEAD_W):
            ya = y[a * HEAD_W:(a + 1) * HEAD_W]
            o_ref[2 * a] = jnp.where(first, ya, 0.0).astype(o_ref.dtype)
            o_ref[2 * a + 1] = jnp.where(first, 0.0, ya).astype(o_ref.dtype)


def _proj_kernel(w_ref, xt_ref, rs_ref, *o_refs, specs, tk):
    seg = pl.program_id(0)
    for k, (group, layout, _) in enumerate(specs):
        @pl.when(seg == k)
        def _(k=k, group=group, layout=layout):
            _proj_tile(w_ref, xt_ref, rs_ref, o_refs[k], group, layout, tk)


def _proj_call(wt, layer, seg0, ht, rowscale, specs, *, name):
    k, s = ht.shape
    nseg = len(specs)
    tn = SEG
    tm = min(1024, s)
    nj = s // tm
    tk = min(ATT_TK, s)
    out_shapes, out_specs = [], []
    for seg, (_, layout, out_dtype) in enumerate(specs):
        def tile(i, j, seg=seg):
            return jnp.where(i < seg, 0, jnp.where(i > seg, nj - 1, j))

        if layout == "t":
            out_shapes.append(jax.ShapeDtypeStruct((tn, s), out_dtype))
            out_specs.append(pl.BlockSpec((tn, tm), lambda i, j, t=tile: (0, t(i, j))))
        elif layout == "n":
            out_shapes.append(jax.ShapeDtypeStruct((s, tn), out_dtype))
            out_specs.append(pl.BlockSpec((tm, tn), lambda i, j, t=tile: (t(i, j), 0)))
        elif layout == "t4":
            out_shapes.append(jax.ShapeDtypeStruct((tn // HEAD_W, s // tk, HEAD_W, tk), out_dtype))
            out_specs.append(pl.BlockSpec((tn // HEAD_W, tm // tk, HEAD_W, tk),
                                          lambda i, j, t=tile: (0, t(i, j), 0, 0)))
        else:
            out_shapes.append(jax.ShapeDtypeStruct((2 * tn // HEAD_W, HEAD_W, s), out_dtype))
            out_specs.append(pl.BlockSpec((2 * tn // HEAD_W, HEAD_W, tm),
                                          lambda i, j, t=tile: (0, 0, t(i, j))))
    return pl.pallas_call(
        functools.partial(_proj_kernel, specs=tuple(specs), tk=tk),
        grid=(nseg, nj),
        in_specs=[pl.BlockSpec((None, tn, k), lambda i, j: (layer, seg0 + i, 0)),
                  pl.BlockSpec((k, tm), lambda i, j: (0, j)),
                  pl.BlockSpec((tn, 1), lambda i, j: (i, 0))],
        out_specs=out_specs,
        out_shape=out_shapes,
        compiler_params=_cparams("arbitrary", "arbitrary"),
        name=name,
    )(wt, ht, rowscale)


def _resid_proj_kernel(a_ref, w_ref, x_ref, g_ref, o_ref):
    y = jnp.dot(a_ref[...], w_ref[...], preferred_element_type=F32)
    o_ref[...] = x_ref[...] + g_ref[...] * y


def _resid_proj_call(a, w, layer, x, g):
    s, k = a.shape
    n = w.shape[2]
    tm = min(1024, s)
    tn = min(1024, n)
    return pl.pallas_call(
        _resid_proj_kernel,
        grid=(s // tm, n // tn),
        in_specs=[pl.BlockSpec((tm, k), lambda i, j: (i, 0)),
                  pl.BlockSpec((None, k, tn), lambda i, j: (layer, 0, j)),
                  pl.BlockSpec((tm, tn), lambda i, j: (i, j)),
                  pl.BlockSpec((1, tn), lambda i, j: (0, j))],
        out_specs=pl.BlockSpec((tm, tn), lambda i, j: (i, j)),
        out_shape=jax.ShapeDtypeStruct((s, n), F32),
        compiler_params=_cparams("arbitrary", "arbitrary"),
        name="out_proj_residual",
    )(a, w, x, g)


def _rms_rows(x, g):
    return x * lax.rsqrt(jnp.mean(x * x, axis=0, keepdims=True) + EPS) * g


def _rope_rows(x, cos_t, sin_t):
    half = MLA_ROPE // 2
    x1, x2 = x[:half], x[half:]
    return x1 * cos_t - x2 * sin_t, x1 * sin_t + x2 * cos_t


def _mla_up_kernel(cb_ref, gq_ref, gkv_ref, wq_ref, wk_ref, wv_ref, gqn_ref, gqp_ref, gkn_ref,
                   gkp_ref, cost_ref, sint_ref, qt_ref, k_ref, vt_ref, *, tk):
    tm = cb_ref.shape[1]
    hw = MLA_NOPE + MLA_ROPE
    cos_t = cost_ref[...]
    sin_t = sint_ref[...]
    cqn = _rms_rows(cb_ref[0:MLA_Q_LORA], gq_ref[...]).astype(BF)
    qt = jnp.dot(wq_ref[...], cqn, preferred_element_type=F32)
    qzero = jnp.zeros((MLA_QK_PAD - hw, tm), BF)
    for h in range(MLA_HEADS):
        nope = _rms_rows(qt[h * hw:h * hw + MLA_NOPE], gqn_ref[...])
        pe1, pe2 = _rope_rows(_rms_rows(qt[h * hw + MLA_NOPE:(h + 1) * hw], gqp_ref[...]), cos_t, sin_t)
        qt_ref[h, 0:MLA_NOPE] = nope.astype(BF)
        qt_ref[h, MLA_NOPE:MLA_NOPE + MLA_ROPE // 2] = pe1.astype(BF)
        qt_ref[h, MLA_NOPE + MLA_ROPE // 2:hw] = pe2.astype(BF)
        qt_ref[h, hw:MLA_QK_PAD] = qzero
    ckvn = _rms_rows(cb_ref[MLA_Q_LORA:LAT_KPE], gkv_ref[...]).astype(BF)
    kt = jnp.dot(wk_ref[...], ckvn, preferred_element_type=F32)
    vt = jnp.dot(wv_ref[...], ckvn, preferred_element_type=F32)
    kpe1, kpe2 = _rope_rows(_rms_rows(cb_ref[LAT_KPE:LAT_KPE + MLA_ROPE], gkp_ref[...]), cos_t, sin_t)
    kzero = jnp.zeros((MLA_QK_PAD - hw, tm), F32)
    for h in range(MLA_HEADS):
        kn = _rms_rows(kt[h * MLA_NOPE:(h + 1) * MLA_NOPE], gkn_ref[...])
        k_ref[h] = jnp.concatenate([kn, kpe1, kpe2, kzero], axis=0).T.astype(BF)
        for b in range(tm // tk):
            vt_ref[h, b] = vt[h * MLA_V:(h + 1) * MLA_V, b * tk:(b + 1) * tk].astype(BF)


def _mla_up_call(cbt, layer, gq, gkv, wq, wk, wv, gqn, gqp, gkn, gkp, cos_t, sin_t):
    s = cbt.shape[1]
    tm = min(512, s)
    tk = min(ATT_TK, s)
    hw = MLA_NOPE + MLA_ROPE
    half = MLA_ROPE // 2

    def const(shape):
        return pl.BlockSpec(shape, lambda i: tuple(0 for _ in shape))

    def layer_w(shape):
        return pl.BlockSpec((None,) + shape, lambda i: (layer, 0, 0))

    return pl.pallas_call(
        functools.partial(_mla_up_kernel, tk=tk),
        grid=(s // tm,),
        in_specs=[pl.BlockSpec((cbt.shape[0], tm), lambda i: (0, i)),
                  const((MLA_Q_LORA, 1)), const((MLA_KV_LORA, 1)),
                  layer_w((MLA_HEADS * hw, MLA_Q_LORA)),
                  layer_w((MLA_HEADS * MLA_NOPE, MLA_KV_LORA)),
                  layer_w((MLA_HEADS * MLA_V, MLA_KV_LORA)),
                  const((MLA_NOPE, 1)), const((MLA_ROPE, 1)), const((MLA_NOPE, 1)),
                  const((MLA_ROPE, 1)),
                  pl.BlockSpec((half, tm), lambda i: (0, i)),
                  pl.BlockSpec((half, tm), lambda i: (0, i))],
        out_specs=[pl.BlockSpec((MLA_HEADS, MLA_QK_PAD, tm), lambda i: (0, 0, i)),
                   pl.BlockSpec((MLA_HEADS, tm, MLA_QK_PAD), lambda i: (0, i, 0)),
                   pl.BlockSpec((MLA_HEADS, tm // tk, MLA_V, tk), lambda i: (0, i, 0, 0))],
        out_shape=[jax.ShapeDtypeStruct((MLA_HEADS, MLA_QK_PAD, s), BF),
                   jax.ShapeDtypeStruct((MLA_HEADS, s, MLA_QK_PAD), BF),
                   jax.ShapeDtypeStruct((MLA_HEADS, s // tk, MLA_V, tk), BF)],
        compiler_params=_cparams("arbitrary"),
        name="mla_up",
    )(cbt, gq, gkv, wq, wk, wv, gqn, gqp, gkn, gkp, cos_t, sin_t)


def _logits_phase(kb, nblk, qts, k_rows, tk):
    rows = pl.ds(pl.multiple_of(kb * tk, tk), nblk * tk)
    return [jnp.dot(k_rows(g, rows), qts[g], preferred_element_type=F32) for g in range(len(qts))]


def _pv_phase(kb, nblk, ps, vt_ref, tk):
    pvs = []
    for g, p in enumerate(ps):
        pv = jnp.dot(vt_ref[g, kb], p[0:tk], preferred_element_type=F32)
        for j in range(1, nblk):
            pv += jnp.dot(vt_ref[g, kb + j], p[j * tk:(j + 1) * tk], preferred_element_type=F32)
        pvs.append(pv)
    return pvs


def _softmax_block(kb, nblk, biases, qts, k_rows, vt_ref, m_ref, l_ref, acc_ref, tk):
    hg = len(qts)
    ss = _logits_phase(kb, nblk, qts, k_rows, tk)
    if biases is not None:
        ss = [s + b for s, b in zip(ss, biases)]
    alphas, ps = [], []
    for g in range(hg):
        m_old = m_ref[g]
        m_new = jnp.maximum(m_old, jnp.max(ss[g], axis=0, keepdims=True))
        alpha = jnp.exp2(m_old - m_new)
        p = jnp.exp2(ss[g] - m_new)
        l_ref[g] = alpha * l_ref[g] + jnp.sum(p, axis=0, keepdims=True)
        m_ref[g] = m_new
        alphas.append(alpha)
        ps.append(p.astype(BF))
    pvs = _pv_phase(kb, nblk, ps, vt_ref, tk)
    for g in range(hg):
        acc_ref[g] = alphas[g] * acc_ref[g] + pvs[g]


def _fixed_ref_block(kb, nblk, qts, k_rows, vt_ref, m_ref, l_ref, acc_ref, tk):
    hg = len(qts)
    ss = _logits_phase(kb, nblk, qts, k_rows, tk)
    ps = []
    for g in range(hg):
        p = jnp.exp2(ss[g] - m_ref[g])
        l_ref[g] += jnp.sum(p, axis=0, keepdims=True)
        ps.append(p.astype(BF))
    pvs = _pv_phase(kb, nblk, ps, vt_ref, tk)
    for g in range(hg):
        acc_ref[g] += pvs[g]


def _far_sweep(n_far, far_blocks, step):
    n_big = n_far // far_blocks

    def big(i, carry):
        step(i * far_blocks, far_blocks)
        return carry

    lax.fori_loop(0, n_big, big, 0)
    done = n_big * far_blocks
    size = far_blocks // 2
    while size >= 1:
        take = ((n_far - done) // size) * size

        @pl.when(take > 0)
        def _(done=done, size=size):
            step(done, size)

        done = done + take
        size //= 2


def _overflowed(l_ref, acc_ref):
    bad = jnp.maximum(jnp.max(jnp.where(jnp.isfinite(l_ref[...]), 0.0, 1.0)),
                      jnp.max(jnp.where(jnp.isfinite(acc_ref[...]), 0.0, 1.0)))
    return bad > 0.0


def _attention_sweep(n_far, far_blocks, near_step, far_std, far_fixed, m_ref, l_ref, acc_ref):
    _init_softmax_state(m_ref, l_ref, acc_ref)
    near_step()
    _far_sweep(n_far, far_blocks, far_fixed)

    @pl.when(_overflowed(l_ref, acc_ref))
    def _():
        _init_softmax_state(m_ref, l_ref, acc_ref)
        near_step()

        def one(kb, carry):
            far_std(kb)
            return carry

        lax.fori_loop(0, n_far, one, 0)


def _resident_spec(shape, index_map):
    return pl.BlockSpec(shape, index_map, pipeline_mode=pl.Buffered(1))


def _init_softmax_state(m_ref, l_ref, acc_ref):
    m_ref[...] = jnp.full(m_ref.shape, NEG_INF, F32)
    l_ref[...] = jnp.zeros(l_ref.shape, F32)
    acc_ref[...] = jnp.zeros(acc_ref.shape, F32)


def _diff_attn_kernel(q_ref, k_ref, vt_ref, bn_ref, lam_ref, g_ref, o_ref,
                      m_ref, l_ref, acc_ref, *, tq, tk, hg, lam_init):
    qb = pl.program_id(1)
    qts = [jnp.concatenate([q_ref[2 * g], q_ref[2 * g + 1]], axis=1) for g in range(hg)]

    def k_rows(g, rows):
        return k_ref[rows, g * HEAD_W:(g + 1) * HEAD_W]

    state = (vt_ref, m_ref, l_ref, acc_ref, tk)
    first = (qb == 0).astype(jnp.int32)

    def near_step():
        _softmax_block(qb - 1 + first, 2, [bn_ref[g, first] for g in range(hg)], qts, k_rows, *state)

    _attention_sweep(
        jnp.maximum(qb - 1, 0), DIFF_FAR_BLOCKS, near_step,
        lambda kb: _softmax_block(kb, 1, None, qts, k_rows, *state),
        lambda kb, nblk: _fixed_ref_block(kb, nblk, qts, k_rows, *state),
        m_ref, l_ref, acc_ref)

    lp = lam_ref[...]
    lam = (jnp.exp(jnp.sum(lp[0:1] * lp[1:2], axis=1, keepdims=True))
           - jnp.exp(jnp.sum(lp[2:3] * lp[3:4], axis=1, keepdims=True)) + lam_init)
    for g in range(hg):
        o = acc_ref[g] / l_ref[g]
        d = o[:, :tq] - lam * o[:, tq:]
        ms = jnp.mean(d * d, axis=0, keepdims=True)
        d = d * lax.rsqrt(ms + EPS) * g_ref[...] * (1.0 - lam_init)
        o_ref[:, g * HEAD_W:(g + 1) * HEAD_W] = d.T.astype(o_ref.dtype)


def _diff_attn_call(qpad, k, vt4, bias_near, lam_p, sub_g, *, lam_init):
    s = k.shape[0]
    tq = tk = min(ATT_TQ, s)
    nkb = s // tk
    n = 2 * tq
    hg = DIFF_HEAD_GROUP
    return pl.pallas_call(
        functools.partial(_diff_attn_kernel, tq=tq, tk=tk, hg=hg, lam_init=lam_init),
        grid=(DA_HEADS // hg, s // tq),
        in_specs=[pl.BlockSpec((2 * hg, HEAD_W, tq), lambda h, i: (h, 0, i)),
                  _resident_spec((s, hg * HEAD_W), lambda h, i: (0, h)),
                  _resident_spec((hg, nkb, HEAD_W, tk), lambda h, i: (h, 0, 0, 0)),
                  _resident_spec((hg, 2, 2 * tk, n), lambda h, i: (h, 0, 0, 0)),
                  pl.BlockSpec((4, DA_QK), lambda h, i: (0, 0)),
                  pl.BlockSpec((HEAD_W, 1), lambda h, i: (0, 0))],
        out_specs=pl.BlockSpec((tq, hg * HEAD_W), lambda h, i: (i, h)),
        out_shape=jax.ShapeDtypeStruct((s, DA_HEADS * HEAD_W), BF),
        scratch_shapes=[pltpu.VMEM((hg, 1, n), F32), pltpu.VMEM((hg, 1, n), F32),
                        pltpu.VMEM((hg, HEAD_W, n), F32)],
        compiler_params=_cparams("arbitrary", "arbitrary"),
        name="diff_attention",
    )(qpad, k, vt4, bias_near, lam_p, sub_g)


def _mla_attn_kernel(q_ref, k_ref, vt_ref, mask_ref, o_ref, m_ref, l_ref, acc_ref, *, tk, hg):
    qb = pl.program_id(1)
    qts = [q_ref[g] for g in range(hg)]

    def k_rows(g, rows):
        return k_ref[g, rows, :]

    state = (vt_ref, m_ref, l_ref, acc_ref, tk)

    def near_step():
        _softmax_block(qb, 1, [mask_ref[...]] * hg, qts, k_rows, *state)

    _attention_sweep(
        qb, MLA_FAR_BLOCKS, near_step,
        lambda kb: _softmax_block(kb, 1, None, qts, k_rows, *state),
        lambda kb, nblk: _fixed_ref_block(kb, nblk, qts, k_rows, *state),
        m_ref, l_ref, acc_ref)
    for g in range(hg):
        o = acc_ref[g] / l_ref[g]
        o_ref[:, g * HEAD_W:(g + 1) * HEAD_W] = o.T.astype(o_ref.dtype)


def _mla_attn_call(qt, k, vt4, mask_diag):
    s = k.shape[1]
    tq = tk = min(ATT_TQ, s)
    nkb = s // tk
    hg = MLA_HEAD_GROUP
    return pl.pallas_call(
        functools.partial(_mla_attn_kernel, tk=tk, hg=hg),
        grid=(MLA_HEADS // hg, s // tq),
        in_specs=[pl.BlockSpec((hg, MLA_QK_PAD, tq), lambda h, i: (h, 0, i)),
                  _resident_spec((hg, s, MLA_QK_PAD), lambda h, i: (h, 0, 0)),
                  _resident_spec((hg, nkb, HEAD_W, tk), lambda h, i: (h, 0, 0, 0)),
                  pl.BlockSpec((tk, tq), lambda h, i: (0, 0))],
        out_specs=pl.BlockSpec((tq, hg * HEAD_W), lambda h, i: (i, h)),
        out_shape=jax.ShapeDtypeStruct((s, MLA_HEADS * HEAD_W), BF),
        scratch_shapes=[pltpu.VMEM((hg, 1, tq), F32), pltpu.VMEM((hg, 1, tq), F32),
                        pltpu.VMEM((hg, HEAD_W, tq), F32)],
        compiler_params=_cparams("arbitrary", "arbitrary"),
        name="mla_attention",
    )(qt, k, vt4, mask_diag)


def _stick_block(kb, mask, qts, k_ref, vt_ref, tri_neg, carry_ref, acc_ref, tk):
    hg = len(qts)
    rows = pl.ds(pl.multiple_of(kb * tk, tk), tk)
    zs = [jnp.dot(k_ref[rows, g * HEAD_W:(g + 1) * HEAD_W], qts[g], preferred_element_type=F32)
          for g in range(hg)]
    logsigs, his, los = [], [], []
    for g in range(hg):
        z = zs[g]
        sp = jnp.maximum(z, 0.0) + jnp.log2(1.0 + jnp.exp2(-jnp.abs(z)))
        logsigs.append(z - sp + carry_ref[g])
        if mask is not None:
            sp = jnp.where(mask, sp, 0.0)
        hi = sp.astype(BF)
        his.append(hi)
        los.append((sp - hi.astype(F32)).astype(BF))
        carry_ref[g] -= jnp.sum(sp, axis=0, keepdims=True)
    betweens = [jnp.dot(tri_neg, his[g], preferred_element_type=F32)
                + jnp.dot(tri_neg, los[g], preferred_element_type=F32) for g in range(hg)]
    ws = []
    for g in range(hg):
        w = jnp.exp2(logsigs[g] + betweens[g])
        if mask is not None:
            w = jnp.where(mask, w, 0.0)
        ws.append(w.astype(BF))
    pvs = [jnp.dot(vt_ref[g, kb], ws[g], preferred_element_type=F32) for g in range(hg)]
    for g in range(hg):
        acc_ref[g] += pvs[g]


def _stick_kernel(q_ref, k_ref, vt_ref, tri_ref, o_ref, carry_ref, acc_ref, *, tq, tk, hg):
    qb = pl.program_id(1)
    qts = [q_ref[g * HEAD_W:(g + 1) * HEAD_W, :] for g in range(hg)]
    tri = tri_ref[...]
    carry_ref[...] = jnp.zeros(carry_ref.shape, F32)
    acc_ref[...] = jnp.zeros(acc_ref.shape, F32)
    kpos = lax.broadcasted_iota(jnp.int32, (tk, tq), 0)
    qpos = lax.broadcasted_iota(jnp.int32, (tk, tq), 1)
    _stick_block(qb, kpos < qpos, qts, k_ref, vt_ref, tri, carry_ref, acc_ref, tk)

    def more(state):
        i, cmax = state
        return jnp.logical_and(i < qb, cmax > SB_LOG2W_UNDERFLOW)

    def earlier(state):
        i, _ = state
        _stick_block(qb - 1 - i, None, qts, k_ref, vt_ref, tri, carry_ref, acc_ref, tk)
        return i + 1, jnp.max(carry_ref[...])

    lax.while_loop(more, earlier, (jnp.int32(0), jnp.max(carry_ref[...])))
    for g in range(hg):
        o_ref[:, g * HEAD_W:(g + 1) * HEAD_W] = acc_ref[g].T.astype(o_ref.dtype)


def _stick_call(qt, k, vt4, tri):
    s = k.shape[0]
    tq = tk = min(ATT_TQ, s)
    nkb = s // tk
    hg = SB_HEAD_GROUP
    return pl.pallas_call(
        functools.partial(_stick_kernel, tq=tq, tk=tk, hg=hg),
        grid=(SB_HEADS // hg, s // tq),
        in_specs=[pl.BlockSpec((hg * HEAD_W, tq), lambda h, i: (h, i)),
                  _resident_spec((s, hg * HEAD_W), lambda h, i: (0, h)),
                  _resident_spec((hg, nkb, HEAD_W, tk), lambda h, i: (h, 0, 0, 0)),
                  pl.BlockSpec((tk, tk), lambda h, i: (0, 0))],
        out_specs=pl.BlockSpec((tq, hg * HEAD_W), lambda h, i: (i, h)),
        out_shape=jax.ShapeDtypeStruct((s, SB_HEADS * HEAD_W), BF),
        scratch_shapes=[pltpu.VMEM((hg, 1, tq), F32), pltpu.VMEM((hg, HEAD_W, tq), F32)],
        compiler_params=_cparams("arbitrary", "arbitrary"),
        name="stick_breaking",
    )(qt, k, vt4, tri)


def _merge_kernel(ht_ref, ya_ref, yb_ref, yc_ref, wga_ref, wgb_ref, wgc_ref, wb_ref, o_ref):
    ht = ht_ref[...]
    acc = None
    for n, (y_ref, wg_ref) in enumerate(((ya_ref, wga_ref), (yb_ref, wgb_ref), (yc_ref, wgc_ref))):
        gate = jax.nn.sigmoid(jnp.dot(wg_ref[...], ht, preferred_element_type=F32)).T
        up = jnp.dot(y_ref[...], wb_ref[n], preferred_element_type=F32)
        acc = gate * up if acc is None else acc + gate * up
    o_ref[...] = acc.astype(o_ref.dtype)


def _merge_call(ht, ya, yb, yc, wt_tail, wb, layer):
    d, s = ht.shape
    tm = min(512, s)
    tn = 512
    ysp = pl.BlockSpec((tm, BRANCH_WIDTH), lambda i, j: (i, 0))

    def gate_spec(n):
        blk0 = (SEG_GATE * SEG + n * d) // tn
        return pl.BlockSpec((None, tn, d), lambda i, j: (layer, blk0 + j, 0))

    return pl.pallas_call(
        _merge_kernel,
        grid=(s // tm, d // tn),
        in_specs=[pl.BlockSpec((d, tm), lambda i, j: (0, i)), ysp, ysp, ysp,
                  gate_spec(0), gate_spec(1), gate_spec(2),
                  pl.BlockSpec((None, N_BRANCHES, BRANCH_WIDTH, tn), lambda i, j: (layer, 0, 0, j))],
        out_specs=pl.BlockSpec((tm, tn), lambda i, j: (i, j)),
        out_shape=jax.ShapeDtypeStruct((s, d), BF),
        compiler_params=_cparams("arbitrary", "arbitrary"),
        name="gated_merge",
    )(ht, ya, yb, yc, wt_tail, wt_tail, wt_tail, wb)


def _mlp_kernel(h_ref, w1_ref, w2_ref, x_ref, g_ref, o_ref, acc_ref):
    f = pl.program_id(1)

    @pl.when(f == 0)
    def _():
        acc_ref[...] = jnp.zeros(acc_ref.shape, F32)

    u = jnp.dot(h_ref[...], w1_ref[...], preferred_element_type=F32)
    u = jnp.square(jnp.maximum(u, 0.0)).astype(BF)
    acc_ref[...] += jnp.dot(u, w2_ref[...], preferred_element_type=F32)

    @pl.when(f == pl.num_programs(1) - 1)
    def _():
        o_ref[...] = x_ref[...] + g_ref[...] * acc_ref[...]


def _mlp_call(h, w1, w2, layer, x, g):
    s, d = h.shape
    ff = w1.shape[2]
    tm = min(512, s)
    tf = 1024
    return pl.pallas_call(
        _mlp_kernel,
        grid=(s // tm, ff // tf),
        in_specs=[pl.BlockSpec((tm, d), lambda i, f: (i, 0)),
                  pl.BlockSpec((None, d, tf), lambda i, f: (layer, 0, f)),
                  pl.BlockSpec((None, tf, d), lambda i, f: (layer, f, 0)),
                  pl.BlockSpec((tm, d), lambda i, f: (i, 0)),
                  pl.BlockSpec((1, d), lambda i, f: (0, 0))],
        out_specs=pl.BlockSpec((tm, d), lambda i, f: (i, 0)),
        out_shape=jax.ShapeDtypeStruct((s, d), F32),
        scratch_shapes=[pltpu.VMEM((tm, d), F32)],
        compiler_params=_cparams("arbitrary", "arbitrary"),
        name="sqrelu_mlp",
    )(h, w1, w2, x, g)


def _t5_bucket(rel):
    nb = T5_BUCKETS // 2
    max_exact = nb // 2
    n = jnp.abs(rel)
    large = max_exact + (jnp.log(jnp.maximum(n, 1).astype(F32) / max_exact)
                         / math.log(T5_MAX_DIST / max_exact) * (nb - max_exact)).astype(jnp.int32)
    large = jnp.minimum(large, nb - 1)
    return jnp.where(rel > 0, nb, 0) + jnp.where(n < max_exact, n, large)


def _bias_tiles(t5_bias, tq, tk):
    table = t5_bias.astype(F32)

    def lookup(bucket):
        hit = bucket[..., None, None] == jnp.arange(T5_BUCKETS)[:, None]
        return jnp.sum(jnp.where(hit, table, 0.0), axis=-2)

    kl = jnp.arange(tk)[:, None]
    ql = jnp.arange(tq)[None, :]
    far = lookup(_t5_bucket(jnp.array(-(tk + tq), jnp.int32)))
    diag = jnp.transpose(lookup(_t5_bucket(kl - ql)) - far, (2, 0, 1))
    prev = jnp.transpose(lookup(_t5_bucket(kl - tk - ql)) - far, (2, 0, 1))
    allowed = (kl // CHUNK) <= (ql // CHUNK)
    diag = jnp.where(allowed[None], diag * LOG2E, NEG_INF)
    near = jnp.stack([jnp.concatenate([prev * LOG2E, diag], axis=1),
                      jnp.concatenate([diag, jnp.full_like(diag, NEG_INF)], axis=1)], axis=1)
    return jnp.concatenate([near, near], axis=3)


def _rope_tables(s):
    half = MLA_ROPE // 2
    inv = ROPE_BASE ** (-jnp.arange(half, dtype=F32) / half)
    ang = jnp.arange(s).astype(F32)[:, None] * inv[None, :]
    return jnp.cos(ang).T, jnp.sin(ang).T


def kernel(x, c, w_ada, b_ada, norm_mix_g, norm_mlp_g, w_in, diff_qk_g, diff_lambda, diff_subln_g,
           t5_bias, mla_q_norm_g, mla_kv_norm_g, w_q_up, w_kv_up, mla_qk_g, w_branch, w_out,
           w_mlp_in, w_mlp_out):
    b, s, d = x.shape
    assert b == 1 and d == D_MODEL
    depth = w_ada.shape[0]
    tq = tk = min(ATT_TQ, s)
    hw = MLA_NOPE + MLA_ROPE

    mod = _mod_call(jnp.broadcast_to(c, (8, d)).astype(BF), w_ada, b_ada[:, None, :])[:, 0, :]
    mod = mod.reshape(depth, N_MOD, 1, d)

    wt_head = jnp.swapaxes(w_in[:, :, :HEAD_ROWS], 1, 2).astype(BF)
    wt_tail = jnp.swapaxes(w_in[:, :, TAIL_ROW0:], 1, 2).astype(BF)
    w_br = w_branch.astype(BF)
    w_o = w_out.astype(BF)
    w_1 = w_mlp_in.astype(BF)
    w_2 = w_mlp_out.astype(BF)
    wt_qu = jnp.swapaxes(w_q_up, 1, 2).astype(BF)
    wkv = w_kv_up.reshape(depth, MLA_KV_LORA, MLA_HEADS, MLA_NOPE + MLA_V)
    wt_kn = jnp.swapaxes(wkv[..., :MLA_NOPE].reshape(depth, MLA_KV_LORA, -1), 1, 2).astype(BF)
    wt_v = jnp.swapaxes(wkv[..., MLA_NOPE:].reshape(depth, MLA_KV_LORA, -1), 1, 2).astype(BF)

    bias_near = _bias_tiles(t5_bias, tq, tk)
    kl = jnp.arange(tk)[:, None]
    ql = jnp.arange(tq)[None, :]
    mask_diag = jnp.where((kl // CHUNK) <= (ql // CHUNK), 0.0, NEG_INF).astype(F32)
    tri = -(jnp.arange(tk)[None, :] > jnp.arange(tk)[:, None]).astype(BF)
    cos_t, sin_t = _rope_tables(s)

    ones_col = jnp.ones((SEG, 1), F32)
    sb_scale = jnp.full((SB_HEADS * SB_DIM, 1), SB_DIM ** -0.5 * LOG2E, F32)
    da_scale = DA_QK ** -0.5 * LOG2E
    mla_scale = hw ** -0.5 * LOG2E

    x2 = x[0]
    for l in range(depth):
        sh1, sc1, g1, sh2, sc2, g2 = (mod[l, i] for i in range(N_MOD))
        lam_init = 0.8 - 0.6 * math.exp(-0.3 * l)

        ht = _norm_call(x2, norm_mix_g[l][None], sc1, sh1, transpose=True)

        gq_rows = jnp.tile(diff_qk_g[l, 0], 2 * DA_HEADS)[:, None] * da_scale
        gk_rows = jnp.tile(diff_qk_g[l, 1], 2 * DA_HEADS)[:, None]
        qa, ka, va, cbt = _proj_call(
            wt_head, l, SEG_DQ, ht, jnp.concatenate([gq_rows, gk_rows, ones_col, ones_col]),
            [(DA_QK, "qpad", BF), (DA_QK, "n", BF), (0, "t4", BF), (0, "t", F32)], name="proj_head")
        qc, kc, vc = _proj_call(
            wt_tail, l, SEG_SQ, ht, jnp.concatenate([sb_scale, ones_col, ones_col]),
            [(0, "t", BF), (0, "n", BF), (0, "t4", BF)], name="proj_tail")
        gqk = mla_qk_g[l]
        qb_t, kb, vb_t = _mla_up_call(
            cbt, l, mla_q_norm_g[l][:, None], mla_kv_norm_g[l][:, None], wt_qu, wt_kn, wt_v,
            gqk[0, :MLA_NOPE, None] * mla_scale, gqk[0, MLA_NOPE:, None] * mla_scale,
            gqk[1, :MLA_NOPE, None], gqk[1, MLA_NOPE:, None], cos_t, sin_t)

        ya = _diff_attn_call(qa, ka, va, bias_near, diff_lambda[l], diff_subln_g[l][:, None],
                             lam_init=lam_init)
        yb = _mla_attn_call(qb_t, kb, vb_t, mask_diag)
        yc = _stick_call(qc, kc, vc, tri)

        merged = _merge_call(ht, ya, yb, yc, wt_tail, w_br, l)
        x2 = _resid_proj_call(merged, w_o, l, x2, g1)

        h2 = _norm_call(x2, norm_mlp_g[l][None], sc2, sh2, transpose=False)
        x2 = _mlp_call(h2, w_1, w_2, l, x2, g2)
    return x2[None]
```

```python
import functools
import math

import jax
import jax.numpy as jnp
from jax import lax
from jax.experimental import pallas as pl
from jax.experimental.pallas import tpu as pltpu

BF = jnp.bfloat16
F32 = jnp.float32

D_MODEL = 2048
CHUNK = 64
DA_HEADS = 8
DA_QK = 64
DA_V = 128
MLA_HEADS = 8
MLA_Q_LORA = 512
MLA_KV_LORA = 256
MLA_NOPE = 128
MLA_ROPE = 64
MLA_V = 128
MLA_QK_PAD = 256
ROPE_BASE = 10000.0
SB_HEADS = 8
SB_DIM = 128
BRANCH_WIDTH = 1024
N_BRANCHES = 3
T5_BUCKETS = 32
T5_MAX_DIST = 128
D_FF = 4 * D_MODEL
N_MOD = 6
EPS = 1e-6
NEG_INF = -1e30
LOG2E = 1.4426950408889634

HEAD_W = 128
ATT_TQ = 256
ATT_TK = 256
DIFF_FAR_BLOCKS = 4
MLA_FAR_BLOCKS = 4
DIFF_HEAD_GROUP = 4
MLA_HEAD_GROUP = 4
SB_HEAD_GROUP = 4
SB_LOG2W_UNDERFLOW = -152.0
V7X_VMEM_LIMIT = 56 * 1024 * 1024

SEG = 1024
HEAD_ROWS = 4 * SEG
TAIL_ROW0 = 3 * SEG + MLA_Q_LORA + MLA_KV_LORA + MLA_ROPE
SEG_DQ, SEG_DK, SEG_DV, SEG_LAT = 0, 1, 2, 3
SEG_SQ, SEG_SK, SEG_SV, SEG_GATE = 0, 1, 2, 3
LAT_KPE = MLA_Q_LORA + MLA_KV_LORA


def _cparams(*sem):
    return pltpu.CompilerParams(dimension_semantics=sem, vmem_limit_bytes=V7X_VMEM_LIMIT)


def _mod_kernel(c_ref, w_ref, b_ref, o_ref):
    w = w_ref[0].astype(BF)
    o_ref[0] = jnp.dot(c_ref[...], w, preferred_element_type=F32) + b_ref[0]


def _mod_call(c8, w_ada, b_ada3):
    depth, d, n = w_ada.shape
    tn = 1024
    return pl.pallas_call(
        _mod_kernel,
        grid=(depth, n // tn),
        in_specs=[pl.BlockSpec((8, d), lambda l, j: (0, 0)),
                  pl.BlockSpec((1, d, tn), lambda l, j: (l, 0, j)),
                  pl.BlockSpec((1, 1, tn), lambda l, j: (l, 0, j))],
        out_specs=pl.BlockSpec((1, 8, tn), lambda l, j: (l, 0, j)),
        out_shape=jax.ShapeDtypeStruct((depth, 8, n), F32),
        compiler_params=_cparams("arbitrary", "arbitrary"),
        name="adaln_mod",
    )(c8, w_ada, b_ada3)


def _norm_kernel(x_ref, g_ref, sc_ref, sh_ref, ht_ref):
    x = x_ref[...]
    ms = jnp.mean(x * x, axis=-1, keepdims=True)
    y = x * lax.rsqrt(ms + EPS) * g_ref[...]
    ht_ref[...] = (y * (1.0 + sc_ref[...]) + sh_ref[...]).T.astype(BF)


def _norm_call(x, g, sc, sh):
    s, d = x.shape
    tm = min(512, s)
    row = pl.BlockSpec((1, d), lambda i: (0, 0))
    return pl.pallas_call(
        _norm_kernel,
        grid=(s // tm,),
        in_specs=[pl.BlockSpec((tm, d), lambda i: (i, 0)), row, row, row],
        out_specs=pl.BlockSpec((d, tm), lambda i: (0, i)),
        out_shape=jax.ShapeDtypeStruct((d, s), BF),
        compiler_params=_cparams("arbitrary"),
        name="adaln_rmsnorm",
    )(x, g, sc, sh)


def _proj_tile(w_ref, xt_ref, rs_ref, o_ref, group, layout, tk):
    y = jnp.dot(w_ref[...], xt_ref[...], preferred_element_type=F32)
    tn, tm = y.shape
    if group:
        y3 = y.reshape(tn // group, group, tm)
        ms = jnp.mean(y3 * y3, axis=1, keepdims=True)
        y = (y3 * lax.rsqrt(ms + EPS)).reshape(tn, tm)
    y = y * rs_ref[...]
    if layout == "t":
        o_ref[...] = y.astype(o_ref.dtype)
    elif layout == "n":
        o_ref[...] = y.T.astype(o_ref.dtype)
    elif layout == "t4":
        for a in range(tn // HEAD_W):
            for b in range(tm // tk):
                o_ref[a, b] = y[a * HEAD_W:(a + 1) * HEAD_W, b * tk:(b + 1) * tk].astype(o_ref.dtype)
    elif layout == "qpad":
        first = lax.broadcasted_iota(jnp.int32, (HEAD_W, tm), 0) < DA_QK
        for a in range(tn // HEAD_W):
            ya = y[a * HEAD_W:(a + 1) * HEAD_W]
            o_ref[2 * a] = jnp.where(first, ya, 0.0).astype(o_ref.dtype)
            o_ref[2 * a + 1] = jnp.where(first, 0.0, ya).astype(o_ref.dtype)


def _proj_kernel(w_ref, xt_ref, rs_ref, *o_refs, specs, tk):
    seg = pl.program_id(0)
    for k, (group, layout, _) in enumerate(specs):
        @pl.when(seg == k)
        def _(k=k, group=group, layout=layout):
            _proj_tile(w_ref, xt_ref, rs_ref, o_refs[k], group, layout, tk)


def _proj_call(wt, layer, seg0, ht, rowscale, specs, *, name):
    k, s = ht.shape
    nseg = len(specs)
    tn = SEG
    tm = min(1024, s)
    nj = s // tm
    tk = min(ATT_TK, s)
    out_shapes, out_specs = [], []
    for seg, (_, layout, out_dtype) in enumerate(specs):
        def tile(i, j, seg=seg):
            return jnp.where(i < seg, 0, jnp.where(i > seg, nj - 1, j))

        if layout == "t":
            out_shapes.append(jax.ShapeDtypeStruct((tn, s), out_dtype))
            out_specs.append(pl.BlockSpec((tn, tm), lambda i, j, t=tile: (0, t(i, j))))
        elif layout == "n":
            out_shapes.append(jax.ShapeDtypeStruct((s, tn), out_dtype))
            out_specs.append(pl.BlockSpec((tm, tn), lambda i, j, t=tile: (t(i, j), 0)))
        elif layout == "t4":
            out_shapes.append(jax.ShapeDtypeStruct((tn // HEAD_W, s // tk, HEAD_W, tk), out_dtype))
            out_specs.append(pl.BlockSpec((tn // HEAD_W, tm // tk, HEAD_W, tk),
                                          lambda i, j, t=tile: (0, t(i, j), 0, 0)))
        else:
            out_shapes.append(jax.ShapeDtypeStruct((2 * tn // HEAD_W, HEAD_W, s), out_dtype))
            out_specs.append(pl.BlockSpec((2 * tn // HEAD_W, HEAD_W, tm),
                                          lambda i, j, t=tile: (0, 0, t(i, j))))
    return pl.pallas_call(
        functools.partial(_proj_kernel, specs=tuple(specs), tk=tk),
        grid=(nseg, nj),
        in_specs=[pl.BlockSpec((None, tn, k), lambda i, j: (layer, seg0 + i, 0)),
                  pl.BlockSpec((k, tm), lambda i, j: (0, j)),
                  pl.BlockSpec((tn, 1), lambda i, j: (i, 0))],
        out_specs=out_specs,
        out_shape=out_shapes,
        compiler_params=_cparams("arbitrary", "arbitrary"),
        name=name,
    )(wt, ht, rowscale)


def _resid_proj_kernel(a_ref, w_ref, x_ref, g_ref, o_ref):
    y = jnp.dot(a_ref[...], w_ref[...], preferred_element_type=F32)
    o_ref[...] = x_ref[...] + g_ref[...] * y


def _resid_proj_call(a, w, layer, x, g):
    s, k = a.shape
    n = w.shape[2]
    tm = min(1024, s)
    tn = min(1024, n)
    return pl.pallas_call(
        _resid_proj_kernel,
        grid=(s // tm, n // tn),
        in_specs=[pl.BlockSpec((tm, k), lambda i, j: (i, 0)),
                  pl.BlockSpec((None, k, tn), lambda i, j: (layer, 0, j)),
                  pl.BlockSpec((tm, tn), lambda i, j: (i, j)),
                  pl.BlockSpec((1, tn), lambda i, j: (0, j))],
        out_specs=pl.BlockSpec((tm, tn), lambda i, j: (i, j)),
        out_shape=jax.ShapeDtypeStruct((s, n), F32),
        compiler_params=_cparams("arbitrary", "arbitrary"),
        name="out_proj_residual",
    )(a, w, x, g)


def _rms_rows(x, g):
    return x * lax.rsqrt(jnp.mean(x * x, axis=0, keepdims=True) + EPS) * g


def _rope_rows(x, cos_t, sin_t):
    half = MLA_ROPE // 2
    x1, x2 = x[:half], x[half:]
    return x1 * cos_t - x2 * sin_t, x1 * sin_t + x2 * cos_t


def _mla_up_kernel(cb_ref, gq_ref, gkv_ref, wq_ref, wk_ref, wv_ref, gqn_ref, gqp_ref, gkn_ref,
                   gkp_ref, cost_ref, sint_ref, qt_ref, k_ref, vt_ref, *, tk):
    tm = cb_ref.shape[1]
    hw = MLA_NOPE + MLA_ROPE
    cos_t = cost_ref[...]
    sin_t = sint_ref[...]
    cqn = _rms_rows(cb_ref[0:MLA_Q_LORA], gq_ref[...]).astype(BF)
    qt = jnp.dot(wq_ref[...], cqn, preferred_element_type=F32)
    qzero = jnp.zeros((MLA_QK_PAD - hw, tm), BF)
    for h in range(MLA_HEADS):
        nope = _rms_rows(qt[h * hw:h * hw + MLA_NOPE], gqn_ref[...])
        pe1, pe2 = _rope_rows(_rms_rows(qt[h * hw + MLA_NOPE:(h + 1) * hw], gqp_ref[...]), cos_t, sin_t)
        qt_ref[h, 0:MLA_NOPE] = nope.astype(BF)
        qt_ref[h, MLA_NOPE:MLA_NOPE + MLA_ROPE // 2] = pe1.astype(BF)
        qt_ref[h, MLA_NOPE + MLA_ROPE // 2:hw] = pe2.astype(BF)
        qt_ref[h, hw:MLA_QK_PAD] = qzero
    ckvn = _rms_rows(cb_ref[MLA_Q_LORA:LAT_KPE], gkv_ref[...]).astype(BF)
    kt = jnp.dot(wk_ref[...], ckvn, preferred_element_type=F32)
    vt = jnp.dot(wv_ref[...], ckvn, preferred_element_type=F32)
    kpe1, kpe2 = _rope_rows(_rms_rows(cb_ref[LAT_KPE:LAT_KPE + MLA_ROPE], gkp_ref[...]), cos_t, sin_t)
    kzero = jnp.zeros((MLA_QK_PAD - hw, tm), F32)
    for h in range(MLA_HEADS):
        kn = _rms_rows(kt[h * MLA_NOPE:(h + 1) * MLA_NOPE], gkn_ref[...])
        k_ref[h] = jnp.concatenate([kn, kpe1, kpe2, kzero], axis=0).T.astype(BF)
        for b in range(tm // tk):
            vt_ref[h, b] = vt[h * MLA_V:(h + 1) * MLA_V, b * tk:(b + 1) * tk].astype(BF)


def _mla_up_call(cbt, layer, gq, gkv, wq, wk, wv, gqn, gqp, gkn, gkp, cos_t, sin_t):
    s = cbt.shape[1]
    tm = min(512, s)
    tk = min(ATT_TK, s)
    hw = MLA_NOPE + MLA_ROPE
    half = MLA_ROPE // 2

    def const(shape):
        return pl.BlockSpec(shape, lambda i: tuple(0 for _ in shape))

    def layer_w(shape):
        return pl.BlockSpec((None,) + shape, lambda i: (layer, 0, 0))

    return pl.pallas_call(
        functools.partial(_mla_up_kernel, tk=tk),
        grid=(s // tm,),
        in_specs=[pl.BlockSpec((cbt.shape[0], tm), lambda i: (0, i)),
                  const((MLA_Q_LORA, 1)), const((MLA_KV_LORA, 1)),
                  layer_w((MLA_HEADS * hw, MLA_Q_LORA)),
                  layer_w((MLA_HEADS * MLA_NOPE, MLA_KV_LORA)),
                  layer_w((MLA_HEADS * MLA_V, MLA_KV_LORA)),
                  const((MLA_NOPE, 1)), const((MLA_ROPE, 1)), const((MLA_NOPE, 1)),
                  const((MLA_ROPE, 1)),
                  pl.BlockSpec((half, tm), lambda i: (0, i)),
                  pl.BlockSpec((half, tm), lambda i: (0, i))],
        out_specs=[pl.BlockSpec((MLA_HEADS, MLA_QK_PAD, tm), lambda i: (0, 0, i)),
                   pl.BlockSpec((MLA_HEADS, tm, MLA_QK_PAD), lambda i: (0, i, 0)),
                   pl.BlockSpec((MLA_HEADS, tm // tk, MLA_V, tk), lambda i: (0, i, 0, 0))],
        out_shape=[jax.ShapeDtypeStruct((MLA_HEADS, MLA_QK_PAD, s), BF),
                   jax.ShapeDtypeStruct((MLA_HEADS, s, MLA_QK_PAD), BF),
                   jax.ShapeDtypeStruct((MLA_HEADS, s // tk, MLA_V, tk), BF)],
        compiler_params=_cparams("arbitrary"),
        name="mla_up",
    )(cbt, gq, gkv, wq, wk, wv, gqn, gqp, gkn, gkp, cos_t, sin_t)


def _logits_phase(kb, nblk, qts, k_rows, tk):
    rows = pl.ds(pl.multiple_of(kb * tk, tk), nblk * tk)
    return [jnp.dot(k_rows(g, rows), qts[g], preferred_element_type=F32) for g in range(len(qts))]


def _pv_phase(kb, nblk, ps, vt_ref, tk):
    pvs = []
    for g, p in enumerate(ps):
        pv = jnp.dot(vt_ref[g, kb], p[0:tk], preferred_element_type=F32)
        for j in range(1, nblk):
            pv += jnp.dot(vt_ref[g, kb + j], p[j * tk:(j + 1) * tk], preferred_element_type=F32)
        pvs.append(pv)
    return pvs


def _softmax_block(kb, nblk, biases, qts, k_rows, vt_ref, m_ref, l_ref, acc_ref, tk):
    hg = len(qts)
    ss = _logits_phase(kb, nblk, qts, k_rows, tk)
    if biases is not None:
        ss = [s + b for s, b in zip(ss, biases)]
    alphas, ps = [], []
    for g in range(hg):
        m_old = m_ref[g]
        m_new = jnp.maximum(m_old, jnp.max(ss[g], axis=0, keepdims=True))
        alpha = jnp.exp2(m_old - m_new)
        p = jnp.exp2(ss[g] - m_new)
        l_ref[g] = alpha * l_ref[g] + jnp.sum(p, axis=0, keepdims=True)
        m_ref[g] = m_new
        alphas.append(alpha)
        ps.append(p.astype(BF))
    pvs = _pv_phase(kb, nblk, ps, vt_ref, tk)
    for g in range(hg):
        acc_ref[g] = alphas[g] * acc_ref[g] + pvs[g]


def _fixed_ref_block(kb, nblk, qts, k_rows, vt_ref, m_ref, l_ref, acc_ref, tk):
    hg = len(qts)
    ss = _logits_phase(kb, nblk, qts, k_rows, tk)
    ps = []
    for g in range(hg):
        p = jnp.exp2(ss[g] - m_ref[g])
        l_ref[g] += jnp.sum(p, axis=0, keepdims=True)
        ps.append(p.astype(BF))
    pvs = _pv_phase(kb, nblk, ps, vt_ref, tk)
    for g in range(hg):
        acc_ref[g] += pvs[g]


def _far_sweep(n_far, far_blocks, step):
    n_big = n_far // far_blocks

    def big(i, carry):
        step(i * far_blocks, far_blocks)
        return carry

    lax.fori_loop(0, n_big, big, 0)
    done = n_big * far_blocks
    size = far_blocks // 2
    while size >= 1:
        take = ((n_far - done) // size) * size

        @pl.when(take > 0)
        def _(done=done, size=size):
            step(done, size)

        done = done + take
        size //= 2


def _overflowed(l_ref, acc_ref):
    bad = jnp.maximum(jnp.max(jnp.where(jnp.isfinite(l_ref[...]), 0.0, 1.0)),
                      jnp.max(jnp.where(jnp.isfinite(acc_ref[...]), 0.0, 1.0)))
    return bad > 0.0


def _attention_sweep(n_far, far_blocks, near_step, far_std, far_fixed, m_ref, l_ref, acc_ref):
    _init_softmax_state(m_ref, l_ref, acc_ref)
    near_step()
    _far_sweep(n_far, far_blocks, far_fixed)

    @pl.when(_overflowed(l_ref, acc_ref))
    def _():
        _init_softmax_state(m_ref, l_ref, acc_ref)
        near_step()

        def one(kb, carry):
            far_std(kb)
            return carry

        lax.fori_loop(0, n_far, one, 0)


def _resident_spec(shape, index_map):
    return pl.BlockSpec(shape, index_map, pipeline_mode=pl.Buffered(1))


def _init_softmax_state(m_ref, l_ref, acc_ref):
    m_ref[...] = jnp.full(m_ref.shape, NEG_INF, F32)
    l_ref[...] = jnp.zeros(l_ref.shape, F32)
    acc_ref[...] = jnp.zeros(acc_ref.shape, F32)


def _diff_attn_kernel(q_ref, k_ref, vt_ref, bn_ref, lam_ref, g_ref, o_ref,
                      m_ref, l_ref, acc_ref, *, tq, tk, hg, lam_init):
    qb = pl.program_id(1)
    qts = [jnp.concatenate([q_ref[2 * g], q_ref[2 * g + 1]], axis=1) for g in range(hg)]

    def k_rows(g, rows):
        return k_ref[rows, g * HEAD_W:(g + 1) * HEAD_W]

    state = (vt_ref, m_ref, l_ref, acc_ref, tk)
    first = (qb == 0).astype(jnp.int32)

    def near_step():
        _softmax_block(qb - 1 + first, 2, [bn_ref[g, first] for g in range(hg)], qts, k_rows, *state)

    _attention_sweep(
        jnp.maximum(qb - 1, 0), DIFF_FAR_BLOCKS, near_step,
        lambda kb: _softmax_block(kb, 1, None, qts, k_rows, *state),
        lambda kb, nblk: _fixed_ref_block(kb, nblk, qts, k_rows, *state),
        m_ref, l_ref, acc_ref)

    lp = lam_ref[...]
    lam = (jnp.exp(jnp.sum(lp[0:1] * lp[1:2], axis=1, keepdims=True))
           - jnp.exp(jnp.sum(lp[2:3] * lp[3:4], axis=1, keepdims=True)) + lam_init)
    for g in range(hg):
        o = acc_ref[g] / l_ref[g]
        d = o[:, :tq] - lam * o[:, tq:]
        ms = jnp.mean(d * d, axis=0, keepdims=True)
        d = d * lax.rsqrt(ms + EPS) * g_ref[...] * (1.0 - lam_init)
        o_ref[:, g * HEAD_W:(g + 1) * HEAD_W] = d.T.astype(o_ref.dtype)


def _diff_attn_call(qpad, k, vt4, bias_near, lam_p, sub_g, *, lam_init):
    s = k.shape[0]
    tq = tk = min(ATT_TQ, s)
    nkb = s // tk
    n = 2 * tq
    hg = DIFF_HEAD_GROUP
    return pl.pallas_call(
        functools.partial(_diff_attn_kernel, tq=tq, tk=tk, hg=hg, lam_init=lam_init),
        grid=(DA_HEADS // hg, s // tq),
        in_specs=[pl.BlockSpec((2 * hg, HEAD_W, tq), lambda h, i: (h, 0, i)),
                  _resident_spec((s, hg * HEAD_W), lambda h, i: (0, h)),
                  _resident_spec((hg, nkb, HEAD_W, tk), lambda h, i: (h, 0, 0, 0)),
                  _resident_spec((hg, 2, 2 * tk, n), lambda h, i: (h, 0, 0, 0)),
                  pl.BlockSpec((4, DA_QK), lambda h, i: (0, 0)),
                  pl.BlockSpec((HEAD_W, 1), lambda h, i: (0, 0))],
        out_specs=pl.BlockSpec((tq, hg * HEAD_W), lambda h, i: (i, h)),
        out_shape=jax.ShapeDtypeStruct((s, DA_HEADS * HEAD_W), BF),
        scratch_shapes=[pltpu.VMEM((hg, 1, n), F32), pltpu.VMEM((hg, 1, n), F32),
                        pltpu.VMEM((hg, HEAD_W, n), F32)],
        compiler_params=_cparams("arbitrary", "arbitrary"),
        name="diff_attention",
    )(qpad, k, vt4, bias_near, lam_p, sub_g)


def _mla_attn_kernel(q_ref, k_ref, vt_ref, mask_ref, o_ref, m_ref, l_ref, acc_ref, *, tk, hg):
    qb = pl.program_id(1)
    qts = [q_ref[g] for g in range(hg)]

    def k_rows(g, rows):
        return k_ref[g, rows, :]

    state = (vt_ref, m_ref, l_ref, acc_ref, tk)

    def near_step():
        _softmax_block(qb, 1, [mask_ref[...]] * hg, qts, k_rows, *state)

    _attention_sweep(
        qb, MLA_FAR_BLOCKS, near_step,
        lambda kb: _softmax_block(kb, 1, None, qts, k_rows, *state),
        lambda kb, nblk: _fixed_ref_block(kb, nblk, qts, k_rows, *state),
        m_ref, l_ref, acc_ref)
    for g in range(hg):
        o = acc_ref[g] / l_ref[g]
        o_ref[:, g * HEAD_W:(g + 1) * HEAD_W] = o.T.astype(o_ref.dtype)


def _mla_attn_call(qt, k, vt4, mask_diag):
    s = k.shape[1]
    tq = tk = min(ATT_TQ, s)
    nkb = s // tk
    hg = MLA_HEAD_GROUP
    return pl.pallas_call(
        functools.partial(_mla_attn_kernel, tk=tk, hg=hg),
        grid=(MLA_HEADS // hg, s // tq),
        in_specs=[pl.BlockSpec((hg, MLA_QK_PAD, tq), lambda h, i: (h, 0, i)),
                  _resident_spec((hg, s, MLA_QK_PAD), lambda h, i: (h, 0, 0)),
                  _resident_spec((hg, nkb, HEAD_W, tk), lambda h, i: (h, 0, 0, 0)),
                  pl.BlockSpec((tk, tq), lambda h, i: (0, 0))],
        out_specs=pl.BlockSpec((tq, hg * HEAD_W), lambda h, i: (i, h)),
        out_shape=jax.ShapeDtypeStruct((s, MLA_HEADS * HEAD_W), BF),
        scratch_shapes=[pltpu.VMEM((hg, 1, tq), F32), pltpu.VMEM((hg, 1, tq), F32),
                        pltpu.VMEM((hg, HEAD_W, tq), F32)],
        compiler_params=_cparams("arbitrary", "arbitrary"),
        name="mla_attention",
    )(qt, k, vt4, mask_diag)


def _stick_blocks(blocks, qts, k_ref, vt_ref, tri_neg, carry_ref, acc_ref, tk):
    hg = len(qts)
    zs = []
    for kb, _ in blocks:
        rows = pl.ds(pl.multiple_of(kb * tk, tk), tk)
        zs.append([jnp.dot(k_ref[rows, g * HEAD_W:(g + 1) * HEAD_W], qts[g],
                           preferred_element_type=F32) for g in range(hg)])
    carries = [carry_ref[g] for g in range(hg)]
    logsigs, his, los = [], [], []
    for b, (_, mask) in enumerate(blocks):
        for g in range(hg):
            z = zs[b][g]
            sp = jnp.maximum(z, 0.0) + jnp.log2(1.0 + jnp.exp2(-jnp.abs(z)))
            logsigs.append(z - sp + carries[g])
            if mask is not None:
                sp = jnp.where(mask, sp, 0.0)
            hi = sp.astype(BF)
            his.append(hi)
            los.append((sp - hi.astype(F32)).astype(BF))
            carries[g] = carries[g] - jnp.sum(sp, axis=0, keepdims=True)
    betweens = [jnp.dot(tri_neg, hi, preferred_element_type=F32)
                + jnp.dot(tri_neg, lo, preferred_element_type=F32) for hi, lo in zip(his, los)]
    ws = []
    for b, (_, mask) in enumerate(blocks):
        for g in range(hg):
            w = jnp.exp2(logsigs[b * hg + g] + betweens[b * hg + g])
            if mask is not None:
                w = jnp.where(mask, w, 0.0)
            ws.append(w.astype(BF))
    for g in range(hg):
        pv = jnp.dot(vt_ref[g, blocks[0][0]], ws[g], preferred_element_type=F32)
        for b in range(1, len(blocks)):
            pv += jnp.dot(vt_ref[g, blocks[b][0]], ws[b * hg + g], preferred_element_type=F32)
        acc_ref[g] += pv
        carry_ref[g] = carries[g]


def _stick_kernel(q_ref, k_ref, vt_ref, tri_ref, o_ref, carry_ref, acc_ref, *, tq, tk, hg):
    qb = pl.program_id(1)
    qts = [q_ref[g * HEAD_W:(g + 1) * HEAD_W, :] for g in range(hg)]
    tri = tri_ref[...]
    carry_ref[...] = jnp.zeros(carry_ref.shape, F32)
    acc_ref[...] = jnp.zeros(acc_ref.shape, F32)
    kpos = lax.broadcasted_iota(jnp.int32, (tk, tq), 0)
    qpos = lax.broadcasted_iota(jnp.int32, (tk, tq), 1)
    causal = kpos < qpos
    state = (qts, k_ref, vt_ref, tri, carry_ref, acc_ref, tk)

    @pl.when(qb == 0)
    def _():
        _stick_blocks([(qb, causal)], *state)

    @pl.when(qb > 0)
    def _():
        _stick_blocks([(qb, causal), (qb - 1, None)], *state)

    def more(state_):
        i, cmax = state_
        return jnp.logical_and(i < qb, cmax > SB_LOG2W_UNDERFLOW)

    def earlier(state_):
        i, _ = state_
        _stick_blocks([(qb - 1 - i, None)], *state)
        return i + 1, jnp.max(carry_ref[...])

    lax.while_loop(more, earlier, (jnp.int32(1), jnp.max(carry_ref[...])))
    for g in range(hg):
        o_ref[:, g * HEAD_W:(g + 1) * HEAD_W] = acc_ref[g].T.astype(o_ref.dtype)


def _stick_call(qt, k, vt4, tri):
    s = k.shape[0]
    tq = tk = min(ATT_TQ, s)
    nkb = s // tk
    hg = SB_HEAD_GROUP
    return pl.pallas_call(
        functools.partial(_stick_kernel, tq=tq, tk=tk, hg=hg),
        grid=(SB_HEADS // hg, s // tq),
        in_specs=[pl.BlockSpec((hg * HEAD_W, tq), lambda h, i: (h, i)),
                  _resident_spec((s, hg * HEAD_W), lambda h, i: (0, h)),
                  _resident_spec((hg, nkb, HEAD_W, tk), lambda h, i: (h, 0, 0, 0)),
                  pl.BlockSpec((tk, tk), lambda h, i: (0, 0))],
        out_specs=pl.BlockSpec((tq, hg * HEAD_W), lambda h, i: (i, h)),
        out_shape=jax.ShapeDtypeStruct((s, SB_HEADS * HEAD_W), BF),
        scratch_shapes=[pltpu.VMEM((hg, 1, tq), F32), pltpu.VMEM((hg, HEAD_W, tq), F32)],
        compiler_params=_cparams("arbitrary", "arbitrary"),
        name="stick_breaking",
    )(qt, k, vt4, tri)


def _merge_kernel(ht_ref, ya_ref, yb_ref, yc_ref, wga_ref, wgb_ref, wgc_ref, wb_ref, o_ref):
    ht = ht_ref[...]
    acc = None
    for n, (y_ref, wg_ref) in enumerate(((ya_ref, wga_ref), (yb_ref, wgb_ref), (yc_ref, wgc_ref))):
        gate = jax.nn.sigmoid(jnp.dot(wg_ref[...], ht, preferred_element_type=F32)).T
        up = jnp.dot(y_ref[...], wb_ref[n], preferred_element_type=F32)
        acc = gate * up if acc is None else acc + gate * up
    o_ref[...] = acc.astype(o_ref.dtype)


def _merge_call(ht, ya, yb, yc, wt_tail, wb, layer):
    d, s = ht.shape
    tm = min(512, s)
    tn = 512
    ysp = pl.BlockSpec((tm, BRANCH_WIDTH), lambda i, j: (i, 0))

    def gate_spec(n):
        blk0 = (SEG_GATE * SEG + n * d) // tn
        return pl.BlockSpec((None, tn, d), lambda i, j: (layer, blk0 + j, 0))

    return pl.pallas_call(
        _merge_kernel,
        grid=(s // tm, d // tn),
        in_specs=[pl.BlockSpec((d, tm), lambda i, j: (0, i)), ysp, ysp, ysp,
                  gate_spec(0), gate_spec(1), gate_spec(2),
                  pl.BlockSpec((None, N_BRANCHES, BRANCH_WIDTH, tn), lambda i, j: (layer, 0, 0, j))],
        out_specs=pl.BlockSpec((tm, tn), lambda i, j: (i, j)),
        out_shape=jax.ShapeDtypeStruct((s, d), BF),
        compiler_params=_cparams("arbitrary", "arbitrary"),
        name="gated_merge",
    )(ht, ya, yb, yc, wt_tail, wt_tail, wt_tail, wb)


def _mlp_kernel(x_ref, ng_ref, sc_ref, sh_ref, w1_ref, w2_ref, g_ref, o_ref, acc_ref, h_ref):
    f = pl.program_id(1)

    @pl.when(f == 0)
    def _():
        acc_ref[...] = jnp.zeros(acc_ref.shape, F32)
        x = x_ref[...]
        y = x * lax.rsqrt(jnp.mean(x * x, axis=-1, keepdims=True) + EPS) * ng_ref[...]
        h_ref[...] = (y * (1.0 + sc_ref[...]) + sh_ref[...]).astype(BF)

    u = jnp.dot(h_ref[...], w1_ref[...], preferred_element_type=F32)
    u = jnp.square(jnp.maximum(u, 0.0)).astype(BF)
    acc_ref[...] += jnp.dot(u, w2_ref[...], preferred_element_type=F32)

    @pl.when(f == pl.num_programs(1) - 1)
    def _():
        o_ref[...] = x_ref[...] + g_ref[...] * acc_ref[...]


def _mlp_call(x, norm_g, sc, sh, w1, w2, layer, g):
    s, d = x.shape
    ff = w1.shape[2]
    tm = min(512, s)
    tf = 1024
    row = pl.BlockSpec((1, d), lambda i, f: (0, 0))
    return pl.pallas_call(
        _mlp_kernel,
        grid=(s // tm, ff // tf),
        in_specs=[pl.BlockSpec((tm, d), lambda i, f: (i, 0)), row, row, row,
                  pl.BlockSpec((None, d, tf), lambda i, f: (layer, 0, f)),
                  pl.BlockSpec((None, tf, d), lambda i, f: (layer, f, 0)),
                  row],
        out_specs=pl.BlockSpec((tm, d), lambda i, f: (i, 0)),
        out_shape=jax.ShapeDtypeStruct((s, d), F32),
        scratch_shapes=[pltpu.VMEM((tm, d), F32), pltpu.VMEM((tm, d), BF)],
        compiler_params=_cparams("arbitrary", "arbitrary"),
        name="sqrelu_mlp",
    )(x, norm_g, sc, sh, w1, w2, g)


def _t5_bucket(rel):
    nb = T5_BUCKETS // 2
    max_exact = nb // 2
    n = jnp.abs(rel)
    large = max_exact + (jnp.log(jnp.maximum(n, 1).astype(F32) / max_exact)
                         / math.log(T5_MAX_DIST / max_exact) * (nb - max_exact)).astype(jnp.int32)
    large = jnp.minimum(large, nb - 1)
    return jnp.where(rel > 0, nb, 0) + jnp.where(n < max_exact, n, large)


def _bias_tiles(t5_bias, tq, tk):
    table = t5_bias.astype(F32)

    def lookup(bucket):
        hit = bucket[..., None, None] == jnp.arange(T5_BUCKETS)[:, None]
        return jnp.sum(jnp.where(hit, table, 0.0), axis=-2)

    kl = jnp.arange(tk)[:, None]
    ql = jnp.arange(tq)[None, :]
    far = lookup(_t5_bucket(jnp.array(-(tk + tq), jnp.int32)))
    diag = jnp.transpose(lookup(_t5_bucket(kl - ql)) - far, (2, 0, 1))
    prev = jnp.transpose(lookup(_t5_bucket(kl - tk - ql)) - far, (2, 0, 1))
    allowed = (kl // CHUNK) <= (ql // CHUNK)
    diag = jnp.where(allowed[None], diag * LOG2E, NEG_INF)
    near = jnp.stack([jnp.concatenate([prev * LOG2E, diag], axis=1),
                      jnp.concatenate([diag, jnp.full_like(diag, NEG_INF)], axis=1)], axis=1)
    return jnp.concatenate([near, near], axis=3)


def _rope_tables(s):
    half = MLA_ROPE // 2
    inv = ROPE_BASE ** (-jnp.arange(half, dtype=F32) / half)
    ang = jnp.arange(s).astype(F32)[:, None] * inv[None, :]
    return jnp.cos(ang).T, jnp.sin(ang).T


def kernel(x, c, w_ada, b_ada, norm_mix_g, norm_mlp_g, w_in, diff_qk_g, diff_lambda, diff_subln_g,
           t5_bias, mla_q_norm_g, mla_kv_norm_g, w_q_up, w_kv_up, mla_qk_g, w_branch, w_out,
           w_mlp_in, w_mlp_out):
    b, s, d = x.shape
    assert b == 1 and d == D_MODEL
    depth = w_ada.shape[0]
    tq = tk = min(ATT_TQ, s)
    hw = MLA_NOPE + MLA_ROPE

    mod = _mod_call(jnp.broadcast_to(c, (8, d)).astype(BF), w_ada, b_ada[:, None, :])[:, 0, :]
    mod = mod.reshape(depth, N_MOD, 1, d)

    wt_head = jnp.swapaxes(w_in[:, :, :HEAD_ROWS], 1, 2).astype(BF)
    wt_tail = jnp.swapaxes(w_in[:, :, TAIL_ROW0:], 1, 2).astype(BF)
    w_br = w_branch.astype(BF)
    w_o = w_out.astype(BF)
    w_1 = w_mlp_in.astype(BF)
    w_2 = w_mlp_out.astype(BF)
    wt_qu = jnp.swapaxes(w_q_up, 1, 2).astype(BF)
    wkv = w_kv_up.reshape(depth, MLA_KV_LORA, MLA_HEADS, MLA_NOPE + MLA_V)
    wt_kn = jnp.swapaxes(wkv[..., :MLA_NOPE].reshape(depth, MLA_KV_LORA, -1), 1, 2).astype(BF)
    wt_v = jnp.swapaxes(wkv[..., MLA_NOPE:].reshape(depth, MLA_KV_LORA, -1), 1, 2).astype(BF)

    bias_near = _bias_tiles(t5_bias, tq, tk)
    kl = jnp.arange(tk)[:, None]
    ql = jnp.arange(tq)[None, :]
    mask_diag = jnp.where((kl // CHUNK) <= (ql // CHUNK), 0.0, NEG_INF).astype(F32)
    tri = -(jnp.arange(tk)[None, :] > jnp.arange(tk)[:, None]).astype(BF)
    cos_t, sin_t = _rope_tables(s)

    ones_col = jnp.ones((SEG, 1), F32)
    sb_scale = jnp.full((SB_HEADS * SB_DIM, 1), SB_DIM ** -0.5 * LOG2E, F32)
    da_scale = DA_QK ** -0.5 * LOG2E
    mla_scale = hw ** -0.5 * LOG2E

    x2 = x[0]
    for l in range(depth):
        sh1, sc1, g1, sh2, sc2, g2 = (mod[l, i] for i in range(N_MOD))
        lam_init = 0.8 - 0.6 * math.exp(-0.3 * l)

        ht = _norm_call(x2, norm_mix_g[l][None], sc1, sh1)

        gq_rows = jnp.tile(diff_qk_g[l, 0], 2 * DA_HEADS)[:, None] * da_scale
        gk_rows = jnp.tile(diff_qk_g[l, 1], 2 * DA_HEADS)[:, None]
        qa, ka, va, cbt = _proj_call(
            wt_head, l, SEG_DQ, ht, jnp.concatenate([gq_rows, gk_rows, ones_col, ones_col]),
            [(DA_QK, "qpad", BF), (DA_QK, "n", BF), (0, "t4", BF), (0, "t", F32)], name="proj_head")
        qc, kc, vc = _proj_call(
            wt_tail, l, SEG_SQ, ht, jnp.concatenate([sb_scale, ones_col, ones_col]),
            [(0, "t", BF), (0, "n", BF), (0, "t4", BF)], name="proj_tail")
        gqk = mla_qk_g[l]
        qb_t, kb, vb_t = _mla_up_call(
            cbt, l, mla_q_norm_g[l][:, None], mla_kv_norm_g[l][:, None], wt_qu, wt_kn, wt_v,
            gqk[0, :MLA_NOPE, None] * mla_scale, gqk[0, MLA_NOPE:, None] * mla_scale,
            gqk[1, :MLA_NOPE, None], gqk[1, MLA_NOPE:, None], cos_t, sin_t)

        ya = _diff_attn_call(qa, ka, va, bias_near, diff_lambda[l], diff_subln_g[l][:, None],
                             lam_init=lam_init)
        yb = _mla_attn_call(qb_t, kb, vb_t, mask_diag)
        yc = _stick_call(qc, kc, vc, tri)

        merged = _merge_call(ht, ya, yb, yc, wt_tail, w_br, l)
        x2 = _resid_proj_call(merged, w_o, l, x2, g1)

        x2 = _mlp_call(x2, norm_mlp_g[l][None], sc2, sh2, w_1, w_2, l, g2)
    return x2[None]
```

```python
import functools
import math

import jax
import jax.numpy as jnp
from jax import lax
from jax.experimental import pallas as pl
from jax.experimental.pallas import tpu as pltpu

BF = jnp.bfloat16
F32 = jnp.float32

D_MODEL = 2048
CHUNK = 64
DA_HEADS = 8
DA_QK = 64
DA_V = 128
MLA_HEADS = 8
MLA_Q_LORA = 512
MLA_KV_LORA = 256
MLA_NOPE = 128
MLA_ROPE = 64
MLA_V = 128
MLA_QK_PAD = 256
ROPE_BASE = 10000.0
SB_HEADS = 8
SB_DIM = 128
BRANCH_WIDTH = 1024
N_BRANCHES = 3
T5_BUCKETS = 32
T5_MAX_DIST = 128
D_FF = 4 * D_MODEL
N_MOD = 6
EPS = 1e-6
NEG_INF = -1e30
LOG2E = 1.4426950408889634

HEAD_W = 128
ATT_TQ = 256
ATT_TK = 256
ATT_QSUB = 2
DIFF_FAR_BLOCKS = 4
MLA_FAR_BLOCKS = 4
DIFF_HEAD_GROUP = 4
MLA_HEAD_GROUP = 4
SB_HEAD_GROUP = 4
SB_LOG2W_UNDERFLOW = -152.0
V7X_VMEM_LIMIT = 56 * 1024 * 1024

SEG = 1024
HEAD_ROWS = 4 * SEG
TAIL_ROW0 = 3 * SEG + MLA_Q_LORA + MLA_KV_LORA + MLA_ROPE
SEG_DQ, SEG_DK, SEG_DV, SEG_LAT = 0, 1, 2, 3
SEG_SQ, SEG_SK, SEG_SV, SEG_GATE = 0, 1, 2, 3
LAT_KPE = MLA_Q_LORA + MLA_KV_LORA


def _cparams(*sem):
    return pltpu.CompilerParams(dimension_semantics=sem, vmem_limit_bytes=V7X_VMEM_LIMIT)


def _mod_kernel(c_ref, w_ref, b_ref, o_ref):
    w = w_ref[0].astype(BF)
    o_ref[0] = jnp.dot(c_ref[...], w, preferred_element_type=F32) + b_ref[0]


def _mod_call(c8, w_ada, b_ada3):
    depth, d, n = w_ada.shape
    tn = 1024
    return pl.pallas_call(
        _mod_kernel,
        grid=(depth, n // tn),
        in_specs=[pl.BlockSpec((8, d), lambda l, j: (0, 0)),
                  pl.BlockSpec((1, d, tn), lambda l, j: (l, 0, j)),
                  pl.BlockSpec((1, 1, tn), lambda l, j: (l, 0, j))],
        out_specs=pl.BlockSpec((1, 8, tn), lambda l, j: (l, 0, j)),
        out_shape=jax.ShapeDtypeStruct((depth, 8, n), F32),
        compiler_params=_cparams("arbitrary", "arbitrary"),
        name="adaln_mod",
    )(c8, w_ada, b_ada3)


def _norm_kernel(x_ref, g_ref, sc_ref, sh_ref, ht_ref):
    x = x_ref[...]
    ms = jnp.mean(x * x, axis=-1, keepdims=True)
    y = x * lax.rsqrt(ms + EPS) * g_ref[...]
    ht_ref[...] = (y * (1.0 + sc_ref[...]) + sh_ref[...]).T.astype(BF)


def _norm_call(x, g, sc, sh):
    s, d = x.shape
    tm = min(512, s)
    row = pl.BlockSpec((1, d), lambda i: (0, 0))
    return pl.pallas_call(
        _norm_kernel,
        grid=(s // tm,),
        in_specs=[pl.BlockSpec((tm, d), lambda i: (i, 0)), row, row, row],
        out_specs=pl.BlockSpec((d, tm), lambda i: (0, i)),
        out_shape=jax.ShapeDtypeStruct((d, s), BF),
        compiler_params=_cparams("arbitrary"),
        name="adaln_rmsnorm",
    )(x, g, sc, sh)


def _proj_tile(w_ref, xt_ref, rs_ref, o_ref, group, layout, tk):
    y = jnp.dot(w_ref[...], xt_ref[...], preferred_element_type=F32)
    tn, tm = y.shape
    if group:
        y3 = y.reshape(tn // group, group, tm)
        ms = jnp.mean(y3 * y3, axis=1, keepdims=True)
        y = (y3 * lax.rsqrt(ms + EPS)).reshape(tn, tm)
    y = y * rs_ref[...]
    if layout == "t":
        o_ref[...] = y.astype(o_ref.dtype)
    elif layout == "n":
        o_ref[...] = y.T.astype(o_ref.dtype)
    elif layout == "t4":
        for a in range(tn // HEAD_W):
            for b in range(tm // tk):
                o_ref[a, b] = y[a * HEAD_W:(a + 1) * HEAD_W, b * tk:(b + 1) * tk].astype(o_ref.dtype)
    elif layout == "qpad":
        first = lax.broadcasted_iota(jnp.int32, (HEAD_W, tm), 0) < DA_QK
        for a in range(tn // HEAD_W):
            ya = y[a * HEAD_W:(a + 1) * HEAD_W]
            o_ref[2 * a] = jnp.where(first, ya, 0.0).astype(o_ref.dtype)
            o_ref[2 * a + 1] = jnp.where(first, 0.0, ya).astype(o_ref.dtype)


def _proj_kernel(w_ref, xt_ref, rs_ref, *o_refs, specs, tk):
    seg = pl.program_id(0)
    for k, (group, layout, _) in enumerate(specs):
        @pl.when(seg == k)
        def _(k=k, group=group, layout=layout):
            _proj_tile(w_ref, xt_ref, rs_ref, o_refs[k], group, layout, tk)


def _proj_call(wt, layer, seg0, ht, rowscale, specs, *, name):
    k, s = ht.shape
    nseg = len(specs)
    tn = SEG
    tm = min(1024, s)
    nj = s // tm
    tk = min(ATT_TK, s)
    out_shapes, out_specs = [], []
    for seg, (_, layout, out_dtype) in enumerate(specs):
        def tile(i, j, seg=seg):
            return jnp.where(i < seg, 0, jnp.where(i > seg, nj - 1, j))

        if layout == "t":
            out_shapes.append(jax.ShapeDtypeStruct((tn, s), out_dtype))
            out_specs.append(pl.BlockSpec((tn, tm), lambda i, j, t=tile: (0, t(i, j))))
        elif layout == "n":
            out_shapes.append(jax.ShapeDtypeStruct((s, tn), out_dtype))
            out_specs.append(pl.BlockSpec((tm, tn), lambda i, j, t=tile: (t(i, j), 0)))
        elif layout == "t4":
            out_shapes.append(jax.ShapeDtypeStruct((tn // HEAD_W, s // tk, HEAD_W, tk), out_dtype))
            out_specs.append(pl.BlockSpec((tn // HEAD_W, tm // tk, HEAD_W, tk),
                                          lambda i, j, t=tile: (0, t(i, j), 0, 0)))
        else:
            out_shapes.append(jax.ShapeDtypeStruct((2 * tn // HEAD_W, HEAD_W, s), out_dtype))
            out_specs.append(pl.BlockSpec((2 * tn // HEAD_W, HEAD_W, tm),
                                          lambda i, j, t=tile: (0, 0, t(i, j))))
    return pl.pallas_call(
        functools.partial(_proj_kernel, specs=tuple(specs), tk=tk),
        grid=(nseg, nj),
        in_specs=[pl.BlockSpec((None, tn, k), lambda i, j: (layer, seg0 + i, 0)),
                  pl.BlockSpec((k, tm), lambda i, j: (0, j)),
                  pl.BlockSpec((tn, 1), lambda i, j: (i, 0))],
        out_specs=out_specs,
        out_shape=out_shapes,
        compiler_params=_cparams("arbitrary", "arbitrary"),
        name=name,
    )(wt, ht, rowscale)


def _resid_proj_kernel(a_ref, w_ref, x_ref, g_ref, o_ref):
    y = jnp.dot(a_ref[...], w_ref[...], preferred_element_type=F32)
    o_ref[...] = x_ref[...] + g_ref[...] * y


def _resid_proj_call(a, w, layer, x, g):
    s, k = a.shape
    n = w.shape[2]
    tm = min(1024, s)
    tn = min(1024, n)
    return pl.pallas_call(
        _resid_proj_kernel,
        grid=(s // tm, n // tn),
        in_specs=[pl.BlockSpec((tm, k), lambda i, j: (i, 0)),
                  pl.BlockSpec((None, k, tn), lambda i, j: (layer, 0, j)),
                  pl.BlockSpec((tm, tn), lambda i, j: (i, j)),
                  pl.BlockSpec((1, tn), lambda i, j: (0, j))],
        out_specs=pl.BlockSpec((tm, tn), lambda i, j: (i, j)),
        out_shape=jax.ShapeDtypeStruct((s, n), F32),
        compiler_params=_cparams("arbitrary", "arbitrary"),
        name="out_proj_residual",
    )(a, w, x, g)


def _rms_rows(x, g):
    return x * lax.rsqrt(jnp.mean(x * x, axis=0, keepdims=True) + EPS) * g


def _rope_rows(x, cos_t, sin_t):
    half = MLA_ROPE // 2
    x1, x2 = x[:half], x[half:]
    return x1 * cos_t - x2 * sin_t, x1 * sin_t + x2 * cos_t


def _mla_up_kernel(cb_ref, gq_ref, gkv_ref, wq_ref, wk_ref, wv_ref, gqn_ref, gqp_ref, gkn_ref,
                   gkp_ref, cost_ref, sint_ref, qt_ref, k_ref, vt_ref, *, tk):
    tm = cb_ref.shape[1]
    hw = MLA_NOPE + MLA_ROPE
    cos_t = cost_ref[...]
    sin_t = sint_ref[...]
    cqn = _rms_rows(cb_ref[0:MLA_Q_LORA], gq_ref[...]).astype(BF)
    qt = jnp.dot(wq_ref[...], cqn, preferred_element_type=F32)
    qzero = jnp.zeros((MLA_QK_PAD - hw, tm), BF)
    for h in range(MLA_HEADS):
        nope = _rms_rows(qt[h * hw:h * hw + MLA_NOPE], gqn_ref[...])
        pe1, pe2 = _rope_rows(_rms_rows(qt[h * hw + MLA_NOPE:(h + 1) * hw], gqp_ref[...]), cos_t, sin_t)
        qt_ref[h, 0:MLA_NOPE] = nope.astype(BF)
        qt_ref[h, MLA_NOPE:MLA_NOPE + MLA_ROPE // 2] = pe1.astype(BF)
        qt_ref[h, MLA_NOPE + MLA_ROPE // 2:hw] = pe2.astype(BF)
        qt_ref[h, hw:MLA_QK_PAD] = qzero
    ckvn = _rms_rows(cb_ref[MLA_Q_LORA:LAT_KPE], gkv_ref[...]).astype(BF)
    kt = jnp.dot(wk_ref[...], ckvn, preferred_element_type=F32)
    vt = jnp.dot(wv_ref[...], ckvn, preferred_element_type=F32)
    kpe1, kpe2 = _rope_rows(_rms_rows(cb_ref[LAT_KPE:LAT_KPE + MLA_ROPE], gkp_ref[...]), cos_t, sin_t)
    kzero = jnp.zeros((MLA_QK_PAD - hw, tm), F32)
    for h in range(MLA_HEADS):
        kn = _rms_rows(kt[h * MLA_NOPE:(h + 1) * MLA_NOPE], gkn_ref[...])
        k_ref[h] = jnp.concatenate([kn, kpe1, kpe2, kzero], axis=0).T.astype(BF)
        for b in range(tm // tk):
            vt_ref[h, b] = vt[h * MLA_V:(h + 1) * MLA_V, b * tk:(b + 1) * tk].astype(BF)


def _mla_up_call(cbt, layer, gq, gkv, wq, wk, wv, gqn, gqp, gkn, gkp, cos_t, sin_t):
    s = cbt.shape[1]
    tm = min(512, s)
    tk = min(ATT_TK, s)
    hw = MLA_NOPE + MLA_ROPE
    half = MLA_ROPE // 2

    def const(shape):
        return pl.BlockSpec(shape, lambda i: tuple(0 for _ in shape))

    def layer_w(shape):
        return pl.BlockSpec((None,) + shape, lambda i: (layer, 0, 0))

    return pl.pallas_call(
        functools.partial(_mla_up_kernel, tk=tk),
        grid=(s // tm,),
        in_specs=[pl.BlockSpec((cbt.shape[0], tm), lambda i: (0, i)),
                  const((MLA_Q_LORA, 1)), const((MLA_KV_LORA, 1)),
                  layer_w((MLA_HEADS * hw, MLA_Q_LORA)),
                  layer_w((MLA_HEADS * MLA_NOPE, MLA_KV_LORA)),
                  layer_w((MLA_HEADS * MLA_V, MLA_KV_LORA)),
                  const((MLA_NOPE, 1)), const((MLA_ROPE, 1)), const((MLA_NOPE, 1)),
                  const((MLA_ROPE, 1)),
                  pl.BlockSpec((half, tm), lambda i: (0, i)),
                  pl.BlockSpec((half, tm), lambda i: (0, i))],
        out_specs=[pl.BlockSpec((MLA_HEADS, MLA_QK_PAD, tm), lambda i: (0, 0, i)),
                   pl.BlockSpec((MLA_HEADS, tm, MLA_QK_PAD), lambda i: (0, i, 0)),
                   pl.BlockSpec((MLA_HEADS, tm // tk, MLA_V, tk), lambda i: (0, i, 0, 0))],
        out_shape=[jax.ShapeDtypeStruct((MLA_HEADS, MLA_QK_PAD, s), BF),
                   jax.ShapeDtypeStruct((MLA_HEADS, s, MLA_QK_PAD), BF),
                   jax.ShapeDtypeStruct((MLA_HEADS, s // tk, MLA_V, tk), BF)],
        compiler_params=_cparams("arbitrary"),
        name="mla_up",
    )(cbt, gq, gkv, wq, wk, wv, gqn, gqp, gkn, gkp, cos_t, sin_t)


def _logits_phase(kb, nblk, qts, k_rows, tk):
    rows = pl.ds(pl.multiple_of(kb * tk, tk), nblk * tk)
    return [jnp.dot(k_rows(g, rows), qts[g], preferred_element_type=F32) for g in range(len(qts))]


def _pv_phase(kb, nblk, ps, vt_ref, tk):
    pvs = []
    for g, p in enumerate(ps):
        pv = jnp.dot(vt_ref[g, kb], p[0:tk], preferred_element_type=F32)
        for j in range(1, nblk):
            pv += jnp.dot(vt_ref[g, kb + j], p[j * tk:(j + 1) * tk], preferred_element_type=F32)
        pvs.append(pv)
    return pvs


def _softmax_block(kb, nblk, biases, qts, k_rows, vt_ref, m_ref, l_ref, acc_ref, tk):
    hg = len(qts)
    ss = _logits_phase(kb, nblk, qts, k_rows, tk)
    if biases is not None:
        ss = [s + b for s, b in zip(ss, biases)]
    alphas, ps = [], []
    for g in range(hg):
        m_old = m_ref[g]
        m_new = jnp.maximum(m_old, jnp.max(ss[g], axis=0, keepdims=True))
        alpha = jnp.exp2(m_old - m_new)
        p = jnp.exp2(ss[g] - m_new)
        l_ref[g] = alpha * l_ref[g] + jnp.sum(p, axis=0, keepdims=True)
        m_ref[g] = m_new
        alphas.append(alpha)
        ps.append(p.astype(BF))
    pvs = _pv_phase(kb, nblk, ps, vt_ref, tk)
    for g in range(hg):
        acc_ref[g] = alphas[g] * acc_ref[g] + pvs[g]


def _fixed_ref_block(kb, nblk, qts, k_rows, vt_ref, m_ref, l_ref, acc_ref, tk):
    hg = len(qts)
    ss = _logits_phase(kb, nblk, qts, k_rows, tk)
    ps = []
    for g in range(hg):
        p = jnp.exp2(ss[g] - m_ref[g])
        l_ref[g] += jnp.sum(p, axis=0, keepdims=True)
        ps.append(p.astype(BF))
    pvs = _pv_phase(kb, nblk, ps, vt_ref, tk)
    for g in range(hg):
        acc_ref[g] += pvs[g]


def _far_sweep(n_far, far_blocks, step):
    n_big = n_far // far_blocks

    def big(i, carry):
        step(i * far_blocks, far_blocks)
        return carry

    lax.fori_loop(0, n_big, big, 0)
    done = n_big * far_blocks
    size = far_blocks // 2
    while size >= 1:
        take = ((n_far - done) // size) * size

        @pl.when(take > 0)
        def _(done=done, size=size):
            step(done, size)

        done = done + take
        size //= 2


def _overflowed(l_ref, acc_ref):
    bad = jnp.maximum(jnp.max(jnp.where(jnp.isfinite(l_ref[...]), 0.0, 1.0)),
                      jnp.max(jnp.where(jnp.isfinite(acc_ref[...]), 0.0, 1.0)))
    return bad > 0.0


def _attention_sweep(n_far, far_blocks, near_step, far_std, far_fixed, m_ref, l_ref, acc_ref):
    _init_softmax_state(m_ref, l_ref, acc_ref)
    near_step()
    _far_sweep(n_far, far_blocks, far_fixed)

    @pl.when(_overflowed(l_ref, acc_ref))
    def _():
        _init_softmax_state(m_ref, l_ref, acc_ref)
        near_step()

        def one(kb, carry):
            far_std(kb)
            return carry

        lax.fori_loop(0, n_far, one, 0)


def _resident_spec(shape, index_map):
    return pl.BlockSpec(shape, index_map, pipeline_mode=pl.Buffered(1))


def _init_softmax_state(m_ref, l_ref, acc_ref):
    m_ref[...] = jnp.full(m_ref.shape, NEG_INF, F32)
    l_ref[...] = jnp.zeros(l_ref.shape, F32)
    acc_ref[...] = jnp.zeros(acc_ref.shape, F32)


def _diff_attn_kernel(q_ref, k_ref, vt_ref, bn_ref, lam_ref, g_ref, o_ref,
                      m_ref, l_ref, acc_ref, *, tq, tk, hg, lam_init):
    for sub in range(ATT_QSUB):
        _diff_query_block(pl.program_id(1) * ATT_QSUB + sub, slice(sub * tq, (sub + 1) * tq),
                          q_ref, k_ref, vt_ref, bn_ref, lam_ref, g_ref, o_ref, m_ref, l_ref, acc_ref,
                          tq=tq, tk=tk, hg=hg, lam_init=lam_init)


def _diff_query_block(qb, span, q_ref, k_ref, vt_ref, bn_ref, lam_ref, g_ref, o_ref,
                      m_ref, l_ref, acc_ref, *, tq, tk, hg, lam_init):
    qts = [jnp.concatenate([q_ref[2 * g, :, span], q_ref[2 * g + 1, :, span]], axis=1)
           for g in range(hg)]

    def k_rows(g, rows):
        return k_ref[rows, g * HEAD_W:(g + 1) * HEAD_W]

    state = (vt_ref, m_ref, l_ref, acc_ref, tk)
    first = (qb == 0).astype(jnp.int32)

    def near_step():
        _softmax_block(qb - 1 + first, 2, [bn_ref[g, first] for g in range(hg)], qts, k_rows, *state)

    _attention_sweep(
        jnp.maximum(qb - 1, 0), DIFF_FAR_BLOCKS, near_step,
        lambda kb: _softmax_block(kb, 1, None, qts, k_rows, *state),
        lambda kb, nblk: _fixed_ref_block(kb, nblk, qts, k_rows, *state),
        m_ref, l_ref, acc_ref)

    lp = lam_ref[...]
    lam = (jnp.exp(jnp.sum(lp[0:1] * lp[1:2], axis=1, keepdims=True))
           - jnp.exp(jnp.sum(lp[2:3] * lp[3:4], axis=1, keepdims=True)) + lam_init)
    for g in range(hg):
        o = acc_ref[g] / l_ref[g]
        d = o[:, :tq] - lam * o[:, tq:]
        ms = jnp.mean(d * d, axis=0, keepdims=True)
        d = d * lax.rsqrt(ms + EPS) * g_ref[...] * (1.0 - lam_init)
        o_ref[span, g * HEAD_W:(g + 1) * HEAD_W] = d.T.astype(o_ref.dtype)


def _diff_attn_call(qpad, k, vt4, bias_near, lam_p, sub_g, *, lam_init):
    s = k.shape[0]
    tq = tk = min(ATT_TQ, s)
    nkb = s // tk
    n = 2 * tq
    hg = DIFF_HEAD_GROUP
    return pl.pallas_call(
        functools.partial(_diff_attn_kernel, tq=tq, tk=tk, hg=hg, lam_init=lam_init),
        grid=(DA_HEADS // hg, s // (tq * ATT_QSUB)),
        in_specs=[pl.BlockSpec((2 * hg, HEAD_W, tq * ATT_QSUB), lambda h, i: (h, 0, i)),
                  _resident_spec((s, hg * HEAD_W), lambda h, i: (0, h)),
                  _resident_spec((hg, nkb, HEAD_W, tk), lambda h, i: (h, 0, 0, 0)),
                  _resident_spec((hg, 2, 2 * tk, n), lambda h, i: (h, 0, 0, 0)),
                  pl.BlockSpec((4, DA_QK), lambda h, i: (0, 0)),
                  pl.BlockSpec((HEAD_W, 1), lambda h, i: (0, 0))],
        out_specs=pl.BlockSpec((tq * ATT_QSUB, hg * HEAD_W), lambda h, i: (i, h)),
        out_shape=jax.ShapeDtypeStruct((s, DA_HEADS * HEAD_W), BF),
        scratch_shapes=[pltpu.VMEM((hg, 1, n), F32), pltpu.VMEM((hg, 1, n), F32),
                        pltpu.VMEM((hg, HEAD_W, n), F32)],
        compiler_params=_cparams("arbitrary", "arbitrary"),
        name="diff_attention",
    )(qpad, k, vt4, bias_near, lam_p, sub_g)


def _mla_attn_kernel(q_ref, k_ref, vt_ref, mask_ref, o_ref, m_ref, l_ref, acc_ref, *, tk, hg):
    for sub in range(ATT_QSUB):
        _mla_query_block(pl.program_id(1) * ATT_QSUB + sub, slice(sub * tk, (sub + 1) * tk),
                         q_ref, k_ref, vt_ref, mask_ref, o_ref, m_ref, l_ref, acc_ref, tk=tk, hg=hg)


def _mla_query_block(qb, span, q_ref, k_ref, vt_ref, mask_ref, o_ref, m_ref, l_ref, acc_ref, *, tk, hg):
    qts = [q_ref[g, :, span] for g in range(hg)]

    def k_rows(g, rows):
        return k_ref[g, rows, :]

    state = (vt_ref, m_ref, l_ref, acc_ref, tk)

    def near_step():
        _softmax_block(qb, 1, [mask_ref[...]] * hg, qts, k_rows, *state)

    _attention_sweep(
        qb, MLA_FAR_BLOCKS, near_step,
        lambda kb: _softmax_block(kb, 1, None, qts, k_rows, *state),
        lambda kb, nblk: _fixed_ref_block(kb, nblk, qts, k_rows, *state),
        m_ref, l_ref, acc_ref)
    for g in range(hg):
        o = acc_ref[g] / l_ref[g]
        o_ref[span, g * HEAD_W:(g + 1) * HEAD_W] = o.T.astype(o_ref.dtype)


def _mla_attn_call(qt, k, vt4, mask_diag):
    s = k.shape[1]
    tq = tk = min(ATT_TQ, s)
    nkb = s // tk
    hg = MLA_HEAD_GROUP
    return pl.pallas_call(
        functools.partial(_mla_attn_kernel, tk=tk, hg=hg),
        grid=(MLA_HEADS // hg, s // (tq * ATT_QSUB)),
        in_specs=[pl.BlockSpec((hg, MLA_QK_PAD, tq * ATT_QSUB), lambda h, i: (h, 0, i)),
                  _resident_spec((hg, s, MLA_QK_PAD), lambda h, i: (h, 0, 0)),
                  _resident_spec((hg, nkb, HEAD_W, tk), lambda h, i: (h, 0, 0, 0)),
                  pl.BlockSpec((tk, tq), lambda h, i: (0, 0))],
        out_specs=pl.BlockSpec((tq * ATT_QSUB, hg * HEAD_W), lambda h, i: (i, h)),
        out_shape=jax.ShapeDtypeStruct((s, MLA_HEADS * HEAD_W), BF),
        scratch_shapes=[pltpu.VMEM((hg, 1, tq), F32), pltpu.VMEM((hg, 1, tq), F32),
                        pltpu.VMEM((hg, HEAD_W, tq), F32)],
        compiler_params=_cparams("arbitrary", "arbitrary"),
        name="mla_attention",
    )(qt, k, vt4, mask_diag)


def _stick_blocks(blocks, qts, k_ref, vt_ref, tri_neg, carry_ref, acc_ref, tk):
    hg = len(qts)
    zs = []
    for kb, _ in blocks:
        rows = pl.ds(pl.multiple_of(kb * tk, tk), tk)
        zs.append([jnp.dot(k_ref[rows, g * HEAD_W:(g + 1) * HEAD_W], qts[g],
                           preferred_element_type=F32) for g in range(hg)])
    carries = [carry_ref[g] for g in range(hg)]
    logsigs, his, los = [], [], []
    for b, (_, mask) in enumerate(blocks):
        for g in range(hg):
            z = zs[b][g]
            sp = jnp.maximum(z, 0.0) + jnp.log2(1.0 + jnp.exp2(-jnp.abs(z)))
            logsigs.append(z - sp + carries[g])
            if mask is not None:
                sp = jnp.where(mask, sp, 0.0)
            hi = sp.astype(BF)
            his.append(hi)
            los.append((sp - hi.astype(F32)).astype(BF))
            carries[g] = carries[g] - jnp.sum(sp, axis=0, keepdims=True)
    betweens = [jnp.dot(tri_neg, hi, preferred_element_type=F32)
                + jnp.dot(tri_neg, lo, preferred_element_type=F32) for hi, lo in zip(his, los)]
    ws = []
    for b, (_, mask) in enumerate(blocks):
        for g in range(hg):
            w = jnp.exp2(logsigs[b * hg + g] + betweens[b * hg + g])
            if mask is not None:
                w = jnp.where(mask, w, 0.0)
            ws.append(w.astype(BF))
    for g in range(hg):
        pv = jnp.dot(vt_ref[g, blocks[0][0]], ws[g], preferred_element_type=F32)
        for b in range(1, len(blocks)):
            pv += jnp.dot(vt_ref[g, blocks[b][0]], ws[b * hg + g], preferred_element_type=F32)
        acc_ref[g] += pv
        carry_ref[g] = carries[g]


def _stick_kernel(q_ref, k_ref, vt_ref, tri_ref, o_ref, carry_ref, acc_ref, *, tq, tk, hg):
    for sub in range(ATT_QSUB):
        _stick_query_block(pl.program_id(1) * ATT_QSUB + sub, slice(sub * tq, (sub + 1) * tq),
                           q_ref, k_ref, vt_ref, tri_ref, o_ref, carry_ref, acc_ref,
                           tq=tq, tk=tk, hg=hg)


def _stick_query_block(qb, span, q_ref, k_ref, vt_ref, tri_ref, o_ref, carry_ref, acc_ref,
                       *, tq, tk, hg):
    qts = [q_ref[g * HEAD_W:(g + 1) * HEAD_W, span] for g in range(hg)]
    tri = tri_ref[...]
    carry_ref[...] = jnp.zeros(carry_ref.shape, F32)
    acc_ref[...] = jnp.zeros(acc_ref.shape, F32)
    kpos = lax.broadcasted_iota(jnp.int32, (tk, tq), 0)
    qpos = lax.broadcasted_iota(jnp.int32, (tk, tq), 1)
    causal = kpos < qpos
    state = (qts, k_ref, vt_ref, tri, carry_ref, acc_ref, tk)

    @pl.when(qb == 0)
    def _():
        _stick_blocks([(qb, causal)], *state)

    @pl.when(qb > 0)
    def _():
        _stick_blocks([(qb, causal), (qb - 1, None)], *state)

    def more(state_):
        i, cmax = state_
        return jnp.logical_and(i < qb, cmax > SB_LOG2W_UNDERFLOW)

    def earlier(state_):
        i, _ = state_
        _stick_blocks([(qb - 1 - i, None)], *state)
        return i + 1, jnp.max(carry_ref[...])

    lax.while_loop(more, earlier, (jnp.int32(1), jnp.max(carry_ref[...])))
    for g in range(hg):
        o_ref[span, g * HEAD_W:(g + 1) * HEAD_W] = acc_ref[g].T.astype(o_ref.dtype)


def _stick_call(qt, k, vt4, tri):
    s = k.shape[0]
    tq = tk = min(ATT_TQ, s)
    nkb = s // tk
    hg = SB_HEAD_GROUP
    return pl.pallas_call(
        functools.partial(_stick_kernel, tq=tq, tk=tk, hg=hg),
        grid=(SB_HEADS // hg, s // (tq * ATT_QSUB)),
        in_specs=[pl.BlockSpec((hg * HEAD_W, tq * ATT_QSUB), lambda h, i: (h, i)),
                  _resident_spec((s, hg * HEAD_W), lambda h, i: (0, h)),
                  _resident_spec((hg, nkb, HEAD_W, tk), lambda h, i: (h, 0, 0, 0)),
                  pl.BlockSpec((tk, tk), lambda h, i: (0, 0))],
        out_specs=pl.BlockSpec((tq * ATT_QSUB, hg * HEAD_W), lambda h, i: (i, h)),
        out_shape=jax.ShapeDtypeStruct((s, SB_HEADS * HEAD_W), BF),
        scratch_shapes=[pltpu.VMEM((hg, 1, tq), F32), pltpu.VMEM((hg, HEAD_W, tq), F32)],
        compiler_params=_cparams("arbitrary", "arbitrary"),
        name="stick_breaking",
    )(qt, k, vt4, tri)


def _merge_kernel(ht_ref, ya_ref, yb_ref, yc_ref, wga_ref, wgb_ref, wgc_ref, wb_ref, o_ref):
    ht = ht_ref[...]
    acc = None
    for n, (y_ref, wg_ref) in enumerate(((ya_ref, wga_ref), (yb_ref, wgb_ref), (yc_ref, wgc_ref))):
        gate = jax.nn.sigmoid(jnp.dot(wg_ref[...], ht, preferred_element_type=F32)).T
        up = jnp.dot(y_ref[...], wb_ref[n], preferred_element_type=F32)
        acc = gate * up if acc is None else acc + gate * up
    o_ref[...] = acc.astype(o_ref.dtype)


def _merge_call(ht, ya, yb, yc, wt_tail, wb, layer):
    d, s = ht.shape
    tm = min(512, s)
    tn = 512
    ysp = pl.BlockSpec((tm, BRANCH_WIDTH), lambda i, j: (i, 0))

    def gate_spec(n):
        blk0 = (SEG_GATE * SEG + n * d) // tn
        return pl.BlockSpec((None, tn, d), lambda i, j: (layer, blk0 + j, 0))

    return pl.pallas_call(
        _merge_kernel,
        grid=(s // tm, d // tn),
        in_specs=[pl.BlockSpec((d, tm), lambda i, j: (0, i)), ysp, ysp, ysp,
                  gate_spec(0), gate_spec(1), gate_spec(2),
                  pl.BlockSpec((None, N_BRANCHES, BRANCH_WIDTH, tn), lambda i, j: (layer, 0, 0, j))],
        out_specs=pl.BlockSpec((tm, tn), lambda i, j: (i, j)),
        out_shape=jax.ShapeDtypeStruct((s, d), BF),
        compiler_params=_cparams("arbitrary", "arbitrary"),
        name="gated_merge",
    )(ht, ya, yb, yc, wt_tail, wt_tail, wt_tail, wb)


def _mlp_kernel(x_ref, ng_ref, sc_ref, sh_ref, w1_ref, w2_ref, g_ref, o_ref, acc_ref, h_ref):
    f = pl.program_id(1)

    @pl.when(f == 0)
    def _():
        acc_ref[...] = jnp.zeros(acc_ref.shape, F32)
        x = x_ref[...]
        y = x * lax.rsqrt(jnp.mean(x * x, axis=-1, keepdims=True) + EPS) * ng_ref[...]
        h_ref[...] = (y * (1.0 + sc_ref[...]) + sh_ref[...]).astype(BF)

    u = jnp.dot(h_ref[...], w1_ref[...], preferred_element_type=F32)
    u = jnp.square(jnp.maximum(u, 0.0)).astype(BF)
    acc_ref[...] += jnp.dot(u, w2_ref[...], preferred_element_type=F32)

    @pl.when(f == pl.num_programs(1) - 1)
    def _():
        o_ref[...] = x_ref[...] + g_ref[...] * acc_ref[...]


def _mlp_call(x, norm_g, sc, sh, w1, w2, layer, g):
    s, d = x.shape
    ff = w1.shape[2]
    tm = min(512, s)
    tf = 1024
    row = pl.BlockSpec((1, d), lambda i, f: (0, 0))
    return pl.pallas_call(
        _mlp_kernel,
        grid=(s // tm, ff // tf),
        in_specs=[pl.BlockSpec((tm, d), lambda i, f: (i, 0)), row, row, row,
                  pl.BlockSpec((None, d, tf), lambda i, f: (layer, 0, f)),
                  pl.BlockSpec((None, tf, d), lambda i, f: (layer, f, 0)),
                  row],
        out_specs=pl.BlockSpec((tm, d), lambda i, f: (i, 0)),
        out_shape=jax.ShapeDtypeStruct((s, d), F32),
        scratch_shapes=[pltpu.VMEM((tm, d), F32), pltpu.VMEM((tm, d), BF)],
        compiler_params=_cparams("arbitrary", "arbitrary"),
        name="sqrelu_mlp",
    )(x, norm_g, sc, sh, w1, w2, g)


def _t5_bucket(rel):
    nb = T5_BUCKETS // 2
    max_exact = nb // 2
    n = jnp.abs(rel)
    large = max_exact + (jnp.log(jnp.maximum(n, 1).astype(F32) / max_exact)
                         / math.log(T5_MAX_DIST / max_exact) * (nb - max_exact)).astype(jnp.int32)
    large = jnp.minimum(large, nb - 1)
    return jnp.where(rel > 0, nb, 0) + jnp.where(n < max_exact, n, large)


def _bias_tiles(t5_bias, tq, tk):
    table = t5_bias.astype(F32)

    def lookup(bucket):
        hit = bucket[..., None, None] == jnp.arange(T5_BUCKETS)[:, None]
        return jnp.sum(jnp.where(hit, table, 0.0), axis=-2)

    kl = jnp.arange(tk)[:, None]
    ql = jnp.arange(tq)[None, :]
    far = lookup(_t5_bucket(jnp.array(-(tk + tq), jnp.int32)))
    diag = jnp.transpose(lookup(_t5_bucket(kl - ql)) - far, (2, 0, 1))
    prev = jnp.transpose(lookup(_t5_bucket(kl - tk - ql)) - far, (2, 0, 1))
    allowed = (kl // CHUNK) <= (ql // CHUNK)
    diag = jnp.where(allowed[None], diag * LOG2E, NEG_INF)
    near = jnp.stack([jnp.concatenate([prev * LOG2E, diag], axis=1),
                      jnp.concatenate([diag, jnp.full_like(diag, NEG_INF)], axis=1)], axis=1)
    return jnp.concatenate([near, near], axis=3)


def _rope_tables(s):
    half = MLA_ROPE // 2
    inv = ROPE_BASE ** (-jnp.arange(half, dtype=F32) / half)
    ang = jnp.arange(s).astype(F32)[:, None] * inv[None, :]
    return jnp.cos(ang).T, jnp.sin(ang).T


def kernel(x, c, w_ada, b_ada, norm_mix_g, norm_mlp_g, w_in, diff_qk_g, diff_lambda, diff_subln_g,
           t5_bias, mla_q_norm_g, mla_kv_norm_g, w_q_up, w_kv_up, mla_qk_g, w_branch, w_out,
           w_mlp_in, w_mlp_out):
    b, s, d = x.shape
    assert b == 1 and d == D_MODEL
    depth = w_ada.shape[0]
    tq = tk = min(ATT_TQ, s)
    hw = MLA_NOPE + MLA_ROPE

    mod = _mod_call(jnp.broadcast_to(c, (8, d)).astype(BF), w_ada, b_ada[:, None, :])[:, 0, :]
    mod = mod.reshape(depth, N_MOD, 1, d)

    wt_head = jnp.swapaxes(w_in[:, :, :HEAD_ROWS], 1, 2).astype(BF)
    wt_tail = jnp.swapaxes(w_in[:, :, TAIL_ROW0:], 1, 2).astype(BF)
    w_br = w_branch.astype(BF)
    w_o = w_out.astype(BF)
    w_1 = w_mlp_in.astype(BF)
    w_2 = w_mlp_out.astype(BF)
    wt_qu = jnp.swapaxes(w_q_up, 1, 2).astype(BF)
    wkv = w_kv_up.reshape(depth, MLA_KV_LORA, MLA_HEADS, MLA_NOPE + MLA_V)
    wt_kn = jnp.swapaxes(wkv[..., :MLA_NOPE].reshape(depth, MLA_KV_LORA, -1), 1, 2).astype(BF)
    wt_v = jnp.swapaxes(wkv[..., MLA_NOPE:].reshape(depth, MLA_KV_LORA, -1), 1, 2).astype(BF)

    bias_near = _bias_tiles(t5_bias, tq, tk)
    kl = jnp.arange(tk)[:, None]
    ql = jnp.arange(tq)[None, :]
    mask_diag = jnp.where((kl // CHUNK) <= (ql // CHUNK), 0.0, NEG_INF).astype(F32)
    tri = -(jnp.arange(tk)[None, :] > jnp.arange(tk)[:, None]).astype(BF)
    cos_t, sin_t = _rope_tables(s)

    ones_col = jnp.ones((SEG, 1), F32)
    sb_scale = jnp.full((SB_HEADS * SB_DIM, 1), SB_DIM ** -0.5 * LOG2E, F32)
    da_scale = DA_QK ** -0.5 * LOG2E
    mla_scale = hw ** -0.5 * LOG2E

    x2 = x[0]
    for l in range(depth):
        sh1, sc1, g1, sh2, sc2, g2 = (mod[l, i] for i in range(N_MOD))
        lam_init = 0.8 - 0.6 * math.exp(-0.3 * l)

        ht = _norm_call(x2, norm_mix_g[l][None], sc1, sh1)

        gq_rows = jnp.tile(diff_qk_g[l, 0], 2 * DA_HEADS)[:, None] * da_scale
        gk_rows = jnp.tile(diff_qk_g[l, 1], 2 * DA_HEADS)[:, None]
        qa, ka, va, cbt = _proj_call(
            wt_head, l, SEG_DQ, ht, jnp.concatenate([gq_rows, gk_rows, ones_col, ones_col]),
            [(DA_QK, "qpad", BF), (DA_QK, "n", BF), (0, "t4", BF), (0, "t", F32)], name="proj_head")
        qc, kc, vc = _proj_call(
            wt_tail, l, SEG_SQ, ht, jnp.concatenate([sb_scale, ones_col, ones_col]),
            [(0, "t", BF), (0, "n", BF), (0, "t4", BF)], name="proj_tail")
        gqk = mla_qk_g[l]
        qb_t, kb, vb_t = _mla_up_call(
            cbt, l, mla_q_norm_g[l][:, None], mla_kv_norm_g[l][:, None], wt_qu, wt_kn, wt_v,
            gqk[0, :MLA_NOPE, None] * mla_scale, gqk[0, MLA_NOPE:, None] * mla_scale,
            gqk[1, :MLA_NOPE, None], gqk[1, MLA_NOPE:, None], cos_t, sin_t)

        ya = _diff_attn_call(qa, ka, va, bias_near, diff_lambda[l], diff_subln_g[l][:, None],
                             lam_init=lam_init)
        yb = _mla_attn_call(qb_t, kb, vb_t, mask_diag)
        yc = _stick_call(qc, kc, vc, tri)

        merged = _merge_call(ht, ya, yb, yc, wt_tail, w_br, l)
        x2 = _resid_proj_call(merged, w_o, l, x2, g1)

        x2 = _mlp_call(x2, norm_mlp_g[l][None], sc2, sh2, w_1, w_2, l, g2)
    return x2[None]
```

```python
import functools
import math

import jax
import jax.numpy as jnp
from jax import lax
from jax.experimental import pallas as pl
from jax.experimental.pallas import tpu as pltpu

BF = jnp.bfloat16
F32 = jnp.float32

D_MODEL = 2048
CHUNK = 64
DA_HEADS = 8
DA_QK = 64
DA_V = 128
MLA_HEADS = 8
MLA_Q_LORA = 512
MLA_KV_LORA = 256
MLA_NOPE = 128
MLA_ROPE = 64
MLA_V = 128
MLA_QK_PAD = 256
ROPE_BASE = 10000.0
SB_HEADS = 8
SB_DIM = 128
BRANCH_WIDTH = 1024
N_BRANCHES = 3
T5_BUCKETS = 32
T5_MAX_DIST = 128
D_FF = 4 * D_MODEL
N_MOD = 6
EPS = 1e-6
NEG_INF = -1e30
LOG2E = 1.4426950408889634

HEAD_W = 128
ATT_TQ = 256
ATT_TK = 256
ATT_QSUB = 2
DIFF_FAR_BLOCKS = 4
MLA_FAR_BLOCKS = 8
DIFF_HEAD_GROUP = 4
MLA_HEAD_GROUP = 4
SB_HEAD_GROUP = 8
SB_LOG2W_UNDERFLOW = -152.0
V7X_VMEM_LIMIT = 56 * 1024 * 1024

SEG = 1024
HEAD_ROWS = 4 * SEG
TAIL_ROW0 = 3 * SEG + MLA_Q_LORA + MLA_KV_LORA + MLA_ROPE
SEG_DQ, SEG_DK, SEG_DV, SEG_LAT = 0, 1, 2, 3
SEG_SQ, SEG_SK, SEG_SV, SEG_GATE = 0, 1, 2, 3
LAT_KPE = MLA_Q_LORA + MLA_KV_LORA


def _cparams(*sem):
    return pltpu.CompilerParams(dimension_semantics=sem, vmem_limit_bytes=V7X_VMEM_LIMIT)


def _mod_kernel(c_ref, w_ref, b_ref, o_ref):
    w = w_ref[0].astype(BF)
    o_ref[0] = jnp.dot(c_ref[...], w, preferred_element_type=F32) + b_ref[0]


def _mod_call(c8, w_ada, b_ada3):
    depth, d, n = w_ada.shape
    tn = 1024
    return pl.pallas_call(
        _mod_kernel,
        grid=(depth, n // tn),
        in_specs=[pl.BlockSpec((8, d), lambda l, j: (0, 0)),
                  pl.BlockSpec((1, d, tn), lambda l, j: (l, 0, j)),
                  pl.BlockSpec((1, 1, tn), lambda l, j: (l, 0, j))],
        out_specs=pl.BlockSpec((1, 8, tn), lambda l, j: (l, 0, j)),
        out_shape=jax.ShapeDtypeStruct((depth, 8, n), F32),
        compiler_params=_cparams("arbitrary", "arbitrary"),
        name="adaln_mod",
    )(c8, w_ada, b_ada3)


def _norm_kernel(x_ref, g_ref, sc_ref, sh_ref, ht_ref):
    x = x_ref[...]
    ms = jnp.mean(x * x, axis=-1, keepdims=True)
    y = x * lax.rsqrt(ms + EPS) * g_ref[...]
    ht_ref[...] = (y * (1.0 + sc_ref[...]) + sh_ref[...]).T.astype(BF)


def _norm_call(x, g, sc, sh):
    s, d = x.shape
    tm = min(512, s)
    row = pl.BlockSpec((1, d), lambda i: (0, 0))
    return pl.pallas_call(
        _norm_kernel,
        grid=(s // tm,),
        in_specs=[pl.BlockSpec((tm, d), lambda i: (i, 0)), row, row, row],
        out_specs=pl.BlockSpec((d, tm), lambda i: (0, i)),
        out_shape=jax.ShapeDtypeStruct((d, s), BF),
        compiler_params=_cparams("arbitrary"),
        name="adaln_rmsnorm",
    )(x, g, sc, sh)


def _proj_tile(w_ref, xt_ref, rs_ref, o_ref, group, layout, tk):
    y = jnp.dot(w_ref[...], xt_ref[...], preferred_element_type=F32)
    tn, tm = y.shape
    if group:
        y3 = y.reshape(tn // group, group, tm)
        ms = jnp.mean(y3 * y3, axis=1, keepdims=True)
        y = (y3 * lax.rsqrt(ms + EPS)).reshape(tn, tm)
    y = y * rs_ref[...]
    if layout == "t":
        o_ref[...] = y.astype(o_ref.dtype)
    elif layout == "n":
        o_ref[...] = y.T.astype(o_ref.dtype)
    elif layout == "t4":
        for a in range(tn // HEAD_W):
            for b in range(tm // tk):
                o_ref[a, b] = y[a * HEAD_W:(a + 1) * HEAD_W, b * tk:(b + 1) * tk].astype(o_ref.dtype)
    elif layout == "qpad":
        first = lax.broadcasted_iota(jnp.int32, (HEAD_W, tm), 0) < DA_QK
        for a in range(tn // HEAD_W):
            ya = y[a * HEAD_W:(a + 1) * HEAD_W]
            o_ref[2 * a] = jnp.where(first, ya, 0.0).astype(o_ref.dtype)
            o_ref[2 * a + 1] = jnp.where(first, 0.0, ya).astype(o_ref.dtype)


def _proj_kernel(w_ref, xt_ref, rs_ref, *o_refs, specs, tk):
    seg = pl.program_id(0)
    for k, (group, layout, _) in enumerate(specs):
        @pl.when(seg == k)
        def _(k=k, group=group, layout=layout):
            _proj_tile(w_ref, xt_ref, rs_ref, o_refs[k], group, layout, tk)


def _proj_call(wt, layer, seg0, ht, rowscale, specs, *, name):
    k, s = ht.shape
    nseg = len(specs)
    tn = SEG
    tm = min(1024, s)
    nj = s // tm
    tk = min(ATT_TK, s)
    out_shapes, out_specs = [], []
    for seg, (_, layout, out_dtype) in enumerate(specs):
        def tile(i, j, seg=seg):
            return jnp.where(i < seg, 0, jnp.where(i > seg, nj - 1, j))

        if layout == "t":
            out_shapes.append(jax.ShapeDtypeStruct((tn, s), out_dtype))
            out_specs.append(pl.BlockSpec((tn, tm), lambda i, j, t=tile: (0, t(i, j))))
        elif layout == "n":
            out_shapes.append(jax.ShapeDtypeStruct((s, tn), out_dtype))
            out_specs.append(pl.BlockSpec((tm, tn), lambda i, j, t=tile: (t(i, j), 0)))
        elif layout == "t4":
            out_shapes.append(jax.ShapeDtypeStruct((tn // HEAD_W, s // tk, HEAD_W, tk), out_dtype))
            out_specs.append(pl.BlockSpec((tn // HEAD_W, tm // tk, HEAD_W, tk),
                                          lambda i, j, t=tile: (0, t(i, j), 0, 0)))
        else:
            out_shapes.append(jax.ShapeDtypeStruct((2 * tn // HEAD_W, HEAD_W, s), out_dtype))
            out_specs.append(pl.BlockSpec((2 * tn // HEAD_W, HEAD_W, tm),
                                          lambda i, j, t=tile: (0, 0, t(i, j))))
    return pl.pallas_call(
        functools.partial(_proj_kernel, specs=tuple(specs), tk=tk),
        grid=(nseg, nj),
        in_specs=[pl.BlockSpec((None, tn, k), lambda i, j: (layer, seg0 + i, 0)),
                  pl.BlockSpec((k, tm), lambda i, j: (0, j)),
                  pl.BlockSpec((tn, 1), lambda i, j: (i, 0))],
        out_specs=out_specs,
        out_shape=out_shapes,
        compiler_params=_cparams("arbitrary", "arbitrary"),
        name=name,
    )(wt, ht, rowscale)


def _resid_proj_kernel(a_ref, w_ref, x_ref, g_ref, o_ref):
    y = jnp.dot(a_ref[...], w_ref[...], preferred_element_type=F32)
    o_ref[...] = x_ref[...] + g_ref[...] * y


def _resid_proj_call(a, w, layer, x, g):
    s, k = a.shape
    n = w.shape[2]
    tm = min(1024, s)
    tn = min(1024, n)
    return pl.pallas_call(
        _resid_proj_kernel,
        grid=(s // tm, n // tn),
        in_specs=[pl.BlockSpec((tm, k), lambda i, j: (i, 0)),
                  pl.BlockSpec((None, k, tn), lambda i, j: (layer, 0, j)),
                  pl.BlockSpec((tm, tn), lambda i, j: (i, j)),
                  pl.BlockSpec((1, tn), lambda i, j: (0, j))],
        out_specs=pl.BlockSpec((tm, tn), lambda i, j: (i, j)),
        out_shape=jax.ShapeDtypeStruct((s, n), F32),
        compiler_params=_cparams("arbitrary", "arbitrary"),
        name="out_proj_residual",
    )(a, w, x, g)


def _rms_rows(x, g):
    return x * lax.rsqrt(jnp.mean(x * x, axis=0, keepdims=True) + EPS) * g


def _rope_rows(x, cos_t, sin_t):
    half = MLA_ROPE // 2
    x1, x2 = x[:half], x[half:]
    return x1 * cos_t - x2 * sin_t, x1 * sin_t + x2 * cos_t


def _mla_up_kernel(cb_ref, gq_ref, gkv_ref, wq_ref, wk_ref, wv_ref, gqn_ref, gqp_ref, gkn_ref,
                   gkp_ref, cost_ref, sint_ref, qt_ref, k_ref, vt_ref, *, tk):
    tm = cb_ref.shape[1]
    hw = MLA_NOPE + MLA_ROPE
    cos_t = cost_ref[...]
    sin_t = sint_ref[...]
    cqn = _rms_rows(cb_ref[0:MLA_Q_LORA], gq_ref[...]).astype(BF)
    qt = jnp.dot(wq_ref[...], cqn, preferred_element_type=F32)
    qzero = jnp.zeros((MLA_QK_PAD - hw, tm), BF)
    for h in range(MLA_HEADS):
        nope = _rms_rows(qt[h * hw:h * hw + MLA_NOPE], gqn_ref[...])
        pe1, pe2 = _rope_rows(_rms_rows(qt[h * hw + MLA_NOPE:(h + 1) * hw], gqp_ref[...]), cos_t, sin_t)
        qt_ref[h, 0:MLA_NOPE] = nope.astype(BF)
        qt_ref[h, MLA_NOPE:MLA_NOPE + MLA_ROPE // 2] = pe1.astype(BF)
        qt_ref[h, MLA_NOPE + MLA_ROPE // 2:hw] = pe2.astype(BF)
        qt_ref[h, hw:MLA_QK_PAD] = qzero
    ckvn = _rms_rows(cb_ref[MLA_Q_LORA:LAT_KPE], gkv_ref[...]).astype(BF)
    kt = jnp.dot(wk_ref[...], ckvn, preferred_element_type=F32)
    vt = jnp.dot(wv_ref[...], ckvn, preferred_element_type=F32)
    kpe1, kpe2 = _rope_rows(_rms_rows(cb_ref[LAT_KPE:LAT_KPE + MLA_ROPE], gkp_ref[...]), cos_t, sin_t)
    kzero = jnp.zeros((MLA_QK_PAD - hw, tm), F32)
    for h in range(MLA_HEADS):
        kn = _rms_rows(kt[h * MLA_NOPE:(h + 1) * MLA_NOPE], gkn_ref[...])
        k_ref[h] = jnp.concatenate([kn, kpe1, kpe2, kzero], axis=0).T.astype(BF)
        for b in range(tm // tk):
            vt_ref[h, b] = vt[h * MLA_V:(h + 1) * MLA_V, b * tk:(b + 1) * tk].astype(BF)


def _mla_up_call(cbt, layer, gq, gkv, wq, wk, wv, gqn, gqp, gkn, gkp, cos_t, sin_t):
    s = cbt.shape[1]
    tm = min(512, s)
    tk = min(ATT_TK, s)
    hw = MLA_NOPE + MLA_ROPE
    half = MLA_ROPE // 2

    def const(shape):
        return pl.BlockSpec(shape, lambda i: tuple(0 for _ in shape))

    def layer_w(shape):
        return pl.BlockSpec((None,) + shape, lambda i: (layer, 0, 0))

    return pl.pallas_call(
        functools.partial(_mla_up_kernel, tk=tk),
        grid=(s // tm,),
        in_specs=[pl.BlockSpec((cbt.shape[0], tm), lambda i: (0, i)),
                  const((MLA_Q_LORA, 1)), const((MLA_KV_LORA, 1)),
                  layer_w((MLA_HEADS * hw, MLA_Q_LORA)),
                  layer_w((MLA_HEADS * MLA_NOPE, MLA_KV_LORA)),
                  layer_w((MLA_HEADS * MLA_V, MLA_KV_LORA)),
                  const((MLA_NOPE, 1)), const((MLA_ROPE, 1)), const((MLA_NOPE, 1)),
                  const((MLA_ROPE, 1)),
                  pl.BlockSpec((half, tm), lambda i: (0, i)),
                  pl.BlockSpec((half, tm), lambda i: (0, i))],
        out_specs=[pl.BlockSpec((MLA_HEADS, MLA_QK_PAD, tm), lambda i: (0, 0, i)),
                   pl.BlockSpec((MLA_HEADS, tm, MLA_QK_PAD), lambda i: (0, i, 0)),
                   pl.BlockSpec((MLA_HEADS, tm // tk, MLA_V, tk), lambda i: (0, i, 0, 0))],
        out_shape=[jax.ShapeDtypeStruct((MLA_HEADS, MLA_QK_PAD, s), BF),
                   jax.ShapeDtypeStruct((MLA_HEADS, s, MLA_QK_PAD), BF),
                   jax.ShapeDtypeStruct((MLA_HEADS, s // tk, MLA_V, tk), BF)],
        compiler_params=_cparams("arbitrary"),
        name="mla_up",
    )(cbt, gq, gkv, wq, wk, wv, gqn, gqp, gkn, gkp, cos_t, sin_t)


def _logits_phase(kb, nblk, qts, k_rows, tk):
    rows = pl.ds(pl.multiple_of(kb * tk, tk), nblk * tk)
    return [jnp.dot(k_rows(g, rows), qts[g], preferred_element_type=F32) for g in range(len(qts))]


def _pv_phase(kb, nblk, ps, vt_ref, tk):
    pvs = []
    for g, p in enumerate(ps):
        pv = jnp.dot(vt_ref[g, kb], p[0:tk], preferred_element_type=F32)
        for j in range(1, nblk):
            pv += jnp.dot(vt_ref[g, kb + j], p[j * tk:(j + 1) * tk], preferred_element_type=F32)
        pvs.append(pv)
    return pvs


def _softmax_block(kb, nblk, biases, qts, k_rows, vt_ref, m_ref, l_ref, acc_ref, tk):
    hg = len(qts)
    ss = _logits_phase(kb, nblk, qts, k_rows, tk)
    if biases is not None:
        ss = [s + b for s, b in zip(ss, biases)]
    alphas, ps = [], []
    for g in range(hg):
        m_old = m_ref[g]
        m_new = jnp.maximum(m_old, jnp.max(ss[g], axis=0, keepdims=True))
        alpha = jnp.exp2(m_old - m_new)
        p = jnp.exp2(ss[g] - m_new)
        l_ref[g] = alpha * l_ref[g] + jnp.sum(p, axis=0, keepdims=True)
        m_ref[g] = m_new
        alphas.append(alpha)
        ps.append(p.astype(BF))
    pvs = _pv_phase(kb, nblk, ps, vt_ref, tk)
    for g in range(hg):
        acc_ref[g] = alphas[g] * acc_ref[g] + pvs[g]


def _fixed_ref_block(kb, nblk, qts, k_rows, vt_ref, m_ref, l_ref, acc_ref, tk):
    hg = len(qts)
    ss = _logits_phase(kb, nblk, qts, k_rows, tk)
    ps = []
    for g in range(hg):
        p = jnp.exp2(ss[g] - m_ref[g])
        l_ref[g] += jnp.sum(p, axis=0, keepdims=True)
        ps.append(p.astype(BF))
    pvs = _pv_phase(kb, nblk, ps, vt_ref, tk)
    for g in range(hg):
        acc_ref[g] += pvs[g]


def _far_sweep(n_far, far_blocks, step):
    n_big = n_far // far_blocks

    def big(i, carry):
        step(i * far_blocks, far_blocks)
        return carry

    lax.fori_loop(0, n_big, big, 0)
    done = n_big * far_blocks
    size = far_blocks // 2
    while size >= 1:
        take = ((n_far - done) // size) * size

        @pl.when(take > 0)
        def _(done=done, size=size):
            step(done, size)

        done = done + take
        size //= 2


def _overflowed(l_ref, acc_ref):
    bad = jnp.maximum(jnp.max(jnp.where(jnp.isfinite(l_ref[...]), 0.0, 1.0)),
                      jnp.max(jnp.where(jnp.isfinite(acc_ref[...]), 0.0, 1.0)))
    return bad > 0.0


def _attention_sweep(n_far, far_blocks, near_step, far_std, far_fixed, m_ref, l_ref, acc_ref):
    _init_softmax_state(m_ref, l_ref, acc_ref)
    near_step()
    _far_sweep(n_far, far_blocks, far_fixed)

    @pl.when(_overflowed(l_ref, acc_ref))
    def _():
        _init_softmax_state(m_ref, l_ref, acc_ref)
        near_step()

        def one(kb, carry):
            far_std(kb)
            return carry

        lax.fori_loop(0, n_far, one, 0)


def _resident_spec(shape, index_map):
    return pl.BlockSpec(shape, index_map, pipeline_mode=pl.Buffered(1))


def _init_softmax_state(m_ref, l_ref, acc_ref):
    m_ref[...] = jnp.full(m_ref.shape, NEG_INF, F32)
    l_ref[...] = jnp.zeros(l_ref.shape, F32)
    acc_ref[...] = jnp.zeros(acc_ref.shape, F32)


def _diff_attn_kernel(q_ref, k_ref, vt_ref, bn_ref, lam_ref, g_ref, o_ref,
                      m_ref, l_ref, acc_ref, *, tq, tk, hg, lam_init):
    for sub in range(ATT_QSUB):
        _diff_query_block(pl.program_id(1) * ATT_QSUB + sub, slice(sub * tq, (sub + 1) * tq),
                          q_ref, k_ref, vt_ref, bn_ref, lam_ref, g_ref, o_ref, m_ref, l_ref, acc_ref,
                          tq=tq, tk=tk, hg=hg, lam_init=lam_init)


def _diff_query_block(qb, span, q_ref, k_ref, vt_ref, bn_ref, lam_ref, g_ref, o_ref,
                      m_ref, l_ref, acc_ref, *, tq, tk, hg, lam_init):
    qts = [jnp.concatenate([q_ref[2 * g, :, span], q_ref[2 * g + 1, :, span]], axis=1)
           for g in range(hg)]

    def k_rows(g, rows):
        return k_ref[rows, g * HEAD_W:(g + 1) * HEAD_W]

    state = (vt_ref, m_ref, l_ref, acc_ref, tk)
    first = (qb == 0).astype(jnp.int32)

    def near_step():
        _softmax_block(qb - 1 + first, 2, [bn_ref[g, first] for g in range(hg)], qts, k_rows, *state)

    _attention_sweep(
        jnp.maximum(qb - 1, 0), DIFF_FAR_BLOCKS, near_step,
        lambda kb: _softmax_block(kb, 1, None, qts, k_rows, *state),
        lambda kb, nblk: _fixed_ref_block(kb, nblk, qts, k_rows, *state),
        m_ref, l_ref, acc_ref)

    lp = lam_ref[...]
    lam = (jnp.exp(jnp.sum(lp[0:1] * lp[1:2], axis=1, keepdims=True))
           - jnp.exp(jnp.sum(lp[2:3] * lp[3:4], axis=1, keepdims=True)) + lam_init)
    for g in range(hg):
        o = acc_ref[g] / l_ref[g]
        d = o[:, :tq] - lam * o[:, tq:]
        ms = jnp.mean(d * d, axis=0, keepdims=True)
        d = d * lax.rsqrt(ms + EPS) * g_ref[...] * (1.0 - lam_init)
        o_ref[span, g * HEAD_W:(g + 1) * HEAD_W] = d.T.astype(o_ref.dtype)


def _diff_attn_call(qpad, k, vt4, bias_near, lam_p, sub_g, *, lam_init):
    s = k.shape[0]
    tq = tk = min(ATT_TQ, s)
    nkb = s // tk
    n = 2 * tq
    hg = DIFF_HEAD_GROUP
    return pl.pallas_call(
        functools.partial(_diff_attn_kernel, tq=tq, tk=tk, hg=hg, lam_init=lam_init),
        grid=(DA_HEADS // hg, s // (tq * ATT_QSUB)),
        in_specs=[pl.BlockSpec((2 * hg, HEAD_W, tq * ATT_QSUB), lambda h, i: (h, 0, i)),
                  _resident_spec((s, hg * HEAD_W), lambda h, i: (0, h)),
                  _resident_spec((hg, nkb, HEAD_W, tk), lambda h, i: (h, 0, 0, 0)),
                  _resident_spec((hg, 2, 2 * tk, n), lambda h, i: (h, 0, 0, 0)),
                  pl.BlockSpec((4, DA_QK), lambda h, i: (0, 0)),
                  pl.BlockSpec((HEAD_W, 1), lambda h, i: (0, 0))],
        out_specs=pl.BlockSpec((tq * ATT_QSUB, hg * HEAD_W), lambda h, i: (i, h)),
        out_shape=jax.ShapeDtypeStruct((s, DA_HEADS * HEAD_W), BF),
        scratch_shapes=[pltpu.VMEM((hg, 1, n), F32), pltpu.VMEM((hg, 1, n), F32),
                        pltpu.VMEM((hg, HEAD_W, n), F32)],
        compiler_params=_cparams("arbitrary", "arbitrary"),
        name="diff_attention",
    )(qpad, k, vt4, bias_near, lam_p, sub_g)


def _mla_attn_kernel(q_ref, k_ref, vt_ref, mask_ref, o_ref, m_ref, l_ref, acc_ref, *, tk, hg):
    for sub in range(ATT_QSUB):
        _mla_query_block(pl.program_id(1) * ATT_QSUB + sub, slice(sub * tk, (sub + 1) * tk),
                         q_ref, k_ref, vt_ref, mask_ref, o_ref, m_ref, l_ref, acc_ref, tk=tk, hg=hg)


def _mla_query_block(qb, span, q_ref, k_ref, vt_ref, mask_ref, o_ref, m_ref, l_ref, acc_ref, *, tk, hg):
    qts = [q_ref[g, :, span] for g in range(hg)]

    def k_rows(g, rows):
        return k_ref[g, rows, :]

    state = (vt_ref, m_ref, l_ref, acc_ref, tk)

    def near_step():
        _softmax_block(qb, 1, [mask_ref[...]] * hg, qts, k_rows, *state)

    _attention_sweep(
        qb, MLA_FAR_BLOCKS, near_step,
        lambda kb: _softmax_block(kb, 1, None, qts, k_rows, *state),
        lambda kb, nblk: _fixed_ref_block(kb, nblk, qts, k_rows, *state),
        m_ref, l_ref, acc_ref)
    for g in range(hg):
        o = acc_ref[g] / l_ref[g]
        o_ref[span, g * HEAD_W:(g + 1) * HEAD_W] = o.T.astype(o_ref.dtype)


def _mla_attn_call(qt, k, vt4, mask_diag):
    s = k.shape[1]
    tq = tk = min(ATT_TQ, s)
    nkb = s // tk
    hg = MLA_HEAD_GROUP
    return pl.pallas_call(
        functools.partial(_mla_attn_kernel, tk=tk, hg=hg),
        grid=(MLA_HEADS // hg, s // (tq * ATT_QSUB)),
        in_specs=[pl.BlockSpec((hg, MLA_QK_PAD, tq * ATT_QSUB), lambda h, i: (h, 0, i)),
                  _resident_spec((hg, s, MLA_QK_PAD), lambda h, i: (h, 0, 0)),
                  _resident_spec((hg, nkb, HEAD_W, tk), lambda h, i: (h, 0, 0, 0)),
                  pl.BlockSpec((tk, tq), lambda h, i: (0, 0))],
        out_specs=pl.BlockSpec((tq * ATT_QSUB, hg * HEAD_W), lambda h, i: (i, h)),
        out_shape=jax.ShapeDtypeStruct((s, MLA_HEADS * HEAD_W), BF),
        scratch_shapes=[pltpu.VMEM((hg, 1, tq), F32), pltpu.VMEM((hg, 1, tq), F32),
                        pltpu.VMEM((hg, HEAD_W, tq), F32)],
        compiler_params=_cparams("arbitrary", "arbitrary"),
        name="mla_attention",
    )(qt, k, vt4, mask_diag)


def _stick_blocks(blocks, qts, k_ref, vt_ref, tri_neg, carry_ref, acc_ref, tk):
    hg = len(qts)
    zs = []
    for kb, _ in blocks:
        rows = pl.ds(pl.multiple_of(kb * tk, tk), tk)
        zs.append([jnp.dot(k_ref[rows, g * HEAD_W:(g + 1) * HEAD_W], qts[g],
                           preferred_element_type=F32) for g in range(hg)])
    carries = [carry_ref[g] for g in range(hg)]
    logsigs, his, los = [], [], []
    for b, (_, mask) in enumerate(blocks):
        for g in range(hg):
            z = zs[b][g]
            sp = jnp.maximum(z, 0.0) + jnp.log2(1.0 + jnp.exp2(-jnp.abs(z)))
            logsigs.append(z - sp + carries[g])
            if mask is not None:
                sp = jnp.where(mask, sp, 0.0)
            hi = sp.astype(BF)
            his.append(hi)
            los.append((sp - hi.astype(F32)).astype(BF))
            carries[g] = carries[g] - jnp.sum(sp, axis=0, keepdims=True)
    betweens = [jnp.dot(tri_neg, hi, preferred_element_type=F32)
                + jnp.dot(tri_neg, lo, preferred_element_type=F32) for hi, lo in zip(his, los)]
    ws = []
    for b, (_, mask) in enumerate(blocks):
        for g in range(hg):
            w = jnp.exp2(logsigs[b * hg + g] + betweens[b * hg + g])
            if mask is not None:
                w = jnp.where(mask, w, 0.0)
            ws.append(w.astype(BF))
    for g in range(hg):
        pv = jnp.dot(vt_ref[g, blocks[0][0]], ws[g], preferred_element_type=F32)
        for b in range(1, len(blocks)):
            pv += jnp.dot(vt_ref[g, blocks[b][0]], ws[b * hg + g], preferred_element_type=F32)
        acc_ref[g] += pv
        carry_ref[g] = carries[g]


def _stick_kernel(q_ref, k_ref, vt_ref, tri_ref, o_ref, carry_ref, acc_ref, *, tq, tk, hg):
    for sub in range(ATT_QSUB):
        _stick_query_block(pl.program_id(1) * ATT_QSUB + sub, slice(sub * tq, (sub + 1) * tq),
                           q_ref, k_ref, vt_ref, tri_ref, o_ref, carry_ref, acc_ref,
                           tq=tq, tk=tk, hg=hg)


def _stick_query_block(qb, span, q_ref, k_ref, vt_ref, tri_ref, o_ref, carry_ref, acc_ref,
                       *, tq, tk, hg):
    qts = [q_ref[g * HEAD_W:(g + 1) * HEAD_W, span] for g in range(hg)]
    tri = tri_ref[...]
    carry_ref[...] = jnp.zeros(carry_ref.shape, F32)
    acc_ref[...] = jnp.zeros(acc_ref.shape, F32)
    kpos = lax.broadcasted_iota(jnp.int32, (tk, tq), 0)
    qpos = lax.broadcasted_iota(jnp.int32, (tk, tq), 1)
    causal = kpos < qpos
    state = (qts, k_ref, vt_ref, tri, carry_ref, acc_ref, tk)

    @pl.when(qb == 0)
    def _():
        _stick_blocks([(qb, causal)], *state)

    @pl.when(qb > 0)
    def _():
        _stick_blocks([(qb, causal), (qb - 1, None)], *state)

    def more(state_):
        i, cmax = state_
        return jnp.logical_and(i < qb, cmax > SB_LOG2W_UNDERFLOW)

    def earlier(state_):
        i, _ = state_
        _stick_blocks([(qb - 1 - i, None)], *state)
        return i + 1, jnp.max(carry_ref[...])

    lax.while_loop(more, earlier, (jnp.int32(1), jnp.max(carry_ref[...])))
    for g in range(hg):
        o_ref[span, g * HEAD_W:(g + 1) * HEAD_W] = acc_ref[g].T.astype(o_ref.dtype)


def _stick_call(qt, k, vt4, tri):
    s = k.shape[0]
    tq = tk = min(ATT_TQ, s)
    nkb = s // tk
    hg = SB_HEAD_GROUP
    return pl.pallas_call(
        functools.partial(_stick_kernel, tq=tq, tk=tk, hg=hg),
        grid=(SB_HEADS // hg, s // (tq * ATT_QSUB)),
        in_specs=[pl.BlockSpec((hg * HEAD_W, tq * ATT_QSUB), lambda h, i: (h, i)),
                  _resident_spec((s, hg * HEAD_W), lambda h, i: (0, h)),
                  _resident_spec((hg, nkb, HEAD_W, tk), lambda h, i: (h, 0, 0, 0)),
                  pl.BlockSpec((tk, tk), lambda h, i: (0, 0))],
        out_specs=pl.BlockSpec((tq * ATT_QSUB, hg * HEAD_W), lambda h, i: (i, h)),
        out_shape=jax.ShapeDtypeStruct((s, SB_HEADS * HEAD_W), BF),
        scratch_shapes=[pltpu.VMEM((hg, 1, tq), F32), pltpu.VMEM((hg, HEAD_W, tq), F32)],
        compiler_params=_cparams("arbitrary", "arbitrary"),
        name="stick_breaking",
    )(qt, k, vt4, tri)


def _merge_kernel(ht_ref, ya_ref, yb_ref, yc_ref, wga_ref, wgb_ref, wgc_ref, wb_ref, o_ref):
    ht = ht_ref[...]
    acc = None
    for n, (y_ref, wg_ref) in enumerate(((ya_ref, wga_ref), (yb_ref, wgb_ref), (yc_ref, wgc_ref))):
        gate = jax.nn.sigmoid(jnp.dot(wg_ref[...], ht, preferred_element_type=F32)).T
        up = jnp.dot(y_ref[...], wb_ref[n], preferred_element_type=F32)
        acc = gate * up if acc is None else acc + gate * up
    o_ref[...] = acc.astype(o_ref.dtype)


def _merge_call(ht, ya, yb, yc, wt_tail, wb, layer):
    d, s = ht.shape
    tm = min(1024, s)
    tn = 512
    ysp = pl.BlockSpec((tm, BRANCH_WIDTH), lambda i, j: (i, 0))

    def gate_spec(n):
        blk0 = (SEG_GATE * SEG + n * d) // tn
        return pl.BlockSpec((None, tn, d), lambda i, j: (layer, blk0 + j, 0))

    return pl.pallas_call(
        _merge_kernel,
        grid=(s // tm, d // tn),
        in_specs=[pl.BlockSpec((d, tm), lambda i, j: (0, i)), ysp, ysp, ysp,
                  gate_spec(0), gate_spec(1), gate_spec(2),
                  pl.BlockSpec((None, N_BRANCHES, BRANCH_WIDTH, tn), lambda i, j: (layer, 0, 0, j))],
        out_specs=pl.BlockSpec((tm, tn), lambda i, j: (i, j)),
        out_shape=jax.ShapeDtypeStruct((s, d), BF),
        compiler_params=_cparams("arbitrary", "arbitrary"),
        name="gated_merge",
    )(ht, ya, yb, yc, wt_tail, wt_tail, wt_tail, wb)


def _mlp_kernel(x_ref, ng_ref, sc_ref, sh_ref, w1_ref, w2_ref, g_ref, o_ref, acc_ref, h_ref):
    f = pl.program_id(1)

    @pl.when(f == 0)
    def _():
        acc_ref[...] = jnp.zeros(acc_ref.shape, F32)
        x = x_ref[...]
        y = x * lax.rsqrt(jnp.mean(x * x, axis=-1, keepdims=True) + EPS) * ng_ref[...]
        h_ref[...] = (y * (1.0 + sc_ref[...]) + sh_ref[...]).astype(BF)

    u = jnp.dot(h_ref[...], w1_ref[...], preferred_element_type=F32)
    u = jnp.square(jnp.maximum(u, 0.0)).astype(BF)
    acc_ref[...] += jnp.dot(u, w2_ref[...], preferred_element_type=F32)

    @pl.when(f == pl.num_programs(1) - 1)
    def _():
        o_ref[...] = x_ref[...] + g_ref[...] * acc_ref[...]


def _mlp_call(x, norm_g, sc, sh, w1, w2, layer, g):
    s, d = x.shape
    ff = w1.shape[2]
    tm = min(512, s)
    tf = 1024
    row = pl.BlockSpec((1, d), lambda i, f: (0, 0))
    return pl.pallas_call(
        _mlp_kernel,
        grid=(s // tm, ff // tf),
        in_specs=[pl.BlockSpec((tm, d), lambda i, f: (i, 0)), row, row, row,
                  pl.BlockSpec((None, d, tf), lambda i, f: (layer, 0, f)),
                  pl.BlockSpec((None, tf, d), lambda i, f: (layer, f, 0)),
                  row],
        out_specs=pl.BlockSpec((tm, d), lambda i, f: (i, 0)),
        out_shape=jax.ShapeDtypeStruct((s, d), F32),
        scratch_shapes=[pltpu.VMEM((tm, d), F32), pltpu.VMEM((tm, d), BF)],
        compiler_params=_cparams("arbitrary", "arbitrary"),
        name="sqrelu_mlp",
    )(x, norm_g, sc, sh, w1, w2, g)


def _t5_bucket(rel):
    nb = T5_BUCKETS // 2
    max_exact = nb // 2
    n = jnp.abs(rel)
    large = max_exact + (jnp.log(jnp.maximum(n, 1).astype(F32) / max_exact)
                         / math.log(T5_MAX_DIST / max_exact) * (nb - max_exact)).astype(jnp.int32)
    large = jnp.minimum(large, nb - 1)
    return jnp.where(rel > 0, nb, 0) + jnp.where(n < max_exact, n, large)


def _bias_tiles(t5_bias, tq, tk):
    table = t5_bias.astype(F32)

    def lookup(bucket):
        hit = bucket[..., None, None] == jnp.arange(T5_BUCKETS)[:, None]
        return jnp.sum(jnp.where(hit, table, 0.0), axis=-2)

    kl = jnp.arange(tk)[:, None]
    ql = jnp.arange(tq)[None, :]
    far = lookup(_t5_bucket(jnp.array(-(tk + tq), jnp.int32)))
    diag = jnp.transpose(lookup(_t5_bucket(kl - ql)) - far, (2, 0, 1))
    prev = jnp.transpose(lookup(_t5_bucket(kl - tk - ql)) - far, (2, 0, 1))
    allowed = (kl // CHUNK) <= (ql // CHUNK)
    diag = jnp.where(allowed[None], diag * LOG2E, NEG_INF)
    near = jnp.stack([jnp.concatenate([prev * LOG2E, diag], axis=1),
                      jnp.concatenate([diag, jnp.full_like(diag, NEG_INF)], axis=1)], axis=1)
    return jnp.concatenate([near, near], axis=3)


def _rope_tables(s):
    half = MLA_ROPE // 2
    inv = ROPE_BASE ** (-jnp.arange(half, dtype=F32) / half)
    ang = jnp.arange(s).astype(F32)[:, None] * inv[None, :]
    return jnp.cos(ang).T, jnp.sin(ang).T


def kernel(x, c, w_ada, b_ada, norm_mix_g, norm_mlp_g, w_in, diff_qk_g, diff_lambda, diff_subln_g,
           t5_bias, mla_q_norm_g, mla_kv_norm_g, w_q_up, w_kv_up, mla_qk_g, w_branch, w_out,
           w_mlp_in, w_mlp_out):
    b, s, d = x.shape
    assert b == 1 and d == D_MODEL
    depth = w_ada.shape[0]
    tq = tk = min(ATT_TQ, s)
    hw = MLA_NOPE + MLA_ROPE

    mod = _mod_call(jnp.broadcast_to(c, (8, d)).astype(BF), w_ada, b_ada[:, None, :])[:, 0, :]
    mod = mod.reshape(depth, N_MOD, 1, d)

    wt_head = jnp.swapaxes(w_in[:, :, :HEAD_ROWS], 1, 2).astype(BF)
    wt_tail = jnp.swapaxes(w_in[:, :, TAIL_ROW0:], 1, 2).astype(BF)
    w_br = w_branch.astype(BF)
    w_o = w_out.astype(BF)
    w_1 = w_mlp_in.astype(BF)
    w_2 = w_mlp_out.astype(BF)
    wt_qu = jnp.swapaxes(w_q_up, 1, 2).astype(BF)
    wkv = w_kv_up.reshape(depth, MLA_KV_LORA, MLA_HEADS, MLA_NOPE + MLA_V)
    wt_kn = jnp.swapaxes(wkv[..., :MLA_NOPE].reshape(depth, MLA_KV_LORA, -1), 1, 2).astype(BF)
    wt_v = jnp.swapaxes(wkv[..., MLA_NOPE:].reshape(depth, MLA_KV_LORA, -1), 1, 2).astype(BF)

    bias_near = _bias_tiles(t5_bias, tq, tk)
    kl = jnp.arange(tk)[:, None]
    ql = jnp.arange(tq)[None, :]
    mask_diag = jnp.where((kl // CHUNK) <= (ql // CHUNK), 0.0, NEG_INF).astype(F32)
    tri = -(jnp.arange(tk)[None, :] > jnp.arange(tk)[:, None]).astype(BF)
    cos_t, sin_t = _rope_tables(s)

    ones_col = jnp.ones((SEG, 1), F32)
    sb_scale = jnp.full((SB_HEADS * SB_DIM, 1), SB_DIM ** -0.5 * LOG2E, F32)
    da_scale = DA_QK ** -0.5 * LOG2E
    mla_scale = hw ** -0.5 * LOG2E

    x2 = x[0]
    for l in range(depth):
        sh1, sc1, g1, sh2, sc2, g2 = (mod[l, i] for i in range(N_MOD))
        lam_init = 0.8 - 0.6 * math.exp(-0.3 * l)

        ht = _norm_call(x2, norm_mix_g[l][None], sc1, sh1)

        gq_rows = jnp.tile(diff_qk_g[l, 0], 2 * DA_HEADS)[:, None] * da_scale
        gk_rows = jnp.tile(diff_qk_g[l, 1], 2 * DA_HEADS)[:, None]
        qa, ka, va, cbt = _proj_call(
            wt_head, l, SEG_DQ, ht, jnp.concatenate([gq_rows, gk_rows, ones_col, ones_col]),
            [(DA_QK, "qpad", BF), (DA_QK, "n", BF), (0, "t4", BF), (0, "t", F32)], name="proj_head")
        qc, kc, vc = _proj_call(
            wt_tail, l, SEG_SQ, ht, jnp.concatenate([sb_scale, ones_col, ones_col]),
            [(0, "t", BF), (0, "n", BF), (0, "t4", BF)], name="proj_tail")
        gqk = mla_qk_g[l]
        qb_t, kb, vb_t = _mla_up_call(
            cbt, l, mla_q_norm_g[l][:, None], mla_kv_norm_g[l][:, None], wt_qu, wt_kn, wt_v,
            gqk[0, :MLA_NOPE, None] * mla_scale, gqk[0, MLA_NOPE:, None] * mla_scale,
            gqk[1, :MLA_NOPE, None], gqk[1, MLA_NOPE:, None], cos_t, sin_t)

        ya = _diff_attn_call(qa, ka, va, bias_near, diff_lambda[l], diff_subln_g[l][:, None],
                             lam_init=lam_init)
        yb = _mla_attn_call(qb_t, kb, vb_t, mask_diag)
        yc = _stick_call(qc, kc, vc, tri)

        merged = _merge_call(ht, ya, yb, yc, wt_tail, w_br, l)
        x2 = _resid_proj_call(merged, w_o, l, x2, g1)

        x2 = _mlp_call(x2, norm_mlp_g[l][None], sc2, sh2, w_1, w_2, l, g2)
    return x2[None]
```

```python
import functools
import math

import jax
import jax.numpy as jnp
from jax import lax
from jax.experimental import pallas as pl
from jax.experimental.pallas import tpu as pltpu

BF = jnp.bfloat16
F32 = jnp.float32

D_MODEL = 2048
CHUNK = 64
DA_HEADS = 8
DA_QK = 64
DA_V = 128
MLA_HEADS = 8
MLA_Q_LORA = 512
MLA_KV_LORA = 256
MLA_NOPE = 128
MLA_ROPE = 64
MLA_V = 128
MLA_QK_PAD = 256
ROPE_BASE = 10000.0
SB_HEADS = 8
SB_DIM = 128
BRANCH_WIDTH = 1024
N_BRANCHES = 3
T5_BUCKETS = 32
T5_MAX_DIST = 128
D_FF = 4 * D_MODEL
N_MOD = 6
EPS = 1e-6
NEG_INF = -1e30
LOG2E = 1.4426950408889634

HEAD_W = 128
ATT_TQ = 256
ATT_TK = 256
ATT_QSUB = 2
DIFF_FAR_BLOCKS = 8
MLA_FAR_BLOCKS = 8
DIFF_HEAD_GROUP = 4
MLA_HEAD_GROUP = 4
SB_HEAD_GROUP = 8
SB_LOG2W_UNDERFLOW = -152.0
V7X_VMEM_LIMIT = 56 * 1024 * 1024

SEG = 1024
HEAD_ROWS = 4 * SEG
TAIL_ROW0 = 3 * SEG + MLA_Q_LORA + MLA_KV_LORA + MLA_ROPE
SEG_DQ, SEG_DK, SEG_DV, SEG_LAT = 0, 1, 2, 3
SEG_SQ, SEG_SK, SEG_SV, SEG_GATE = 0, 1, 2, 3
LAT_KPE = MLA_Q_LORA + MLA_KV_LORA


def _cparams(*sem):
    return pltpu.CompilerParams(dimension_semantics=sem, vmem_limit_bytes=V7X_VMEM_LIMIT)


def _mod_kernel(c_ref, w_ref, b_ref, o_ref):
    w = w_ref[0].astype(BF)
    o_ref[0] = jnp.dot(c_ref[...], w, preferred_element_type=F32) + b_ref[0]


def _mod_call(c8, w_ada, b_ada3):
    depth, d, n = w_ada.shape
    tn = 1024
    return pl.pallas_call(
        _mod_kernel,
        grid=(depth, n // tn),
        in_specs=[pl.BlockSpec((8, d), lambda l, j: (0, 0)),
                  pl.BlockSpec((1, d, tn), lambda l, j: (l, 0, j)),
                  pl.BlockSpec((1, 1, tn), lambda l, j: (l, 0, j))],
        out_specs=pl.BlockSpec((1, 8, tn), lambda l, j: (l, 0, j)),
        out_shape=jax.ShapeDtypeStruct((depth, 8, n), F32),
        compiler_params=_cparams("arbitrary", "arbitrary"),
        name="adaln_mod",
    )(c8, w_ada, b_ada3)


def _norm_kernel(x_ref, g_ref, sc_ref, sh_ref, ht_ref):
    x = x_ref[...]
    ms = jnp.mean(x * x, axis=-1, keepdims=True)
    y = x * lax.rsqrt(ms + EPS) * g_ref[...]
    ht_ref[...] = (y * (1.0 + sc_ref[...]) + sh_ref[...]).T.astype(BF)


def _norm_call(x, g, sc, sh):
    s, d = x.shape
    tm = min(512, s)
    row = pl.BlockSpec((1, d), lambda i: (0, 0))
    return pl.pallas_call(
        _norm_kernel,
        grid=(s // tm,),
        in_specs=[pl.BlockSpec((tm, d), lambda i: (i, 0)), row, row, row],
        out_specs=pl.BlockSpec((d, tm), lambda i: (0, i)),
        out_shape=jax.ShapeDtypeStruct((d, s), BF),
        compiler_params=_cparams("arbitrary"),
        name="adaln_rmsnorm",
    )(x, g, sc, sh)


def _proj_tile(w_ref, xt_ref, rs_ref, o_ref, group, layout, tk):
    y = jnp.dot(w_ref[...], xt_ref[...], preferred_element_type=F32)
    tn, tm = y.shape
    if group:
        y3 = y.reshape(tn // group, group, tm)
        ms = jnp.mean(y3 * y3, axis=1, keepdims=True)
        y = (y3 * lax.rsqrt(ms + EPS)).reshape(tn, tm)
    y = y * rs_ref[...]
    if layout == "t":
        o_ref[...] = y.astype(o_ref.dtype)
    elif layout == "n":
        o_ref[...] = y.T.astype(o_ref.dtype)
    elif layout == "t4":
        for a in range(tn // HEAD_W):
            for b in range(tm // tk):
                o_ref[a, b] = y[a * HEAD_W:(a + 1) * HEAD_W, b * tk:(b + 1) * tk].astype(o_ref.dtype)
    elif layout == "qpad":
        first = lax.broadcasted_iota(jnp.int32, (HEAD_W, tm), 0) < DA_QK
        for a in range(tn // HEAD_W):
            ya = y[a * HEAD_W:(a + 1) * HEAD_W]
            o_ref[2 * a] = jnp.where(first, ya, 0.0).astype(o_ref.dtype)
            o_ref[2 * a + 1] = jnp.where(first, 0.0, ya).astype(o_ref.dtype)


def _proj_kernel(w_ref, xt_ref, rs_ref, *o_refs, specs, tk):
    seg = pl.program_id(0)
    for k, (group, layout, _) in enumerate(specs):
        @pl.when(seg == k)
        def _(k=k, group=group, layout=layout):
            _proj_tile(w_ref, xt_ref, rs_ref, o_refs[k], group, layout, tk)


def _proj_call(wt, layer, seg0, ht, rowscale, specs, *, name):
    k, s = ht.shape
    nseg = len(specs)
    tn = SEG
    tm = min(1024, s)
    nj = s // tm
    tk = min(ATT_TK, s)
    out_shapes, out_specs = [], []
    for seg, (_, layout, out_dtype) in enumerate(specs):
        def tile(i, j, seg=seg):
            return jnp.where(i < seg, 0, jnp.where(i > seg, nj - 1, j))

        if layout == "t":
            out_shapes.append(jax.ShapeDtypeStruct((tn, s), out_dtype))
            out_specs.append(pl.BlockSpec((tn, tm), lambda i, j, t=tile: (0, t(i, j))))
        elif layout == "n":
            out_shapes.append(jax.ShapeDtypeStruct((s, tn), out_dtype))
            out_specs.append(pl.BlockSpec((tm, tn), lambda i, j, t=tile: (t(i, j), 0)))
        elif layout == "t4":
            out_shapes.append(jax.ShapeDtypeStruct((tn // HEAD_W, s // tk, HEAD_W, tk), out_dtype))
            out_specs.append(pl.BlockSpec((tn // HEAD_W, tm // tk, HEAD_W, tk),
                                          lambda i, j, t=tile: (0, t(i, j), 0, 0)))
        else:
            out_shapes.append(jax.ShapeDtypeStruct((2 * tn // HEAD_W, HEAD_W, s), out_dtype))
            out_specs.append(pl.BlockSpec((2 * tn // HEAD_W, HEAD_W, tm),
                                          lambda i, j, t=tile: (0, 0, t(i, j))))
    return pl.pallas_call(
        functools.partial(_proj_kernel, specs=tuple(specs), tk=tk),
        grid=(nseg, nj),
        in_specs=[pl.BlockSpec((None, tn, k), lambda i, j: (layer, seg0 + i, 0)),
                  pl.BlockSpec((k, tm), lambda i, j: (0, j)),
                  pl.BlockSpec((tn, 1), lambda i, j: (i, 0))],
        out_specs=out_specs,
        out_shape=out_shapes,
        compiler_params=_cparams("arbitrary", "arbitrary"),
        name=name,
    )(wt, ht, rowscale)


def _resid_proj_kernel(a_ref, w_ref, x_ref, g_ref, o_ref):
    y = jnp.dot(a_ref[...], w_ref[...], preferred_element_type=F32)
    o_ref[...] = x_ref[...] + g_ref[...] * y


def _resid_proj_call(a, w, layer, x, g):
    s, k = a.shape
    n = w.shape[2]
    tm = min(1024, s)
    tn = min(1024, n)
    return pl.pallas_call(
        _resid_proj_kernel,
        grid=(s // tm, n // tn),
        in_specs=[pl.BlockSpec((tm, k), lambda i, j: (i, 0)),
                  pl.BlockSpec((None, k, tn), lambda i, j: (layer, 0, j)),
                  pl.BlockSpec((tm, tn), lambda i, j: (i, j)),
                  pl.BlockSpec((1, tn), lambda i, j: (0, j))],
        out_specs=pl.BlockSpec((tm, tn), lambda i, j: (i, j)),
        out_shape=jax.ShapeDtypeStruct((s, n), F32),
        compiler_params=_cparams("arbitrary", "arbitrary"),
        name="out_proj_residual",
    )(a, w, x, g)


def _rms_rows(x, g):
    return x * lax.rsqrt(jnp.mean(x * x, axis=0, keepdims=True) + EPS) * g


def _rope_rows(x, cos_t, sin_t):
    half = MLA_ROPE // 2
    x1, x2 = x[:half], x[half:]
    return x1 * cos_t - x2 * sin_t, x1 * sin_t + x2 * cos_t


def _mla_up_kernel(cb_ref, gq_ref, gkv_ref, wq_ref, wk_ref, wv_ref, gqn_ref, gqp_ref, gkn_ref,
                   gkp_ref, cost_ref, sint_ref, qt_ref, k_ref, vt_ref, *, tk):
    tm = cb_ref.shape[1]
    hw = MLA_NOPE + MLA_ROPE
    cos_t = cost_ref[...]
    sin_t = sint_ref[...]
    cqn = _rms_rows(cb_ref[0:MLA_Q_LORA], gq_ref[...]).astype(BF)
    qt = jnp.dot(wq_ref[...], cqn, preferred_element_type=F32)
    qzero = jnp.zeros((MLA_QK_PAD - hw, tm), BF)
    for h in range(MLA_HEADS):
        nope = _rms_rows(qt[h * hw:h * hw + MLA_NOPE], gqn_ref[...])
        pe1, pe2 = _rope_rows(_rms_rows(qt[h * hw + MLA_NOPE:(h + 1) * hw], gqp_ref[...]), cos_t, sin_t)
        qt_ref[h, 0:MLA_NOPE] = nope.astype(BF)
        qt_ref[h, MLA_NOPE:MLA_NOPE + MLA_ROPE // 2] = pe1.astype(BF)
        qt_ref[h, MLA_NOPE + MLA_ROPE // 2:hw] = pe2.astype(BF)
        qt_ref[h, hw:MLA_QK_PAD] = qzero
    ckvn = _rms_rows(cb_ref[MLA_Q_LORA:LAT_KPE], gkv_ref[...]).astype(BF)
    kt = jnp.dot(wk_ref[...], ckvn, preferred_element_type=F32)
    vt = jnp.dot(wv_ref[...], ckvn, preferred_element_type=F32)
    kpe1, kpe2 = _rope_rows(_rms_rows(cb_ref[LAT_KPE:LAT_KPE + MLA_ROPE], gkp_ref[...]), cos_t, sin_t)
    kzero = jnp.zeros((MLA_QK_PAD - hw, tm), F32)
    for h in range(MLA_HEADS):
        kn = _rms_rows(kt[h * MLA_NOPE:(h + 1) * MLA_NOPE], gkn_ref[...])
        k_ref[h] = jnp.concatenate([kn, kpe1, kpe2, kzero], axis=0).T.astype(BF)
        for b in range(tm // tk):
            vt_ref[h, b] = vt[h * MLA_V:(h + 1) * MLA_V, b * tk:(b + 1) * tk].astype(BF)


def _mla_up_call(cbt, layer, gq, gkv, wq, wk, wv, gqn, gqp, gkn, gkp, cos_t, sin_t):
    s = cbt.shape[1]
    tm = min(512, s)
    tk = min(ATT_TK, s)
    hw = MLA_NOPE + MLA_ROPE
    half = MLA_ROPE // 2

    def const(shape):
        return pl.BlockSpec(shape, lambda i: tuple(0 for _ in shape))

    def layer_w(shape):
        return pl.BlockSpec((None,) + shape, lambda i: (layer, 0, 0))

    return pl.pallas_call(
        functools.partial(_mla_up_kernel, tk=tk),
        grid=(s // tm,),
        in_specs=[pl.BlockSpec((cbt.shape[0], tm), lambda i: (0, i)),
                  const((MLA_Q_LORA, 1)), const((MLA_KV_LORA, 1)),
                  layer_w((MLA_HEADS * hw, MLA_Q_LORA)),
                  layer_w((MLA_HEADS * MLA_NOPE, MLA_KV_LORA)),
                  layer_w((MLA_HEADS * MLA_V, MLA_KV_LORA)),
                  const((MLA_NOPE, 1)), const((MLA_ROPE, 1)), const((MLA_NOPE, 1)),
                  const((MLA_ROPE, 1)),
                  pl.BlockSpec((half, tm), lambda i: (0, i)),
                  pl.BlockSpec((half, tm), lambda i: (0, i))],
        out_specs=[pl.BlockSpec((MLA_HEADS, MLA_QK_PAD, tm), lambda i: (0, 0, i)),
                   pl.BlockSpec((MLA_HEADS, tm, MLA_QK_PAD), lambda i: (0, i, 0)),
                   pl.BlockSpec((MLA_HEADS, tm // tk, MLA_V, tk), lambda i: (0, i, 0, 0))],
        out_shape=[jax.ShapeDtypeStruct((MLA_HEADS, MLA_QK_PAD, s), BF),
                   jax.ShapeDtypeStruct((MLA_HEADS, s, MLA_QK_PAD), BF),
                   jax.ShapeDtypeStruct((MLA_HEADS, s // tk, MLA_V, tk), BF)],
        compiler_params=_cparams("arbitrary"),
        name="mla_up",
    )(cbt, gq, gkv, wq, wk, wv, gqn, gqp, gkn, gkp, cos_t, sin_t)


def _logits_phase(kb, nblk, qts, k_rows, tk):
    rows = pl.ds(pl.multiple_of(kb * tk, tk), nblk * tk)
    return [jnp.dot(k_rows(g, rows), qts[g], preferred_element_type=F32) for g in range(len(qts))]


def _pv_phase(kb, nblk, ps, vt_ref, tk):
    pvs = []
    for g, p in enumerate(ps):
        pv = jnp.dot(vt_ref[g, kb], p[0:tk], preferred_element_type=F32)
        for j in range(1, nblk):
            pv += jnp.dot(vt_ref[g, kb + j], p[j * tk:(j + 1) * tk], preferred_element_type=F32)
        pvs.append(pv)
    return pvs


def _softmax_block(kb, nblk, biases, qts, k_rows, vt_ref, m_ref, l_ref, acc_ref, tk):
    hg = len(qts)
    ss = _logits_phase(kb, nblk, qts, k_rows, tk)
    if biases is not None:
        ss = [s + b for s, b in zip(ss, biases)]
    alphas, ps = [], []
    for g in range(hg):
        m_old = m_ref[g]
        m_new = jnp.maximum(m_old, jnp.max(ss[g], axis=0, keepdims=True))
        alpha = jnp.exp2(m_old - m_new)
        p = jnp.exp2(ss[g] - m_new)
        l_ref[g] = alpha * l_ref[g] + jnp.sum(p, axis=0, keepdims=True)
        m_ref[g] = m_new
        alphas.append(alpha)
        ps.append(p.astype(BF))
    pvs = _pv_phase(kb, nblk, ps, vt_ref, tk)
    for g in range(hg):
        acc_ref[g] = alphas[g] * acc_ref[g] + pvs[g]


def _fixed_ref_block(kb, nblk, qts, k_rows, vt_ref, m_ref, l_ref, acc_ref, tk):
    hg = len(qts)
    ss = _logits_phase(kb, nblk, qts, k_rows, tk)
    ps = []
    for g in range(hg):
        p = jnp.exp2(ss[g] - m_ref[g])
        l_ref[g] += jnp.sum(p, axis=0, keepdims=True)
        ps.append(p.astype(BF))
    pvs = _pv_phase(kb, nblk, ps, vt_ref, tk)
    for g in range(hg):
        acc_ref[g] += pvs[g]


def _far_sweep(n_far, far_blocks, step):
    n_big = n_far // far_blocks

    def big(i, carry):
        step(i * far_blocks, far_blocks)
        return carry

    lax.fori_loop(0, n_big, big, 0)
    done = n_big * far_blocks
    size = far_blocks // 2
    while size >= 1:
        take = ((n_far - done) // size) * size

        @pl.when(take > 0)
        def _(done=done, size=size):
            step(done, size)

        done = done + take
        size //= 2


def _overflowed(l_ref, acc_ref):
    bad = jnp.maximum(jnp.max(jnp.where(jnp.isfinite(l_ref[...]), 0.0, 1.0)),
                      jnp.max(jnp.where(jnp.isfinite(acc_ref[...]), 0.0, 1.0)))
    return bad > 0.0


def _attention_sweep(n_far, far_blocks, near_step, far_std, far_fixed, m_ref, l_ref, acc_ref):
    _init_softmax_state(m_ref, l_ref, acc_ref)
    near_step()
    _far_sweep(n_far, far_blocks, far_fixed)

    @pl.when(_overflowed(l_ref, acc_ref))
    def _():
        _init_softmax_state(m_ref, l_ref, acc_ref)
        near_step()

        def one(kb, carry):
            far_std(kb)
            return carry

        lax.fori_loop(0, n_far, one, 0)


def _resident_spec(shape, index_map):
    return pl.BlockSpec(shape, index_map, pipeline_mode=pl.Buffered(1))


def _init_softmax_state(m_ref, l_ref, acc_ref):
    m_ref[...] = jnp.full(m_ref.shape, NEG_INF, F32)
    l_ref[...] = jnp.zeros(l_ref.shape, F32)
    acc_ref[...] = jnp.zeros(acc_ref.shape, F32)


def _diff_attn_kernel(q_ref, k_ref, vt_ref, bn_ref, lam_ref, g_ref, o_ref,
                      m_ref, l_ref, acc_ref, *, tq, tk, hg, lam_init):
    for sub in range(ATT_QSUB):
        _diff_query_block(pl.program_id(1) * ATT_QSUB + sub, slice(sub * tq, (sub + 1) * tq),
                          q_ref, k_ref, vt_ref, bn_ref, lam_ref, g_ref, o_ref, m_ref, l_ref, acc_ref,
                          tq=tq, tk=tk, hg=hg, lam_init=lam_init)


def _diff_query_block(qb, span, q_ref, k_ref, vt_ref, bn_ref, lam_ref, g_ref, o_ref,
                      m_ref, l_ref, acc_ref, *, tq, tk, hg, lam_init):
    qts = [jnp.concatenate([q_ref[2 * g, :, span], q_ref[2 * g + 1, :, span]], axis=1)
           for g in range(hg)]

    def k_rows(g, rows):
        return k_ref[rows, g * HEAD_W:(g + 1) * HEAD_W]

    state = (vt_ref, m_ref, l_ref, acc_ref, tk)
    first = (qb == 0).astype(jnp.int32)

    def near_step():
        _softmax_block(qb - 1 + first, 2, [bn_ref[g, first] for g in range(hg)], qts, k_rows, *state)

    _attention_sweep(
        jnp.maximum(qb - 1, 0), DIFF_FAR_BLOCKS, near_step,
        lambda kb: _softmax_block(kb, 1, None, qts, k_rows, *state),
        lambda kb, nblk: _fixed_ref_block(kb, nblk, qts, k_rows, *state),
        m_ref, l_ref, acc_ref)

    lp = lam_ref[...]
    lam = (jnp.exp(jnp.sum(lp[0:1] * lp[1:2], axis=1, keepdims=True))
           - jnp.exp(jnp.sum(lp[2:3] * lp[3:4], axis=1, keepdims=True)) + lam_init)
    for g in range(hg):
        o = acc_ref[g] / l_ref[g]
        d = o[:, :tq] - lam * o[:, tq:]
        ms = jnp.mean(d * d, axis=0, keepdims=True)
        d = d * lax.rsqrt(ms + EPS) * g_ref[...] * (1.0 - lam_init)
        o_ref[span, g * HEAD_W:(g + 1) * HEAD_W] = d.T.astype(o_ref.dtype)


def _diff_attn_call(qpad, k, vt4, bias_near, lam_p, sub_g, *, lam_init):
    s = k.shape[0]
    tq = tk = min(ATT_TQ, s)
    nkb = s // tk
    n = 2 * tq
    hg = DIFF_HEAD_GROUP
    return pl.pallas_call(
        functools.partial(_diff_attn_kernel, tq=tq, tk=tk, hg=hg, lam_init=lam_init),
        grid=(DA_HEADS // hg, s // (tq * ATT_QSUB)),
        in_specs=[pl.BlockSpec((2 * hg, HEAD_W, tq * ATT_QSUB), lambda h, i: (h, 0, i)),
                  _resident_spec((s, hg * HEAD_W), lambda h, i: (0, h)),
                  _resident_spec((hg, nkb, HEAD_W, tk), lambda h, i: (h, 0, 0, 0)),
                  _resident_spec((hg, 2, 2 * tk, n), lambda h, i: (h, 0, 0, 0)),
                  pl.BlockSpec((4, DA_QK), lambda h, i: (0, 0)),
                  pl.BlockSpec((HEAD_W, 1), lambda h, i: (0, 0))],
        out_specs=pl.BlockSpec((tq * ATT_QSUB, hg * HEAD_W), lambda h, i: (i, h)),
        out_shape=jax.ShapeDtypeStruct((s, DA_HEADS * HEAD_W), BF),
        scratch_shapes=[pltpu.VMEM((hg, 1, n), F32), pltpu.VMEM((hg, 1, n), F32),
                        pltpu.VMEM((hg, HEAD_W, n), F32)],
        compiler_params=_cparams("arbitrary", "arbitrary"),
        name="diff_attention",
    )(qpad, k, vt4, bias_near, lam_p, sub_g)


def _mla_attn_kernel(q_ref, k_ref, vt_ref, mask_ref, o_ref, m_ref, l_ref, acc_ref, *, tk, hg):
    for sub in range(ATT_QSUB):
        _mla_query_block(pl.program_id(1) * ATT_QSUB + sub, slice(sub * tk, (sub + 1) * tk),
                         q_ref, k_ref, vt_ref, mask_ref, o_ref, m_ref, l_ref, acc_ref, tk=tk, hg=hg)


def _mla_query_block(qb, span, q_ref, k_ref, vt_ref, mask_ref, o_ref, m_ref, l_ref, acc_ref, *, tk, hg):
    qts = [q_ref[g, :, span] for g in range(hg)]

    def k_rows(g, rows):
        return k_ref[g, rows, :]

    state = (vt_ref, m_ref, l_ref, acc_ref, tk)

    def near_step():
        _softmax_block(qb, 1, [mask_ref[...]] * hg, qts, k_rows, *state)

    _attention_sweep(
        qb, MLA_FAR_BLOCKS, near_step,
        lambda kb: _softmax_block(kb, 1, None, qts, k_rows, *state),
        lambda kb, nblk: _fixed_ref_block(kb, nblk, qts, k_rows, *state),
        m_ref, l_ref, acc_ref)
    for g in range(hg):
        o = acc_ref[g] / l_ref[g]
        o_ref[span, g * HEAD_W:(g + 1) * HEAD_W] = o.T.astype(o_ref.dtype)


def _mla_attn_call(qt, k, vt4, mask_diag):
    s = k.shape[1]
    tq = tk = min(ATT_TQ, s)
    nkb = s // tk
    hg = MLA_HEAD_GROUP
    return pl.pallas_call(
        functools.partial(_mla_attn_kernel, tk=tk, hg=hg),
        grid=(MLA_HEADS // hg, s // (tq * ATT_QSUB)),
        in_specs=[pl.BlockSpec((hg, MLA_QK_PAD, tq * ATT_QSUB), lambda h, i: (h, 0, i)),
                  _resident_spec((hg, s, MLA_QK_PAD), lambda h, i: (h, 0, 0)),
                  _resident_spec((hg, nkb, HEAD_W, tk), lambda h, i: (h, 0, 0, 0)),
                  pl.BlockSpec((tk, tq), lambda h, i: (0, 0))],
        out_specs=pl.BlockSpec((tq * ATT_QSUB, hg * HEAD_W), lambda h, i: (i, h)),
        out_shape=jax.ShapeDtypeStruct((s, MLA_HEADS * HEAD_W), BF),
        scratch_shapes=[pltpu.VMEM((hg, 1, tq), F32), pltpu.VMEM((hg, 1, tq), F32),
                        pltpu.VMEM((hg, HEAD_W, tq), F32)],
        compiler_params=_cparams("arbitrary", "arbitrary"),
        name="mla_attention",
    )(qt, k, vt4, mask_diag)


def _stick_blocks(blocks, qts, k_ref, vt_ref, tri_neg, carry_ref, acc_ref, tk):
    hg = len(qts)
    zs = []
    for kb, _ in blocks:
        rows = pl.ds(pl.multiple_of(kb * tk, tk), tk)
        zs.append([jnp.dot(k_ref[rows, g * HEAD_W:(g + 1) * HEAD_W], qts[g],
                           preferred_element_type=F32) for g in range(hg)])
    carries = [carry_ref[g] for g in range(hg)]
    logsigs, his, los = [], [], []
    for b, (_, mask) in enumerate(blocks):
        for g in range(hg):
            z = zs[b][g]
            sp = jnp.maximum(z, 0.0) + jnp.log2(1.0 + jnp.exp2(-jnp.abs(z)))
            logsigs.append(z - sp + carries[g])
            if mask is not None:
                sp = jnp.where(mask, sp, 0.0)
            hi = sp.astype(BF)
            his.append(hi)
            los.append((sp - hi.astype(F32)).astype(BF))
            carries[g] = carries[g] - jnp.sum(sp, axis=0, keepdims=True)
    betweens = [jnp.dot(tri_neg, hi, preferred_element_type=F32)
                + jnp.dot(tri_neg, lo, preferred_element_type=F32) for hi, lo in zip(his, los)]
    ws = []
    for b, (_, mask) in enumerate(blocks):
        for g in range(hg):
            w = jnp.exp2(logsigs[b * hg + g] + betweens[b * hg + g])
            if mask is not None:
                w = jnp.where(mask, w, 0.0)
            ws.append(w.astype(BF))
    for g in range(hg):
        pv = jnp.dot(vt_ref[g, blocks[0][0]], ws[g], preferred_element_type=F32)
        for b in range(1, len(blocks)):
            pv += jnp.dot(vt_ref[g, blocks[b][0]], ws[b * hg + g], preferred_element_type=F32)
        acc_ref[g] += pv
        carry_ref[g] = carries[g]


def _stick_kernel(q_ref, k_ref, vt_ref, tri_ref, o_ref, carry_ref, acc_ref, *, tq, tk, hg):
    for sub in range(ATT_QSUB):
        _stick_query_block(pl.program_id(1) * ATT_QSUB + sub, slice(sub * tq, (sub + 1) * tq),
                           q_ref, k_ref, vt_ref, tri_ref, o_ref, carry_ref, acc_ref,
                           tq=tq, tk=tk, hg=hg)


def _stick_query_block(qb, span, q_ref, k_ref, vt_ref, tri_ref, o_ref, carry_ref, acc_ref,
                       *, tq, tk, hg):
    qts = [q_ref[g * HEAD_W:(g + 1) * HEAD_W, span] for g in range(hg)]
    tri = tri_ref[...]
    carry_ref[...] = jnp.zeros(carry_ref.shape, F32)
    acc_ref[...] = jnp.zeros(acc_ref.shape, F32)
    kpos = lax.broadcasted_iota(jnp.int32, (tk, tq), 0)
    qpos = lax.broadcasted_iota(jnp.int32, (tk, tq), 1)
    causal = kpos < qpos
    state = (qts, k_ref, vt_ref, tri, carry_ref, acc_ref, tk)

    @pl.when(qb == 0)
    def _():
        _stick_blocks([(qb, causal)], *state)

    @pl.when(qb > 0)
    def _():
        _stick_blocks([(qb, causal), (qb - 1, None)], *state)

    def more(state_):
        i, cmax = state_
        return jnp.logical_and(i < qb, cmax > SB_LOG2W_UNDERFLOW)

    def earlier(state_):
        i, _ = state_
        _stick_blocks([(qb - 1 - i, None)], *state)
        return i + 1, jnp.max(carry_ref[...])

    lax.while_loop(more, earlier, (jnp.int32(1), jnp.max(carry_ref[...])))
    for g in range(hg):
        o_ref[span, g * HEAD_W:(g + 1) * HEAD_W] = acc_ref[g].T.astype(o_ref.dtype)


def _stick_call(qt, k, vt4, tri):
    s = k.shape[0]
    tq = tk = min(ATT_TQ, s)
    nkb = s // tk
    hg = SB_HEAD_GROUP
    return pl.pallas_call(
        functools.partial(_stick_kernel, tq=tq, tk=tk, hg=hg),
        grid=(SB_HEADS // hg, s // (tq * ATT_QSUB)),
        in_specs=[pl.BlockSpec((hg * HEAD_W, tq * ATT_QSUB), lambda h, i: (h, i)),
                  _resident_spec((s, hg * HEAD_W), lambda h, i: (0, h)),
                  _resident_spec((hg, nkb, HEAD_W, tk), lambda h, i: (h, 0, 0, 0)),
                  pl.BlockSpec((tk, tk), lambda h, i: (0, 0))],
        out_specs=pl.BlockSpec((tq * ATT_QSUB, hg * HEAD_W), lambda h, i: (i, h)),
        out_shape=jax.ShapeDtypeStruct((s, SB_HEADS * HEAD_W), BF),
        scratch_shapes=[pltpu.VMEM((hg, 1, tq), F32), pltpu.VMEM((hg, HEAD_W, tq), F32)],
        compiler_params=_cparams("arbitrary", "arbitrary"),
        name="stick_breaking",
    )(qt, k, vt4, tri)


def _merge_kernel(ht_ref, ya_ref, yb_ref, yc_ref, wga_ref, wgb_ref, wgc_ref, wb_ref, o_ref):
    ht = ht_ref[...]
    acc = None
    for n, (y_ref, wg_ref) in enumerate(((ya_ref, wga_ref), (yb_ref, wgb_ref), (yc_ref, wgc_ref))):
        gate = jax.nn.sigmoid(jnp.dot(wg_ref[...], ht, preferred_element_type=F32)).T
        up = jnp.dot(y_ref[...], wb_ref[n], preferred_element_type=F32)
        acc = gate * up if acc is None else acc + gate * up
    o_ref[...] = acc.astype(o_ref.dtype)


def _merge_call(ht, ya, yb, yc, wt_tail, wb, layer):
    d, s = ht.shape
    tm = min(1024, s)
    tn = 512
    ysp = pl.BlockSpec((tm, BRANCH_WIDTH), lambda i, j: (i, 0))

    def gate_spec(n):
        blk0 = (SEG_GATE * SEG + n * d) // tn
        return pl.BlockSpec((None, tn, d), lambda i, j: (layer, blk0 + j, 0))

    return pl.pallas_call(
        _merge_kernel,
        grid=(s // tm, d // tn),
        in_specs=[pl.BlockSpec((d, tm), lambda i, j: (0, i)), ysp, ysp, ysp,
                  gate_spec(0), gate_spec(1), gate_spec(2),
                  pl.BlockSpec((None, N_BRANCHES, BRANCH_WIDTH, tn), lambda i, j: (layer, 0, 0, j))],
        out_specs=pl.BlockSpec((tm, tn), lambda i, j: (i, j)),
        out_shape=jax.ShapeDtypeStruct((s, d), BF),
        compiler_params=_cparams("arbitrary", "arbitrary"),
        name="gated_merge",
    )(ht, ya, yb, yc, wt_tail, wt_tail, wt_tail, wb)


def _mlp_kernel(x_ref, ng_ref, sc_ref, sh_ref, w1_ref, w2_ref, g_ref, o_ref, acc_ref, h_ref):
    f = pl.program_id(1)

    @pl.when(f == 0)
    def _():
        acc_ref[...] = jnp.zeros(acc_ref.shape, F32)
        x = x_ref[...]
        y = x * lax.rsqrt(jnp.mean(x * x, axis=-1, keepdims=True) + EPS) * ng_ref[...]
        h_ref[...] = (y * (1.0 + sc_ref[...]) + sh_ref[...]).astype(BF)

    u = jnp.dot(h_ref[...], w1_ref[...], preferred_element_type=F32)
    u = jnp.square(jnp.maximum(u, 0.0)).astype(BF)
    acc_ref[...] += jnp.dot(u, w2_ref[...], preferred_element_type=F32)

    @pl.when(f == pl.num_programs(1) - 1)
    def _():
        o_ref[...] = x_ref[...] + g_ref[...] * acc_ref[...]


def _mlp_call(x, norm_g, sc, sh, w1, w2, layer, g):
    s, d = x.shape
    ff = w1.shape[2]
    tm = min(512, s)
    tf = 1024
    row = pl.BlockSpec((1, d), lambda i, f: (0, 0))
    return pl.pallas_call(
        _mlp_kernel,
        grid=(s // tm, ff // tf),
        in_specs=[pl.BlockSpec((tm, d), lambda i, f: (i, 0)), row, row, row,
                  pl.BlockSpec((None, d, tf), lambda i, f: (layer, 0, f)),
                  pl.BlockSpec((None, tf, d), lambda i, f: (layer, f, 0)),
                  row],
        out_specs=pl.BlockSpec((tm, d), lambda i, f: (i, 0)),
        out_shape=jax.ShapeDtypeStruct((s, d), F32),
        scratch_shapes=[pltpu.VMEM((tm, d), F32), pltpu.VMEM((tm, d), BF)],
        compiler_params=_cparams("arbitrary", "arbitrary"),
        name="sqrelu_mlp",
    )(x, norm_g, sc, sh, w1, w2, g)


def _t5_bucket(rel):
    nb = T5_BUCKETS // 2
    max_exact = nb // 2
    n = jnp.abs(rel)
    large = max_exact + (jnp.log(jnp.maximum(n, 1).astype(F32) / max_exact)
                         / math.log(T5_MAX_DIST / max_exact) * (nb - max_exact)).astype(jnp.int32)
    large = jnp.minimum(large, nb - 1)
    return jnp.where(rel > 0, nb, 0) + jnp.where(n < max_exact, n, large)


def _bias_tiles(t5_bias, tq, tk):
    table = t5_bias.astype(F32)

    def lookup(bucket):
        hit = bucket[..., None, None] == jnp.arange(T5_BUCKETS)[:, None]
        return jnp.sum(jnp.where(hit, table, 0.0), axis=-2)

    kl = jnp.arange(tk)[:, None]
    ql = jnp.arange(tq)[None, :]
    far = lookup(_t5_bucket(jnp.array(-(tk + tq), jnp.int32)))
    diag = jnp.transpose(lookup(_t5_bucket(kl - ql)) - far, (2, 0, 1))
    prev = jnp.transpose(lookup(_t5_bucket(kl - tk - ql)) - far, (2, 0, 1))
    allowed = (kl // CHUNK) <= (ql // CHUNK)
    diag = jnp.where(allowed[None], diag * LOG2E, NEG_INF)
    near = jnp.stack([jnp.concatenate([prev * LOG2E, diag], axis=1),
                      jnp.concatenate([diag, jnp.full_like(diag, NEG_INF)], axis=1)], axis=1)
    return jnp.concatenate([near, near], axis=3)


def _rope_tables(s):
    half = MLA_ROPE // 2
    inv = ROPE_BASE ** (-jnp.arange(half, dtype=F32) / half)
    ang = jnp.arange(s).astype(F32)[:, None] * inv[None, :]
    return jnp.cos(ang).T, jnp.sin(ang).T


def kernel(x, c, w_ada, b_ada, norm_mix_g, norm_mlp_g, w_in, diff_qk_g, diff_lambda, diff_subln_g,
           t5_bias, mla_q_norm_g, mla_kv_norm_g, w_q_up, w_kv_up, mla_qk_g, w_branch, w_out,
           w_mlp_in, w_mlp_out):
    b, s, d = x.shape
    assert b == 1 and d == D_MODEL
    depth = w_ada.shape[0]
    tq = tk = min(ATT_TQ, s)
    hw = MLA_NOPE + MLA_ROPE

    mod = _mod_call(jnp.broadcast_to(c, (8, d)).astype(BF), w_ada, b_ada[:, None, :])[:, 0, :]
    mod = mod.reshape(depth, N_MOD, 1, d)

    wt_head = jnp.swapaxes(w_in[:, :, :HEAD_ROWS], 1, 2).astype(BF)
    wt_tail = jnp.swapaxes(w_in[:, :, TAIL_ROW0:], 1, 2).astype(BF)
    w_br = w_branch.astype(BF)
    w_o = w_out.astype(BF)
    w_1 = w_mlp_in.astype(BF)
    w_2 = w_mlp_out.astype(BF)
    wt_qu = jnp.swapaxes(w_q_up, 1, 2).astype(BF)
    wkv = w_kv_up.reshape(depth, MLA_KV_LORA, MLA_HEADS, MLA_NOPE + MLA_V)
    wt_kn = jnp.swapaxes(wkv[..., :MLA_NOPE].reshape(depth, MLA_KV_LORA, -1), 1, 2).astype(BF)
    wt_v = jnp.swapaxes(wkv[..., MLA_NOPE:].reshape(depth, MLA_KV_LORA, -1), 1, 2).astype(BF)

    bias_near = _bias_tiles(t5_bias, tq, tk)
    kl = jnp.arange(tk)[:, None]
    ql = jnp.arange(tq)[None, :]
    mask_diag = jnp.where((kl // CHUNK) <= (ql // CHUNK), 0.0, NEG_INF).astype(F32)
    tri = -(jnp.arange(tk)[None, :] > jnp.arange(tk)[:, None]).astype(BF)
    cos_t, sin_t = _rope_tables(s)

    ones_col = jnp.ones((SEG, 1), F32)
    sb_scale = jnp.full((SB_HEADS * SB_DIM, 1), SB_DIM ** -0.5 * LOG2E, F32)
    da_scale = DA_QK ** -0.5 * LOG2E
    mla_scale = hw ** -0.5 * LOG2E

    x2 = x[0]
    for l in range(depth):
        sh1, sc1, g1, sh2, sc2, g2 = (mod[l, i] for i in range(N_MOD))
        lam_init = 0.8 - 0.6 * math.exp(-0.3 * l)

        ht = _norm_call(x2, norm_mix_g[l][None], sc1, sh1)

        gq_rows = jnp.tile(diff_qk_g[l, 0], 2 * DA_HEADS)[:, None] * da_scale
        gk_rows = jnp.tile(diff_qk_g[l, 1], 2 * DA_HEADS)[:, None]
        qa, ka, va, cbt = _proj_call(
            wt_head, l, SEG_DQ, ht, jnp.concatenate([gq_rows, gk_rows, ones_col, ones_col]),
            [(DA_QK, "qpad", BF), (DA_QK, "n", BF), (0, "t4", BF), (0, "t", F32)], name="proj_head")
        qc, kc, vc = _proj_call(
            wt_tail, l, SEG_SQ, ht, jnp.concatenate([sb_scale, ones_col, ones_col]),
            [(0, "t", BF), (0, "n", BF), (0, "t4", BF)], name="proj_tail")
        gqk = mla_qk_g[l]
        qb_t, kb, vb_t = _mla_up_call(
            cbt, l, mla_q_norm_g[l][:, None], mla_kv_norm_g[l][:, None], wt_qu, wt_kn, wt_v,
            gqk[0, :MLA_NOPE, None] * mla_scale, gqk[0, MLA_NOPE:, None] * mla_scale,
            gqk[1, :MLA_NOPE, None], gqk[1, MLA_NOPE:, None], cos_t, sin_t)

        ya = _diff_attn_call(qa, ka, va, bias_near, diff_lambda[l], diff_subln_g[l][:, None],
                             lam_init=lam_init)
        yb = _mla_attn_call(qb_t, kb, vb_t, mask_diag)
        yc = _stick_call(qc, kc, vc, tri)

        merged = _merge_call(ht, ya, yb, yc, wt_tail, w_br, l)
        x2 = _resid_proj_call(merged, w_o, l, x2, g1)

        x2 = _mlp_call(x2, norm_mlp_g[l][None], sc2, sh2, w_1, w_2, l, g2)
    return x2[None]
```

```python
import functools
import math

import jax
import jax.numpy as jnp
from jax import lax
from jax.experimental import pallas as pl
from jax.experimental.pallas import tpu as pltpu

BF = jnp.bfloat16
F32 = jnp.float32

D_MODEL = 2048
CHUNK = 64
DA_HEADS = 8
DA_QK = 64
DA_V = 128
MLA_HEADS = 8
MLA_Q_LORA = 512
MLA_KV_LORA = 256
MLA_NOPE = 128
MLA_ROPE = 64
MLA_V = 128
MLA_QK_PAD = 256
ROPE_BASE = 10000.0
SB_HEADS = 8
SB_DIM = 128
BRANCH_WIDTH = 1024
N_BRANCHES = 3
T5_BUCKETS = 32
T5_MAX_DIST = 128
D_FF = 4 * D_MODEL
N_MOD = 6
EPS = 1e-6
NEG_INF = -1e30
LOG2E = 1.4426950408889634

HEAD_W = 128
ATT_TQ = 256
ATT_TK = 256
ATT_QSUB = 2
DIFF_FAR_BLOCKS = 8
MLA_FAR_BLOCKS = 8
DIFF_HEAD_GROUP = 4
MLA_HEAD_GROUP = 4
SB_HEAD_GROUP = 8
SB_LOG2W_UNDERFLOW = -152.0
V7X_VMEM_LIMIT = 56 * 1024 * 1024

SEG = 1024
TAIL_ROW0 = 3 * SEG + MLA_Q_LORA + MLA_KV_LORA + MLA_ROPE
SEG_DQ, SEG_DK, SEG_DV, SEG_LAT = 0, 1, 2, 3
SEG_SQ, SEG_SK, SEG_SV, SEG_GATE = 0, 1, 2, 3
LAT_KPE = MLA_Q_LORA + MLA_KV_LORA


def _cparams(*sem):
    return pltpu.CompilerParams(dimension_semantics=sem, vmem_limit_bytes=V7X_VMEM_LIMIT)


def _mod_kernel(c_ref, w_ref, b_ref, o_ref):
    w = w_ref[0].astype(BF)
    o_ref[0] = jnp.dot(c_ref[...], w, preferred_element_type=F32) + b_ref[0]


def _mod_call(c8, w_ada, b_ada3):
    depth, d, n = w_ada.shape
    tn = 1024
    return pl.pallas_call(
        _mod_kernel,
        grid=(depth, n // tn),
        in_specs=[pl.BlockSpec((8, d), lambda l, j: (0, 0)),
                  pl.BlockSpec((1, d, tn), lambda l, j: (l, 0, j)),
                  pl.BlockSpec((1, 1, tn), lambda l, j: (l, 0, j))],
        out_specs=pl.BlockSpec((1, 8, tn), lambda l, j: (l, 0, j)),
        out_shape=jax.ShapeDtypeStruct((depth, 8, n), F32),
        compiler_params=_cparams("arbitrary", "arbitrary"),
        name="adaln_mod",
    )(c8, w_ada, b_ada3)


def _norm_kernel(x_ref, g_ref, sc_ref, sh_ref, ht_ref):
    x = x_ref[...]
    ms = jnp.mean(x * x, axis=-1, keepdims=True)
    y = x * lax.rsqrt(ms + EPS) * g_ref[...]
    ht_ref[...] = (y * (1.0 + sc_ref[...]) + sh_ref[...]).T.astype(BF)


def _norm_call(x, g, sc, sh):
    s, d = x.shape
    tm = min(512, s)
    row = pl.BlockSpec((1, d), lambda i: (0, 0))
    return pl.pallas_call(
        _norm_kernel,
        grid=(s // tm,),
        in_specs=[pl.BlockSpec((tm, d), lambda i: (i, 0)), row, row, row],
        out_specs=pl.BlockSpec((d, tm), lambda i: (0, i)),
        out_shape=jax.ShapeDtypeStruct((d, s), BF),
        compiler_params=_cparams("arbitrary"),
        name="adaln_rmsnorm",
    )(x, g, sc, sh)


def _proj_tile(w_ref, xt_ref, rs_ref, o_ref, group, layout, tk):
    y = jnp.dot(w_ref[...], xt_ref[...], preferred_element_type=F32)
    tn, tm = y.shape
    if group:
        y3 = y.reshape(tn // group, group, tm)
        ms = jnp.mean(y3 * y3, axis=1, keepdims=True)
        y = (y3 * lax.rsqrt(ms + EPS)).reshape(tn, tm)
    y = y * rs_ref[...]
    if layout == "t":
        o_ref[...] = y.astype(o_ref.dtype)
    elif layout == "n":
        o_ref[...] = y.T.astype(o_ref.dtype)
    elif layout == "t4":
        for a in range(tn // HEAD_W):
            for b in range(tm // tk):
                o_ref[a, b] = y[a * HEAD_W:(a + 1) * HEAD_W, b * tk:(b + 1) * tk].astype(o_ref.dtype)
    elif layout == "qpad":
        first = lax.broadcasted_iota(jnp.int32, (HEAD_W, tm), 0) < DA_QK
        for a in range(tn // HEAD_W):
            ya = y[a * HEAD_W:(a + 1) * HEAD_W]
            o_ref[2 * a] = jnp.where(first, ya, 0.0).astype(o_ref.dtype)
            o_ref[2 * a + 1] = jnp.where(first, 0.0, ya).astype(o_ref.dtype)


def _proj_kernel(w_ref, xt_ref, rs_ref, *o_refs, specs, tk):
    seg = pl.program_id(0)
    for k, (group, layout, _) in enumerate(specs):
        @pl.when(seg == k)
        def _(k=k, group=group, layout=layout):
            _proj_tile(w_ref, xt_ref, rs_ref, o_refs[k], group, layout, tk)


def _proj_call(wt, layer, seg0, ht, rowscale, specs, *, name):
    k, s = ht.shape
    nseg = len(specs)
    tn = SEG
    tm = min(1024, s)
    nj = s // tm
    tk = min(ATT_TK, s)
    out_shapes, out_specs = [], []
    for seg, (_, layout, out_dtype) in enumerate(specs):
        def tile(i, j, seg=seg):
            return jnp.where(i < seg, 0, jnp.where(i > seg, nj - 1, j))

        if layout == "t":
            out_shapes.append(jax.ShapeDtypeStruct((tn, s), out_dtype))
            out_specs.append(pl.BlockSpec((tn, tm), lambda i, j, t=tile: (0, t(i, j))))
        elif layout == "n":
            out_shapes.append(jax.ShapeDtypeStruct((s, tn), out_dtype))
            out_specs.append(pl.BlockSpec((tm, tn), lambda i, j, t=tile: (t(i, j), 0)))
        elif layout == "t4":
            out_shapes.append(jax.ShapeDtypeStruct((tn // HEAD_W, s // tk, HEAD_W, tk), out_dtype))
            out_specs.append(pl.BlockSpec((tn // HEAD_W, tm // tk, HEAD_W, tk),
                                          lambda i, j, t=tile: (0, t(i, j), 0, 0)))
        else:
            out_shapes.append(jax.ShapeDtypeStruct((2 * tn // HEAD_W, HEAD_W, s), out_dtype))
            out_specs.append(pl.BlockSpec((2 * tn // HEAD_W, HEAD_W, tm),
                                          lambda i, j, t=tile: (0, 0, t(i, j))))
    return pl.pallas_call(
        functools.partial(_proj_kernel, specs=tuple(specs), tk=tk),
        grid=(nseg, nj),
        in_specs=[pl.BlockSpec((None, tn, k), lambda i, j: (layer, seg0 + i, 0)),
                  pl.BlockSpec((k, tm), lambda i, j: (0, j)),
                  pl.BlockSpec((tn, 1), lambda i, j: (i, 0))],
        out_specs=out_specs,
        out_shape=out_shapes,
        compiler_params=_cparams("arbitrary", "arbitrary"),
        name=name,
    )(wt, ht, rowscale)


def _resid_proj_kernel(a_ref, w_ref, x_ref, g_ref, o_ref):
    y = jnp.dot(a_ref[...], w_ref[...], preferred_element_type=F32)
    o_ref[...] = x_ref[...] + g_ref[...] * y


def _resid_proj_call(a, w, layer, x, g):
    s, k = a.shape
    n = w.shape[2]
    tm = min(1024, s)
    tn = min(1024, n)
    return pl.pallas_call(
        _resid_proj_kernel,
        grid=(s // tm, n // tn),
        in_specs=[pl.BlockSpec((tm, k), lambda i, j: (i, 0)),
                  pl.BlockSpec((None, k, tn), lambda i, j: (layer, 0, j)),
                  pl.BlockSpec((tm, tn), lambda i, j: (i, j)),
                  pl.BlockSpec((1, tn), lambda i, j: (0, j))],
        out_specs=pl.BlockSpec((tm, tn), lambda i, j: (i, j)),
        out_shape=jax.ShapeDtypeStruct((s, n), F32),
        compiler_params=_cparams("arbitrary", "arbitrary"),
        name="out_proj_residual",
    )(a, w, x, g)


def _rms_rows(x, g):
    return x * lax.rsqrt(jnp.mean(x * x, axis=0, keepdims=True) + EPS) * g


def _rope_rows(x, cos_t, sin_t):
    half = MLA_ROPE // 2
    x1, x2 = x[:half], x[half:]
    return x1 * cos_t - x2 * sin_t, x1 * sin_t + x2 * cos_t


def _mla_up_kernel(cb_ref, gq_ref, gkv_ref, wq_ref, wk_ref, wv_ref, gqn_ref, gqp_ref, gkn_ref,
                   gkp_ref, cost_ref, sint_ref, qt_ref, k_ref, vt_ref, *, tk):
    tm = cb_ref.shape[1]
    hw = MLA_NOPE + MLA_ROPE
    cos_t = cost_ref[...]
    sin_t = sint_ref[...]
    cqn = _rms_rows(cb_ref[0:MLA_Q_LORA], gq_ref[...]).astype(BF)
    qt = jnp.dot(wq_ref[...], cqn, preferred_element_type=F32)
    qzero = jnp.zeros((MLA_QK_PAD - hw, tm), BF)
    for h in range(MLA_HEADS):
        nope = _rms_rows(qt[h * hw:h * hw + MLA_NOPE], gqn_ref[...])
        pe1, pe2 = _rope_rows(_rms_rows(qt[h * hw + MLA_NOPE:(h + 1) * hw], gqp_ref[...]), cos_t, sin_t)
        qt_ref[h, 0:MLA_NOPE] = nope.astype(BF)
        qt_ref[h, MLA_NOPE:MLA_NOPE + MLA_ROPE // 2] = pe1.astype(BF)
        qt_ref[h, MLA_NOPE + MLA_ROPE // 2:hw] = pe2.astype(BF)
        qt_ref[h, hw:MLA_QK_PAD] = qzero
    ckvn = _rms_rows(cb_ref[MLA_Q_LORA:LAT_KPE], gkv_ref[...]).astype(BF)
    kt = jnp.dot(wk_ref[...], ckvn, preferred_element_type=F32)
    vt = jnp.dot(wv_ref[...], ckvn, preferred_element_type=F32)
    kpe1, kpe2 = _rope_rows(_rms_rows(cb_ref[LAT_KPE:LAT_KPE + MLA_ROPE], gkp_ref[...]), cos_t, sin_t)
    kzero = jnp.zeros((MLA_QK_PAD - hw, tm), F32)
    for h in range(MLA_HEADS):
        kn = _rms_rows(kt[h * MLA_NOPE:(h + 1) * MLA_NOPE], gkn_ref[...])
        k_ref[h] = jnp.concatenate([kn, kpe1, kpe2, kzero], axis=0).T.astype(BF)
        for b in range(tm // tk):
            vt_ref[h, b] = vt[h * MLA_V:(h + 1) * MLA_V, b * tk:(b + 1) * tk].astype(BF)


def _mla_up_call(cbt, layer, gq, gkv, wq, wk, wv, gqn, gqp, gkn, gkp, cos_t, sin_t):
    s = cbt.shape[1]
    tm = min(512, s)
    tk = min(ATT_TK, s)
    hw = MLA_NOPE + MLA_ROPE
    half = MLA_ROPE // 2

    def const(shape):
        return pl.BlockSpec(shape, lambda i: tuple(0 for _ in shape))

    def layer_w(shape):
        return pl.BlockSpec((None,) + shape, lambda i: (layer, 0, 0))

    return pl.pallas_call(
        functools.partial(_mla_up_kernel, tk=tk),
        grid=(s // tm,),
        in_specs=[pl.BlockSpec((cbt.shape[0], tm), lambda i: (0, i)),
                  const((MLA_Q_LORA, 1)), const((MLA_KV_LORA, 1)),
                  layer_w((MLA_HEADS * hw, MLA_Q_LORA)),
                  layer_w((MLA_HEADS * MLA_NOPE, MLA_KV_LORA)),
                  layer_w((MLA_HEADS * MLA_V, MLA_KV_LORA)),
                  const((MLA_NOPE, 1)), const((MLA_ROPE, 1)), const((MLA_NOPE, 1)),
                  const((MLA_ROPE, 1)),
                  pl.BlockSpec((half, tm), lambda i: (0, i)),
                  pl.BlockSpec((half, tm), lambda i: (0, i))],
        out_specs=[pl.BlockSpec((MLA_HEADS, MLA_QK_PAD, tm), lambda i: (0, 0, i)),
                   pl.BlockSpec((MLA_HEADS, tm, MLA_QK_PAD), lambda i: (0, i, 0)),
                   pl.BlockSpec((MLA_HEADS, tm // tk, MLA_V, tk), lambda i: (0, i, 0, 0))],
        out_shape=[jax.ShapeDtypeStruct((MLA_HEADS, MLA_QK_PAD, s), BF),
                   jax.ShapeDtypeStruct((MLA_HEADS, s, MLA_QK_PAD), BF),
                   jax.ShapeDtypeStruct((MLA_HEADS, s // tk, MLA_V, tk), BF)],
        compiler_params=_cparams("arbitrary"),
        name="mla_up",
    )(cbt, gq, gkv, wq, wk, wv, gqn, gqp, gkn, gkp, cos_t, sin_t)


def _logits_phase(kb, nblk, qts, k_rows, tk):
    rows = pl.ds(pl.multiple_of(kb * tk, tk), nblk * tk)
    return [jnp.dot(k_rows(g, rows), qts[g], preferred_element_type=F32) for g in range(len(qts))]


def _pv_phase(kb, nblk, ps, vt_ref, tk):
    pvs = []
    for g, p in enumerate(ps):
        pv = jnp.dot(vt_ref[g, kb], p[0:tk], preferred_element_type=F32)
        for j in range(1, nblk):
            pv += jnp.dot(vt_ref[g, kb + j], p[j * tk:(j + 1) * tk], preferred_element_type=F32)
        pvs.append(pv)
    return pvs


def _softmax_block(kb, nblk, biases, qts, k_rows, vt_ref, m_ref, l_ref, acc_ref, tk):
    hg = len(qts)
    ss = _logits_phase(kb, nblk, qts, k_rows, tk)
    if biases is not None:
        ss = [s + b for s, b in zip(ss, biases)]
    alphas, ps = [], []
    for g in range(hg):
        m_old = m_ref[g]
        m_new = jnp.maximum(m_old, jnp.max(ss[g], axis=0, keepdims=True))
        alpha = jnp.exp2(m_old - m_new)
        p = jnp.exp2(ss[g] - m_new)
        l_ref[g] = alpha * l_ref[g] + jnp.sum(p, axis=0, keepdims=True)
        m_ref[g] = m_new
        alphas.append(alpha)
        ps.append(p.astype(BF))
    pvs = _pv_phase(kb, nblk, ps, vt_ref, tk)
    for g in range(hg):
        acc_ref[g] = alphas[g] * acc_ref[g] + pvs[g]


def _fixed_ref_block(kb, nblk, qts, k_rows, vt_ref, m_ref, l_ref, acc_ref, tk):
    hg = len(qts)
    ss = _logits_phase(kb, nblk, qts, k_rows, tk)
    ps = []
    for g in range(hg):
        p = jnp.exp2(ss[g] - m_ref[g])
        l_ref[g] += jnp.sum(p, axis=0, keepdims=True)
        ps.append(p.astype(BF))
    pvs = _pv_phase(kb, nblk, ps, vt_ref, tk)
    for g in range(hg):
        acc_ref[g] += pvs[g]


def _far_sweep(n_far, far_blocks, step):
    n_big = n_far // far_blocks

    def big(i, carry):
        step(i * far_blocks, far_blocks)
        return carry

    lax.fori_loop(0, n_big, big, 0)
    done = n_big * far_blocks
    size = far_blocks // 2
    while size >= 1:
        take = ((n_far - done) // size) * size

        @pl.when(take > 0)
        def _(done=done, size=size):
            step(done, size)

        done = done + take
        size //= 2


def _overflowed(l_ref, acc_ref):
    bad = jnp.maximum(jnp.max(jnp.where(jnp.isfinite(l_ref[...]), 0.0, 1.0)),
                      jnp.max(jnp.where(jnp.isfinite(acc_ref[...]), 0.0, 1.0)))
    return bad > 0.0


def _attention_sweep(n_far, far_blocks, near_step, far_std, far_fixed, m_ref, l_ref, acc_ref):
    _init_softmax_state(m_ref, l_ref, acc_ref)
    near_step()
    _far_sweep(n_far, far_blocks, far_fixed)

    @pl.when(_overflowed(l_ref, acc_ref))
    def _():
        _init_softmax_state(m_ref, l_ref, acc_ref)
        near_step()

        def one(kb, carry):
            far_std(kb)
            return carry

        lax.fori_loop(0, n_far, one, 0)


def _resident_spec(shape, index_map):
    return pl.BlockSpec(shape, index_map, pipeline_mode=pl.Buffered(1))


def _init_softmax_state(m_ref, l_ref, acc_ref):
    m_ref[...] = jnp.full(m_ref.shape, NEG_INF, F32)
    l_ref[...] = jnp.zeros(l_ref.shape, F32)
    acc_ref[...] = jnp.zeros(acc_ref.shape, F32)


def _diff_attn_kernel(q_ref, k_ref, vt_ref, bn_ref, lam_ref, g_ref, o_ref,
                      m_ref, l_ref, acc_ref, *, tq, tk, hg, lam_init):
    for sub in range(ATT_QSUB):
        _diff_query_block(pl.program_id(1) * ATT_QSUB + sub, slice(sub * tq, (sub + 1) * tq),
                          q_ref, k_ref, vt_ref, bn_ref, lam_ref, g_ref, o_ref, m_ref, l_ref, acc_ref,
                          tq=tq, tk=tk, hg=hg, lam_init=lam_init)


def _diff_query_block(qb, span, q_ref, k_ref, vt_ref, bn_ref, lam_ref, g_ref, o_ref,
                      m_ref, l_ref, acc_ref, *, tq, tk, hg, lam_init):
    qts = [jnp.concatenate([q_ref[2 * g, :, span], q_ref[2 * g + 1, :, span]], axis=1)
           for g in range(hg)]

    def k_rows(g, rows):
        return k_ref[rows, g * HEAD_W:(g + 1) * HEAD_W]

    state = (vt_ref, m_ref, l_ref, acc_ref, tk)
    first = (qb == 0).astype(jnp.int32)

    def near_step():
        _softmax_block(qb - 1 + first, 2, [bn_ref[g, first] for g in range(hg)], qts, k_rows, *state)

    _attention_sweep(
        jnp.maximum(qb - 1, 0), DIFF_FAR_BLOCKS, near_step,
        lambda kb: _softmax_block(kb, 1, None, qts, k_rows, *state),
        lambda kb, nblk: _fixed_ref_block(kb, nblk, qts, k_rows, *state),
        m_ref, l_ref, acc_ref)

    lp = lam_ref[...]
    lam = (jnp.exp(jnp.sum(lp[0:1] * lp[1:2], axis=1, keepdims=True))
           - jnp.exp(jnp.sum(lp[2:3] * lp[3:4], axis=1, keepdims=True)) + lam_init)
    for g in range(hg):
        o = acc_ref[g] / l_ref[g]
        d = o[:, :tq] - lam * o[:, tq:]
        ms = jnp.mean(d * d, axis=0, keepdims=True)
        d = d * lax.rsqrt(ms + EPS) * g_ref[...] * (1.0 - lam_init)
        o_ref[span, g * HEAD_W:(g + 1) * HEAD_W] = d.T.astype(o_ref.dtype)


def _diff_attn_call(qpad, k, vt4, bias_near, lam_p, sub_g, *, lam_init):
    s = k.shape[0]
    tq = tk = min(ATT_TQ, s)
    nkb = s // tk
    n = 2 * tq
    hg = DIFF_HEAD_GROUP
    return pl.pallas_call(
        functools.partial(_diff_attn_kernel, tq=tq, tk=tk, hg=hg, lam_init=lam_init),
        grid=(DA_HEADS // hg, s // (tq * ATT_QSUB)),
        in_specs=[pl.BlockSpec((2 * hg, HEAD_W, tq * ATT_QSUB), lambda h, i: (h, 0, i)),
                  _resident_spec((s, hg * HEAD_W), lambda h, i: (0, h)),
                  _resident_spec((hg, nkb, HEAD_W, tk), lambda h, i: (h, 0, 0, 0)),
                  _resident_spec((hg, 2, 2 * tk, n), lambda h, i: (h, 0, 0, 0)),
                  pl.BlockSpec((4, DA_QK), lambda h, i: (0, 0)),
                  pl.BlockSpec((HEAD_W, 1), lambda h, i: (0, 0))],
        out_specs=pl.BlockSpec((tq * ATT_QSUB, hg * HEAD_W), lambda h, i: (i, h)),
        out_shape=jax.ShapeDtypeStruct((s, DA_HEADS * HEAD_W), BF),
        scratch_shapes=[pltpu.VMEM((hg, 1, n), F32), pltpu.VMEM((hg, 1, n), F32),
                        pltpu.VMEM((hg, HEAD_W, n), F32)],
        compiler_params=_cparams("arbitrary", "arbitrary"),
        name="diff_attention",
    )(qpad, k, vt4, bias_near, lam_p, sub_g)


def _mla_attn_kernel(q_ref, k_ref, vt_ref, mask_ref, o_ref, m_ref, l_ref, acc_ref, *, tk, hg):
    for sub in range(ATT_QSUB):
        _mla_query_block(pl.program_id(1) * ATT_QSUB + sub, slice(sub * tk, (sub + 1) * tk),
                         q_ref, k_ref, vt_ref, mask_ref, o_ref, m_ref, l_ref, acc_ref, tk=tk, hg=hg)


def _mla_query_block(qb, span, q_ref, k_ref, vt_ref, mask_ref, o_ref, m_ref, l_ref, acc_ref, *, tk, hg):
    qts = [q_ref[g, :, span] for g in range(hg)]

    def k_rows(g, rows):
        return k_ref[g, rows, :]

    state = (vt_ref, m_ref, l_ref, acc_ref, tk)

    def near_step():
        _softmax_block(qb, 1, [mask_ref[...]] * hg, qts, k_rows, *state)

    _attention_sweep(
        qb, MLA_FAR_BLOCKS, near_step,
        lambda kb: _softmax_block(kb, 1, None, qts, k_rows, *state),
        lambda kb, nblk: _fixed_ref_block(kb, nblk, qts, k_rows, *state),
        m_ref, l_ref, acc_ref)
    for g in range(hg):
        o = acc_ref[g] / l_ref[g]
        o_ref[span, g * HEAD_W:(g + 1) * HEAD_W] = o.T.astype(o_ref.dtype)


def _mla_attn_call(qt, k, vt4, mask_diag):
    s = k.shape[1]
    tq = tk = min(ATT_TQ, s)
    nkb = s // tk
    hg = MLA_HEAD_GROUP
    return pl.pallas_call(
        functools.partial(_mla_attn_kernel, tk=tk, hg=hg),
        grid=(MLA_HEADS // hg, s // (tq * ATT_QSUB)),
        in_specs=[pl.BlockSpec((hg, MLA_QK_PAD, tq * ATT_QSUB), lambda h, i: (h, 0, i)),
                  _resident_spec((hg, s, MLA_QK_PAD), lambda h, i: (h, 0, 0)),
                  _resident_spec((hg, nkb, HEAD_W, tk), lambda h, i: (h, 0, 0, 0)),
                  pl.BlockSpec((tk, tq), lambda h, i: (0, 0))],
        out_specs=pl.BlockSpec((tq * ATT_QSUB, hg * HEAD_W), lambda h, i: (i, h)),
        out_shape=jax.ShapeDtypeStruct((s, MLA_HEADS * HEAD_W), BF),
        scratch_shapes=[pltpu.VMEM((hg, 1, tq), F32), pltpu.VMEM((hg, 1, tq), F32),
                        pltpu.VMEM((hg, HEAD_W, tq), F32)],
        compiler_params=_cparams("arbitrary", "arbitrary"),
        name="mla_attention",
    )(qt, k, vt4, mask_diag)


def _stick_blocks(blocks, qts, k_ref, vt_ref, tri_neg, carry_ref, acc_ref, tk):
    hg = len(qts)
    zs = []
    for kb, _ in blocks:
        rows = pl.ds(pl.multiple_of(kb * tk, tk), tk)
        zs.append([jnp.dot(k_ref[rows, g * HEAD_W:(g + 1) * HEAD_W], qts[g],
                           preferred_element_type=F32) for g in range(hg)])
    carries = [carry_ref[g] for g in range(hg)]
    logsigs, his, los = [], [], []
    for b, (_, mask) in enumerate(blocks):
        for g in range(hg):
            z = zs[b][g]
            sp = jnp.maximum(z, 0.0) + jnp.log2(1.0 + jnp.exp2(-jnp.abs(z)))
            logsigs.append(z - sp + carries[g])
            if mask is not None:
                sp = jnp.where(mask, sp, 0.0)
            hi = sp.astype(BF)
            his.append(hi)
            los.append((sp - hi.astype(F32)).astype(BF))
            carries[g] = carries[g] - jnp.sum(sp, axis=0, keepdims=True)
    betweens = [jnp.dot(tri_neg, hi, preferred_element_type=F32)
                + jnp.dot(tri_neg, lo, preferred_element_type=F32) for hi, lo in zip(his, los)]
    ws = []
    for b, (_, mask) in enumerate(blocks):
        for g in range(hg):
            w = jnp.exp2(logsigs[b * hg + g] + betweens[b * hg + g])
            if mask is not None:
                w = jnp.where(mask, w, 0.0)
            ws.append(w.astype(BF))
    for g in range(hg):
        pv = jnp.dot(vt_ref[g, blocks[0][0]], ws[g], preferred_element_type=F32)
        for b in range(1, len(blocks)):
            pv += jnp.dot(vt_ref[g, blocks[b][0]], ws[b * hg + g], preferred_element_type=F32)
        acc_ref[g] += pv
        carry_ref[g] = carries[g]


def _stick_kernel(q_ref, k_ref, vt_ref, tri_ref, o_ref, carry_ref, acc_ref, *, tq, tk, hg):
    for sub in range(ATT_QSUB):
        _stick_query_block(pl.program_id(1) * ATT_QSUB + sub, slice(sub * tq, (sub + 1) * tq),
                           q_ref, k_ref, vt_ref, tri_ref, o_ref, carry_ref, acc_ref,
                           tq=tq, tk=tk, hg=hg)


def _stick_query_block(qb, span, q_ref, k_ref, vt_ref, tri_ref, o_ref, carry_ref, acc_ref,
                       *, tq, tk, hg):
    qts = [q_ref[g * HEAD_W:(g + 1) * HEAD_W, span] for g in range(hg)]
    tri = tri_ref[...]
    carry_ref[...] = jnp.zeros(carry_ref.shape, F32)
    acc_ref[...] = jnp.zeros(acc_ref.shape, F32)
    kpos = lax.broadcasted_iota(jnp.int32, (tk, tq), 0)
    qpos = lax.broadcasted_iota(jnp.int32, (tk, tq), 1)
    causal = kpos < qpos
    state = (qts, k_ref, vt_ref, tri, carry_ref, acc_ref, tk)

    @pl.when(qb == 0)
    def _():
        _stick_blocks([(qb, causal)], *state)

    @pl.when(qb > 0)
    def _():
        _stick_blocks([(qb, causal), (qb - 1, None)], *state)

    def more(state_):
        i, cmax = state_
        return jnp.logical_and(i < qb, cmax > SB_LOG2W_UNDERFLOW)

    def earlier(state_):
        i, _ = state_
        _stick_blocks([(qb - 1 - i, None)], *state)
        return i + 1, jnp.max(carry_ref[...])

    lax.while_loop(more, earlier, (jnp.int32(1), jnp.max(carry_ref[...])))
    for g in range(hg):
        o_ref[span, g * HEAD_W:(g + 1) * HEAD_W] = acc_ref[g].T.astype(o_ref.dtype)


def _stick_call(qt, k, vt4, tri):
    s = k.shape[0]
    tq = tk = min(ATT_TQ, s)
    nkb = s // tk
    hg = SB_HEAD_GROUP
    return pl.pallas_call(
        functools.partial(_stick_kernel, tq=tq, tk=tk, hg=hg),
        grid=(SB_HEADS // hg, s // (tq * ATT_QSUB)),
        in_specs=[pl.BlockSpec((hg * HEAD_W, tq * ATT_QSUB), lambda h, i: (h, i)),
                  _resident_spec((s, hg * HEAD_W), lambda h, i: (0, h)),
                  _resident_spec((hg, nkb, HEAD_W, tk), lambda h, i: (h, 0, 0, 0)),
                  pl.BlockSpec((tk, tk), lambda h, i: (0, 0))],
        out_specs=pl.BlockSpec((tq * ATT_QSUB, hg * HEAD_W), lambda h, i: (i, h)),
        out_shape=jax.ShapeDtypeStruct((s, SB_HEADS * HEAD_W), BF),
        scratch_shapes=[pltpu.VMEM((hg, 1, tq), F32), pltpu.VMEM((hg, HEAD_W, tq), F32)],
        compiler_params=_cparams("arbitrary", "arbitrary"),
        name="stick_breaking",
    )(qt, k, vt4, tri)


def _merge_kernel(ht_ref, ya_ref, yb_ref, yc_ref, wga_ref, wgb_ref, wgc_ref, wb_ref, o_ref):
    ht = ht_ref[...]
    acc = None
    for n, (y_ref, wg_ref) in enumerate(((ya_ref, wga_ref), (yb_ref, wgb_ref), (yc_ref, wgc_ref))):
        gate = jax.nn.sigmoid(jnp.dot(wg_ref[...], ht, preferred_element_type=F32)).T
        up = jnp.dot(y_ref[...], wb_ref[n], preferred_element_type=F32)
        acc = gate * up if acc is None else acc + gate * up
    o_ref[...] = acc.astype(o_ref.dtype)


def _merge_call(ht, ya, yb, yc, wt_tail, wb, layer):
    d, s = ht.shape
    tm = min(1024, s)
    tn = 512
    ysp = pl.BlockSpec((tm, BRANCH_WIDTH), lambda i, j: (i, 0))

    def gate_spec(n):
        blk0 = (SEG_GATE * SEG + n * d) // tn
        return pl.BlockSpec((None, tn, d), lambda i, j: (layer, blk0 + j, 0))

    return pl.pallas_call(
        _merge_kernel,
        grid=(s // tm, d // tn),
        in_specs=[pl.BlockSpec((d, tm), lambda i, j: (0, i)), ysp, ysp, ysp,
                  gate_spec(0), gate_spec(1), gate_spec(2),
                  pl.BlockSpec((None, N_BRANCHES, BRANCH_WIDTH, tn), lambda i, j: (layer, 0, 0, j))],
        out_specs=pl.BlockSpec((tm, tn), lambda i, j: (i, j)),
        out_shape=jax.ShapeDtypeStruct((s, d), BF),
        compiler_params=_cparams("arbitrary", "arbitrary"),
        name="gated_merge",
    )(ht, ya, yb, yc, wt_tail, wt_tail, wt_tail, wb)


def _mlp_kernel(x_ref, ng_ref, sc_ref, sh_ref, w1_ref, w2_ref, g_ref, o_ref, acc_ref, h_ref):
    f = pl.program_id(1)

    @pl.when(f == 0)
    def _():
        acc_ref[...] = jnp.zeros(acc_ref.shape, F32)
        x = x_ref[...]
        y = x * lax.rsqrt(jnp.mean(x * x, axis=-1, keepdims=True) + EPS) * ng_ref[...]
        h_ref[...] = (y * (1.0 + sc_ref[...]) + sh_ref[...]).astype(BF)

    u = jnp.dot(h_ref[...], w1_ref[...], preferred_element_type=F32)
    u = jnp.square(jnp.maximum(u, 0.0)).astype(BF)
    acc_ref[...] += jnp.dot(u, w2_ref[...], preferred_element_type=F32)

    @pl.when(f == pl.num_programs(1) - 1)
    def _():
        o_ref[...] = x_ref[...] + g_ref[...] * acc_ref[...]


def _mlp_call(x, norm_g, sc, sh, w1, w2, layer, g):
    s, d = x.shape
    ff = w1.shape[2]
    tm = min(512, s)
    tf = 1024
    row = pl.BlockSpec((1, d), lambda i, f: (0, 0))
    return pl.pallas_call(
        _mlp_kernel,
        grid=(s // tm, ff // tf),
        in_specs=[pl.BlockSpec((tm, d), lambda i, f: (i, 0)), row, row, row,
                  pl.BlockSpec((None, d, tf), lambda i, f: (layer, 0, f)),
                  pl.BlockSpec((None, tf, d), lambda i, f: (layer, f, 0)),
                  row],
        out_specs=pl.BlockSpec((tm, d), lambda i, f: (i, 0)),
        out_shape=jax.ShapeDtypeStruct((s, d), F32),
        scratch_shapes=[pltpu.VMEM((tm, d), F32), pltpu.VMEM((tm, d), BF)],
        compiler_params=_cparams("arbitrary", "arbitrary"),
        name="sqrelu_mlp",
    )(x, norm_g, sc, sh, w1, w2, g)


def _t5_bucket(rel):
    nb = T5_BUCKETS // 2
    max_exact = nb // 2
    n = jnp.abs(rel)
    large = max_exact + (jnp.log(jnp.maximum(n, 1).astype(F32) / max_exact)
                         / math.log(T5_MAX_DIST / max_exact) * (nb - max_exact)).astype(jnp.int32)
    large = jnp.minimum(large, nb - 1)
    return jnp.where(rel > 0, nb, 0) + jnp.where(n < max_exact, n, large)


def _bias_tiles(t5_bias, tq, tk):
    table = t5_bias.astype(F32)

    def lookup(bucket):
        hit = bucket[..., None, None] == jnp.arange(T5_BUCKETS)[:, None]
        return jnp.sum(jnp.where(hit, table, 0.0), axis=-2)

    kl = jnp.arange(tk)[:, None]
    ql = jnp.arange(tq)[None, :]
    far = lookup(_t5_bucket(jnp.array(-(tk + tq), jnp.int32)))
    diag = jnp.transpose(lookup(_t5_bucket(kl - ql)) - far, (2, 0, 1))
    prev = jnp.transpose(lookup(_t5_bucket(kl - tk - ql)) - far, (2, 0, 1))
    allowed = (kl // CHUNK) <= (ql // CHUNK)
    diag = jnp.where(allowed[None], diag * LOG2E, NEG_INF)
    near = jnp.stack([jnp.concatenate([prev * LOG2E, diag], axis=1),
                      jnp.concatenate([diag, jnp.full_like(diag, NEG_INF)], axis=1)], axis=1)
    return jnp.concatenate([near, near], axis=3)


def _rope_tables(s):
    half = MLA_ROPE // 2
    inv = ROPE_BASE ** (-jnp.arange(half, dtype=F32) / half)
    ang = jnp.arange(s).astype(F32)[:, None] * inv[None, :]
    return jnp.cos(ang).T, jnp.sin(ang).T


def kernel(x, c, w_ada, b_ada, norm_mix_g, norm_mlp_g, w_in, diff_qk_g, diff_lambda, diff_subln_g,
           t5_bias, mla_q_norm_g, mla_kv_norm_g, w_q_up, w_kv_up, mla_qk_g, w_branch, w_out,
           w_mlp_in, w_mlp_out):
    b, s, d = x.shape
    assert b == 1 and d == D_MODEL
    depth = w_ada.shape[0]
    tq = tk = min(ATT_TQ, s)
    hw = MLA_NOPE + MLA_ROPE

    mod = _mod_call(jnp.broadcast_to(c, (8, d)).astype(BF), w_ada, b_ada[:, None, :])[:, 0, :]
    mod = mod.reshape(depth, N_MOD, 1, d)

    wt_all = jnp.swapaxes(w_in, 1, 2).astype(BF)
    wt_tail = wt_all[:, TAIL_ROW0:]
    w_br = w_branch.astype(BF)
    w_o = w_out.astype(BF)
    w_1 = w_mlp_in.astype(BF)
    w_2 = w_mlp_out.astype(BF)
    wt_qu = jnp.swapaxes(w_q_up, 1, 2).astype(BF)
    wkv = w_kv_up.reshape(depth, MLA_KV_LORA, MLA_HEADS, MLA_NOPE + MLA_V)
    wt_kn = jnp.swapaxes(wkv[..., :MLA_NOPE].reshape(depth, MLA_KV_LORA, -1), 1, 2).astype(BF)
    wt_v = jnp.swapaxes(wkv[..., MLA_NOPE:].reshape(depth, MLA_KV_LORA, -1), 1, 2).astype(BF)

    bias_near = _bias_tiles(t5_bias, tq, tk)
    kl = jnp.arange(tk)[:, None]
    ql = jnp.arange(tq)[None, :]
    mask_diag = jnp.where((kl // CHUNK) <= (ql // CHUNK), 0.0, NEG_INF).astype(F32)
    tri = -(jnp.arange(tk)[None, :] > jnp.arange(tk)[:, None]).astype(BF)
    cos_t, sin_t = _rope_tables(s)

    ones_col = jnp.ones((SEG, 1), F32)
    sb_scale = jnp.full((SB_HEADS * SB_DIM, 1), SB_DIM ** -0.5 * LOG2E, F32)
    da_scale = DA_QK ** -0.5 * LOG2E
    mla_scale = hw ** -0.5 * LOG2E

    x2 = x[0]
    for l in range(depth):
        sh1, sc1, g1, sh2, sc2, g2 = (mod[l, i] for i in range(N_MOD))
        lam_init = 0.8 - 0.6 * math.exp(-0.3 * l)

        ht = _norm_call(x2, norm_mix_g[l][None], sc1, sh1)

        gq_rows = jnp.tile(diff_qk_g[l, 0], 2 * DA_HEADS)[:, None] * da_scale
        gk_rows = jnp.tile(diff_qk_g[l, 1], 2 * DA_HEADS)[:, None]
        qa, ka, va, cbt = _proj_call(
            wt_all, l, SEG_DQ, ht, jnp.concatenate([gq_rows, gk_rows, ones_col, ones_col]),
            [(DA_QK, "qpad", BF), (DA_QK, "n", BF), (0, "t4", BF), (0, "t", F32)], name="proj_head")
        qc, kc, vc = _proj_call(
            wt_tail, l, SEG_SQ, ht, jnp.concatenate([sb_scale, ones_col, ones_col]),
            [(0, "t", BF), (0, "n", BF), (0, "t4", BF)], name="proj_tail")
        gqk = mla_qk_g[l]
        qb_t, kb, vb_t = _mla_up_call(
            cbt, l, mla_q_norm_g[l][:, None], mla_kv_norm_g[l][:, None], wt_qu, wt_kn, wt_v,
            gqk[0, :MLA_NOPE, None] * mla_scale, gqk[0, MLA_NOPE:, None] * mla_scale,
            gqk[1, :MLA_NOPE, None], gqk[1, MLA_NOPE:, None], cos_t, sin_t)

        ya = _diff_attn_call(qa, ka, va, bias_near, diff_lambda[l], diff_subln_g[l][:, None],
                             lam_init=lam_init)
        yb = _mla_attn_call(qb_t, kb, vb_t, mask_diag)
        yc = _stick_call(qc, kc, vc, tri)

        merged = _merge_call(ht, ya, yb, yc, wt_tail, w_br, l)
        x2 = _resid_proj_call(merged, w_o, l, x2, g1)

        x2 = _mlp_call(x2, norm_mlp_g[l][None], sc2, sh2, w_1, w_2, l, g2)
    return x2[None]
```

```python
import functools
import math

import jax
import jax.numpy as jnp
from jax import lax
from jax.experimental import pallas as pl
from jax.experimental.pallas import tpu as pltpu

BF = jnp.bfloat16
F32 = jnp.float32

D_MODEL = 2048
CHUNK = 64
DA_HEADS = 8
DA_QK = 64
DA_V = 128
MLA_HEADS = 8
MLA_Q_LORA = 512
MLA_KV_LORA = 256
MLA_NOPE = 128
MLA_ROPE = 64
MLA_V = 128
MLA_QK_PAD = 256
ROPE_BASE = 10000.0
SB_HEADS = 8
SB_DIM = 128
BRANCH_WIDTH = 1024
N_BRANCHES = 3
T5_BUCKETS = 32
T5_MAX_DIST = 128
D_FF = 4 * D_MODEL
N_MOD = 6
EPS = 1e-6
NEG_INF = -1e30
LOG2E = 1.4426950408889634

HEAD_W = 128
ATT_TQ = 256
ATT_TK = 256
ATT_QSUB = 2
DIFF_FAR_BLOCKS = 8
MLA_FAR_BLOCKS = 8
DIFF_HEAD_GROUP = 4
MLA_HEAD_GROUP = 4
SB_HEAD_GROUP = 8
SB_LOG2W_UNDERFLOW = -152.0
V7X_VMEM_LIMIT = 56 * 1024 * 1024

SEG = 1024
TAIL_ROW0 = 3 * SEG + MLA_Q_LORA + MLA_KV_LORA + MLA_ROPE
SEG_DQ, SEG_DK, SEG_DV, SEG_LAT = 0, 1, 2, 3
SEG_SQ, SEG_SK, SEG_SV, SEG_GATE = 0, 1, 2, 3
LAT_KPE = MLA_Q_LORA + MLA_KV_LORA


def _cparams(*sem):
    return pltpu.CompilerParams(dimension_semantics=sem, vmem_limit_bytes=V7X_VMEM_LIMIT)


def _mod_kernel(c_ref, w_ref, b_ref, o_ref):
    w = w_ref[0].astype(BF)
    o_ref[0] = jnp.dot(c_ref[...], w, preferred_element_type=F32) + b_ref[0]


def _mod_call(c8, w_ada, b_ada3):
    depth, d, n = w_ada.shape
    tn = 1024
    return pl.pallas_call(
        _mod_kernel,
        grid=(depth, n // tn),
        in_specs=[pl.BlockSpec((8, d), lambda l, j: (0, 0)),
                  pl.BlockSpec((1, d, tn), lambda l, j: (l, 0, j)),
                  pl.BlockSpec((1, 1, tn), lambda l, j: (l, 0, j))],
        out_specs=pl.BlockSpec((1, 8, tn), lambda l, j: (l, 0, j)),
        out_shape=jax.ShapeDtypeStruct((depth, 8, n), F32),
        compiler_params=_cparams("arbitrary", "arbitrary"),
        name="adaln_mod",
    )(c8, w_ada, b_ada3)


def _norm_kernel(x_ref, g_ref, sc_ref, sh_ref, ht_ref):
    x = x_ref[...]
    ms = jnp.mean(x * x, axis=-1, keepdims=True)
    y = x * lax.rsqrt(ms + EPS) * g_ref[...]
    ht_ref[...] = (y * (1.0 + sc_ref[...]) + sh_ref[...]).T.astype(BF)


def _norm_call(x, g, sc, sh):
    s, d = x.shape
    tm = min(512, s)
    row = pl.BlockSpec((1, d), lambda i: (0, 0))
    return pl.pallas_call(
        _norm_kernel,
        grid=(s // tm,),
        in_specs=[pl.BlockSpec((tm, d), lambda i: (i, 0)), row, row, row],
        out_specs=pl.BlockSpec((d, tm), lambda i: (0, i)),
        out_shape=jax.ShapeDtypeStruct((d, s), BF),
        compiler_params=_cparams("arbitrary"),
        name="adaln_rmsnorm",
    )(x, g, sc, sh)


def _proj_tile(w_ref, xt_ref, rs_ref, o_ref, group, layout, tk):
    y = jnp.dot(w_ref[...], xt_ref[...], preferred_element_type=F32)
    tn, tm = y.shape
    if group:
        y3 = y.reshape(tn // group, group, tm)
        ms = jnp.mean(y3 * y3, axis=1, keepdims=True)
        y = (y3 * lax.rsqrt(ms + EPS)).reshape(tn, tm)
    y = y * rs_ref[...]
    if layout == "t":
        o_ref[...] = y.astype(o_ref.dtype)
    elif layout == "n":
        o_ref[...] = y.T.astype(o_ref.dtype)
    elif layout == "t4":
        for a in range(tn // HEAD_W):
            for b in range(tm // tk):
                o_ref[a, b] = y[a * HEAD_W:(a + 1) * HEAD_W, b * tk:(b + 1) * tk].astype(o_ref.dtype)
    elif layout == "qpad":
        first = lax.broadcasted_iota(jnp.int32, (HEAD_W, tm), 0) < DA_QK
        for a in range(tn // HEAD_W):
            ya = y[a * HEAD_W:(a + 1) * HEAD_W]
            o_ref[2 * a] = jnp.where(first, ya, 0.0).astype(o_ref.dtype)
            o_ref[2 * a + 1] = jnp.where(first, 0.0, ya).astype(o_ref.dtype)


def _proj_kernel(w_ref, xt_ref, rs_ref, *o_refs, specs, tk):
    seg = pl.program_id(0)
    for k, (group, layout, _) in enumerate(specs):
        @pl.when(seg == k)
        def _(k=k, group=group, layout=layout):
            _proj_tile(w_ref, xt_ref, rs_ref, o_refs[k], group, layout, tk)


def _proj_call(wt, layer, seg0, ht, rowscale, specs, *, name):
    k, s = ht.shape
    nseg = len(specs)
    tn = SEG
    tm = min(1024, s)
    nj = s // tm
    tk = min(ATT_TK, s)
    out_shapes, out_specs = [], []
    for seg, (_, layout, out_dtype) in enumerate(specs):
        def tile(i, j, seg=seg):
            return jnp.where(i < seg, 0, jnp.where(i > seg, nj - 1, j))

        if layout == "t":
            out_shapes.append(jax.ShapeDtypeStruct((tn, s), out_dtype))
            out_specs.append(pl.BlockSpec((tn, tm), lambda i, j, t=tile: (0, t(i, j))))
        elif layout == "n":
            out_shapes.append(jax.ShapeDtypeStruct((s, tn), out_dtype))
            out_specs.append(pl.BlockSpec((tm, tn), lambda i, j, t=tile: (t(i, j), 0)))
        elif layout == "t4":
            out_shapes.append(jax.ShapeDtypeStruct((tn // HEAD_W, s // tk, HEAD_W, tk), out_dtype))
            out_specs.append(pl.BlockSpec((tn // HEAD_W, tm // tk, HEAD_W, tk),
                                          lambda i, j, t=tile: (0, t(i, j), 0, 0)))
        else:
            out_shapes.append(jax.ShapeDtypeStruct((2 * tn // HEAD_W, HEAD_W, s), out_dtype))
            out_specs.append(pl.BlockSpec((2 * tn // HEAD_W, HEAD_W, tm),
                                          lambda i, j, t=tile: (0, 0, t(i, j))))
    return pl.pallas_call(
        functools.partial(_proj_kernel, specs=tuple(specs), tk=tk),
        grid=(nseg, nj),
        in_specs=[pl.BlockSpec((None, tn, k), lambda i, j: (layer, seg0 + i, 0)),
                  pl.BlockSpec((k, tm), lambda i, j: (0, j)),
                  pl.BlockSpec((tn, 1), lambda i, j: (i, 0))],
        out_specs=out_specs,
        out_shape=out_shapes,
        compiler_params=_cparams("arbitrary", "arbitrary"),
        name=name,
    )(wt, ht, rowscale)


def _resid_proj_kernel(a_ref, w_ref, x_ref, g_ref, o_ref):
    y = jnp.dot(a_ref[...], w_ref[...], preferred_element_type=F32)
    o_ref[...] = x_ref[...] + g_ref[...] * y


def _resid_proj_call(a, w, layer, x, g):
    s, k = a.shape
    n = w.shape[2]
    tm = min(512, s)
    tn = n
    return pl.pallas_call(
        _resid_proj_kernel,
        grid=(s // tm, n // tn),
        in_specs=[pl.BlockSpec((tm, k), lambda i, j: (i, 0)),
                  pl.BlockSpec((None, k, tn), lambda i, j: (layer, 0, j)),
                  pl.BlockSpec((tm, tn), lambda i, j: (i, j)),
                  pl.BlockSpec((1, tn), lambda i, j: (0, j))],
        out_specs=pl.BlockSpec((tm, tn), lambda i, j: (i, j)),
        out_shape=jax.ShapeDtypeStruct((s, n), F32),
        compiler_params=_cparams("arbitrary", "arbitrary"),
        name="out_proj_residual",
    )(a, w, x, g)


def _rms_rows(x, g):
    return x * lax.rsqrt(jnp.mean(x * x, axis=0, keepdims=True) + EPS) * g


def _rope_rows(x, cos_t, sin_t):
    half = MLA_ROPE // 2
    x1, x2 = x[:half], x[half:]
    return x1 * cos_t - x2 * sin_t, x1 * sin_t + x2 * cos_t


def _mla_up_kernel(cb_ref, gq_ref, gkv_ref, wq_ref, wk_ref, wv_ref, gqn_ref, gqp_ref, gkn_ref,
                   gkp_ref, cost_ref, sint_ref, qt_ref, k_ref, vt_ref, *, tk):
    tm = cb_ref.shape[1]
    hw = MLA_NOPE + MLA_ROPE
    cos_t = cost_ref[...]
    sin_t = sint_ref[...]
    cqn = _rms_rows(cb_ref[0:MLA_Q_LORA], gq_ref[...]).astype(BF)
    qt = jnp.dot(wq_ref[...], cqn, preferred_element_type=F32)
    qzero = jnp.zeros((MLA_QK_PAD - hw, tm), BF)
    for h in range(MLA_HEADS):
        nope = _rms_rows(qt[h * hw:h * hw + MLA_NOPE], gqn_ref[...])
        pe1, pe2 = _rope_rows(_rms_rows(qt[h * hw + MLA_NOPE:(h + 1) * hw], gqp_ref[...]), cos_t, sin_t)
        qt_ref[h, 0:MLA_NOPE] = nope.astype(BF)
        qt_ref[h, MLA_NOPE:MLA_NOPE + MLA_ROPE // 2] = pe1.astype(BF)
        qt_ref[h, MLA_NOPE + MLA_ROPE // 2:hw] = pe2.astype(BF)
        qt_ref[h, hw:MLA_QK_PAD] = qzero
    ckvn = _rms_rows(cb_ref[MLA_Q_LORA:LAT_KPE], gkv_ref[...]).astype(BF)
    kt = jnp.dot(wk_ref[...], ckvn, preferred_element_type=F32)
    vt = jnp.dot(wv_ref[...], ckvn, preferred_element_type=F32)
    kpe1, kpe2 = _rope_rows(_rms_rows(cb_ref[LAT_KPE:LAT_KPE + MLA_ROPE], gkp_ref[...]), cos_t, sin_t)
    kzero = jnp.zeros((MLA_QK_PAD - hw, tm), F32)
    for h in range(MLA_HEADS):
        kn = _rms_rows(kt[h * MLA_NOPE:(h + 1) * MLA_NOPE], gkn_ref[...])
        k_ref[h] = jnp.concatenate([kn, kpe1, kpe2, kzero], axis=0).T.astype(BF)
        for b in range(tm // tk):
            vt_ref[h, b] = vt[h * MLA_V:(h + 1) * MLA_V, b * tk:(b + 1) * tk].astype(BF)


def _mla_up_call(cbt, layer, gq, gkv, wq, wk, wv, gqn, gqp, gkn, gkp, cos_t, sin_t):
    s = cbt.shape[1]
    tm = min(512, s)
    tk = min(ATT_TK, s)
    hw = MLA_NOPE + MLA_ROPE
    half = MLA_ROPE // 2

    def const(shape):
        return pl.BlockSpec(shape, lambda i: tuple(0 for _ in shape))

    def layer_w(shape):
        return pl.BlockSpec((None,) + shape, lambda i: (layer, 0, 0))

    return pl.pallas_call(
        functools.partial(_mla_up_kernel, tk=tk),
        grid=(s // tm,),
        in_specs=[pl.BlockSpec((cbt.shape[0], tm), lambda i: (0, i)),
                  const((MLA_Q_LORA, 1)), const((MLA_KV_LORA, 1)),
                  layer_w((MLA_HEADS * hw, MLA_Q_LORA)),
                  layer_w((MLA_HEADS * MLA_NOPE, MLA_KV_LORA)),
                  layer_w((MLA_HEADS * MLA_V, MLA_KV_LORA)),
                  const((MLA_NOPE, 1)), const((MLA_ROPE, 1)), const((MLA_NOPE, 1)),
                  const((MLA_ROPE, 1)),
                  pl.BlockSpec((half, tm), lambda i: (0, i)),
                  pl.BlockSpec((half, tm), lambda i: (0, i))],
        out_specs=[pl.BlockSpec((MLA_HEADS, MLA_QK_PAD, tm), lambda i: (0, 0, i)),
                   pl.BlockSpec((MLA_HEADS, tm, MLA_QK_PAD), lambda i: (0, i, 0)),
                   pl.BlockSpec((MLA_HEADS, tm // tk, MLA_V, tk), lambda i: (0, i, 0, 0))],
        out_shape=[jax.ShapeDtypeStruct((MLA_HEADS, MLA_QK_PAD, s), BF),
                   jax.ShapeDtypeStruct((MLA_HEADS, s, MLA_QK_PAD), BF),
                   jax.ShapeDtypeStruct((MLA_HEADS, s // tk, MLA_V, tk), BF)],
        compiler_params=_cparams("arbitrary"),
        name="mla_up",
    )(cbt, gq, gkv, wq, wk, wv, gqn, gqp, gkn, gkp, cos_t, sin_t)


def _logits_phase(kb, nblk, qts, k_rows, tk):
    rows = pl.ds(pl.multiple_of(kb * tk, tk), nblk * tk)
    return [jnp.dot(k_rows(g, rows), qts[g], preferred_element_type=F32) for g in range(len(qts))]


def _pv_phase(kb, nblk, ps, vt_ref, tk):
    pvs = []
    for g, p in enumerate(ps):
        pv = jnp.dot(vt_ref[g, kb], p[0:tk], preferred_element_type=F32)
        for j in range(1, nblk):
            pv += jnp.dot(vt_ref[g, kb + j], p[j * tk:(j + 1) * tk], preferred_element_type=F32)
        pvs.append(pv)
    return pvs


def _softmax_block(kb, nblk, biases, qts, k_rows, vt_ref, m_ref, l_ref, acc_ref, tk):
    hg = len(qts)
    ss = _logits_phase(kb, nblk, qts, k_rows, tk)
    if biases is not None:
        ss = [s + b for s, b in zip(ss, biases)]
    alphas, ps = [], []
    for g in range(hg):
        m_old = m_ref[g]
        m_new = jnp.maximum(m_old, jnp.max(ss[g], axis=0, keepdims=True))
        alpha = jnp.exp2(m_old - m_new)
        p = jnp.exp2(ss[g] - m_new)
        l_ref[g] = alpha * l_ref[g] + jnp.sum(p, axis=0, keepdims=True)
        m_ref[g] = m_new
        alphas.append(alpha)
        ps.append(p.astype(BF))
    pvs = _pv_phase(kb, nblk, ps, vt_ref, tk)
    for g in range(hg):
        acc_ref[g] = alphas[g] * acc_ref[g] + pvs[g]


def _fixed_ref_block(kb, nblk, qts, k_rows, vt_ref, m_ref, l_ref, acc_ref, tk):
    hg = len(qts)
    ss = _logits_phase(kb, nblk, qts, k_rows, tk)
    ps = []
    for g in range(hg):
        p = jnp.exp2(ss[g] - m_ref[g])
        l_ref[g] += jnp.sum(p, axis=0, keepdims=True)
        ps.append(p.astype(BF))
    pvs = _pv_phase(kb, nblk, ps, vt_ref, tk)
    for g in range(hg):
        acc_ref[g] += pvs[g]


def _far_sweep(n_far, far_blocks, step):
    n_big = n_far // far_blocks

    def big(i, carry):
        step(i * far_blocks, far_blocks)
        return carry

    lax.fori_loop(0, n_big, big, 0)
    done = n_big * far_blocks
    size = far_blocks // 2
    while size >= 1:
        take = ((n_far - done) // size) * size

        @pl.when(take > 0)
        def _(done=done, size=size):
            step(done, size)

        done = done + take
        size //= 2


def _overflowed(l_ref, acc_ref):
    bad = jnp.maximum(jnp.max(jnp.where(jnp.isfinite(l_ref[...]), 0.0, 1.0)),
                      jnp.max(jnp.where(jnp.isfinite(acc_ref[...]), 0.0, 1.0)))
    return bad > 0.0


def _attention_sweep(n_far, far_blocks, near_step, far_std, far_fixed, m_ref, l_ref, acc_ref):
    _init_softmax_state(m_ref, l_ref, acc_ref)
    near_step()
    _far_sweep(n_far, far_blocks, far_fixed)

    @pl.when(_overflowed(l_ref, acc_ref))
    def _():
        _init_softmax_state(m_ref, l_ref, acc_ref)
        near_step()

        def one(kb, carry):
            far_std(kb)
            return carry

        lax.fori_loop(0, n_far, one, 0)


def _resident_spec(shape, index_map):
    return pl.BlockSpec(shape, index_map, pipeline_mode=pl.Buffered(1))


def _init_softmax_state(m_ref, l_ref, acc_ref):
    m_ref[...] = jnp.full(m_ref.shape, NEG_INF, F32)
    l_ref[...] = jnp.zeros(l_ref.shape, F32)
    acc_ref[...] = jnp.zeros(acc_ref.shape, F32)


def _diff_attn_kernel(q_ref, k_ref, vt_ref, bn_ref, lam_ref, g_ref, o_ref,
                      m_ref, l_ref, acc_ref, *, tq, tk, hg, lam_init):
    for sub in range(ATT_QSUB):
        _diff_query_block(pl.program_id(1) * ATT_QSUB + sub, slice(sub * tq, (sub + 1) * tq),
                          q_ref, k_ref, vt_ref, bn_ref, lam_ref, g_ref, o_ref, m_ref, l_ref, acc_ref,
                          tq=tq, tk=tk, hg=hg, lam_init=lam_init)


def _diff_query_block(qb, span, q_ref, k_ref, vt_ref, bn_ref, lam_ref, g_ref, o_ref,
                      m_ref, l_ref, acc_ref, *, tq, tk, hg, lam_init):
    qts = [jnp.concatenate([q_ref[2 * g, :, span], q_ref[2 * g + 1, :, span]], axis=1)
           for g in range(hg)]

    def k_rows(g, rows):
        return k_ref[rows, g * HEAD_W:(g + 1) * HEAD_W]

    state = (vt_ref, m_ref, l_ref, acc_ref, tk)
    first = (qb == 0).astype(jnp.int32)

    def near_step():
        _softmax_block(qb - 1 + first, 2, [bn_ref[g, first] for g in range(hg)], qts, k_rows, *state)

    _attention_sweep(
        jnp.maximum(qb - 1, 0), DIFF_FAR_BLOCKS, near_step,
        lambda kb: _softmax_block(kb, 1, None, qts, k_rows, *state),
        lambda kb, nblk: _fixed_ref_block(kb, nblk, qts, k_rows, *state),
        m_ref, l_ref, acc_ref)

    lp = lam_ref[...]
    lam = (jnp.exp(jnp.sum(lp[0:1] * lp[1:2], axis=1, keepdims=True))
           - jnp.exp(jnp.sum(lp[2:3] * lp[3:4], axis=1, keepdims=True)) + lam_init)
    for g in range(hg):
        o = acc_ref[g] / l_ref[g]
        d = o[:, :tq] - lam * o[:, tq:]
        ms = jnp.mean(d * d, axis=0, keepdims=True)
        d = d * lax.rsqrt(ms + EPS) * g_ref[...] * (1.0 - lam_init)
        o_ref[span, g * HEAD_W:(g + 1) * HEAD_W] = d.T.astype(o_ref.dtype)


def _diff_attn_call(qpad, k, vt4, bias_near, lam_p, sub_g, *, lam_init):
    s = k.shape[0]
    tq = tk = min(ATT_TQ, s)
    nkb = s // tk
    n = 2 * tq
    hg = DIFF_HEAD_GROUP
    return pl.pallas_call(
        functools.partial(_diff_attn_kernel, tq=tq, tk=tk, hg=hg, lam_init=lam_init),
        grid=(DA_HEADS // hg, s // (tq * ATT_QSUB)),
        in_specs=[pl.BlockSpec((2 * hg, HEAD_W, tq * ATT_QSUB), lambda h, i: (h, 0, i)),
                  _resident_spec((s, hg * HEAD_W), lambda h, i: (0, h)),
                  _resident_spec((hg, nkb, HEAD_W, tk), lambda h, i: (h, 0, 0, 0)),
                  _resident_spec((hg, 2, 2 * tk, n), lambda h, i: (h, 0, 0, 0)),
                  pl.BlockSpec((4, DA_QK), lambda h, i: (0, 0)),
                  pl.BlockSpec((HEAD_W, 1), lambda h, i: (0, 0))],
        out_specs=pl.BlockSpec((tq * ATT_QSUB, hg * HEAD_W), lambda h, i: (i, h)),
        out_shape=jax.ShapeDtypeStruct((s, DA_HEADS * HEAD_W), BF),
        scratch_shapes=[pltpu.VMEM((hg, 1, n), F32), pltpu.VMEM((hg, 1, n), F32),
                        pltpu.VMEM((hg, HEAD_W, n), F32)],
        compiler_params=_cparams("arbitrary", "arbitrary"),
        name="diff_attention",
    )(qpad, k, vt4, bias_near, lam_p, sub_g)


def _mla_attn_kernel(q_ref, k_ref, vt_ref, mask_ref, o_ref, m_ref, l_ref, acc_ref, *, tk, hg):
    for sub in range(ATT_QSUB):
        _mla_query_block(pl.program_id(1) * ATT_QSUB + sub, slice(sub * tk, (sub + 1) * tk),
                         q_ref, k_ref, vt_ref, mask_ref, o_ref, m_ref, l_ref, acc_ref, tk=tk, hg=hg)


def _mla_query_block(qb, span, q_ref, k_ref, vt_ref, mask_ref, o_ref, m_ref, l_ref, acc_ref, *, tk, hg):
    qts = [q_ref[g, :, span] for g in range(hg)]

    def k_rows(g, rows):
        return k_ref[g, rows, :]

    state = (vt_ref, m_ref, l_ref, acc_ref, tk)

    def near_step():
        _softmax_block(qb, 1, [mask_ref[...]] * hg, qts, k_rows, *state)

    _attention_sweep(
        qb, MLA_FAR_BLOCKS, near_step,
        lambda kb: _softmax_block(kb, 1, None, qts, k_rows, *state),
        lambda kb, nblk: _fixed_ref_block(kb, nblk, qts, k_rows, *state),
        m_ref, l_ref, acc_ref)
    for g in range(hg):
        o = acc_ref[g] / l_ref[g]
        o_ref[span, g * HEAD_W:(g + 1) * HEAD_W] = o.T.astype(o_ref.dtype)


def _mla_attn_call(qt, k, vt4, mask_diag):
    s = k.shape[1]
    tq = tk = min(ATT_TQ, s)
    nkb = s // tk
    hg = MLA_HEAD_GROUP
    return pl.pallas_call(
        functools.partial(_mla_attn_kernel, tk=tk, hg=hg),
        grid=(MLA_HEADS // hg, s // (tq * ATT_QSUB)),
        in_specs=[pl.BlockSpec((hg, MLA_QK_PAD, tq * ATT_QSUB), lambda h, i: (h, 0, i)),
                  _resident_spec((hg, s, MLA_QK_PAD), lambda h, i: (h, 0, 0)),
                  _resident_spec((hg, nkb, HEAD_W, tk), lambda h, i: (h, 0, 0, 0)),
                  pl.BlockSpec((tk, tq), lambda h, i: (0, 0))],
        out_specs=pl.BlockSpec((tq * ATT_QSUB, hg * HEAD_W), lambda h, i: (i, h)),
        out_shape=jax.ShapeDtypeStruct((s, MLA_HEADS * HEAD_W), BF),
        scratch_shapes=[pltpu.VMEM((hg, 1, tq), F32), pltpu.VMEM((hg, 1, tq), F32),
                        pltpu.VMEM((hg, HEAD_W, tq), F32)],
        compiler_params=_cparams("arbitrary", "arbitrary"),
        name="mla_attention",
    )(qt, k, vt4, mask_diag)


def _stick_blocks(blocks, qts, k_ref, vt_ref, tri_neg, carry_ref, acc_ref, tk):
    hg = len(qts)
    zs = []
    for kb, _ in blocks:
        rows = pl.ds(pl.multiple_of(kb * tk, tk), tk)
        zs.append([jnp.dot(k_ref[rows, g * HEAD_W:(g + 1) * HEAD_W], qts[g],
                           preferred_element_type=F32) for g in range(hg)])
    carries = [carry_ref[g] for g in range(hg)]
    logsigs, his, los = [], [], []
    for b, (_, mask) in enumerate(blocks):
        for g in range(hg):
            z = zs[b][g]
            sp = jnp.maximum(z, 0.0) + jnp.log2(1.0 + jnp.exp2(-jnp.abs(z)))
            logsigs.append(z - sp + carries[g])
            if mask is not None:
                sp = jnp.where(mask, sp, 0.0)
            hi = sp.astype(BF)
            his.append(hi)
            los.append((sp - hi.astype(F32)).astype(BF))
            carries[g] = carries[g] - jnp.sum(sp, axis=0, keepdims=True)
    betweens = [jnp.dot(tri_neg, hi, preferred_element_type=F32)
                + jnp.dot(tri_neg, lo, preferred_element_type=F32) for hi, lo in zip(his, los)]
    ws = []
    for b, (_, mask) in enumerate(blocks):
        for g in range(hg):
            w = jnp.exp2(logsigs[b * hg + g] + betweens[b * hg + g])
            if mask is not None:
                w = jnp.where(mask, w, 0.0)
            ws.append(w.astype(BF))
    for g in range(hg):
        pv = jnp.dot(vt_ref[g, blocks[0][0]], ws[g], preferred_element_type=F32)
        for b in range(1, len(blocks)):
            pv += jnp.dot(vt_ref[g, blocks[b][0]], ws[b * hg + g], preferred_element_type=F32)
        acc_ref[g] += pv
        carry_ref[g] = carries[g]


def _stick_kernel(q_ref, k_ref, vt_ref, tri_ref, o_ref, carry_ref, acc_ref, *, tq, tk, hg):
    for sub in range(ATT_QSUB):
        _stick_query_block(pl.program_id(1) * ATT_QSUB + sub, slice(sub * tq, (sub + 1) * tq),
                           q_ref, k_ref, vt_ref, tri_ref, o_ref, carry_ref, acc_ref,
                           tq=tq, tk=tk, hg=hg)


def _stick_query_block(qb, span, q_ref, k_ref, vt_ref, tri_ref, o_ref, carry_ref, acc_ref,
                       *, tq, tk, hg):
    qts = [q_ref[g * HEAD_W:(g + 1) * HEAD_W, span] for g in range(hg)]
    tri = tri_ref[...]
    carry_ref[...] = jnp.zeros(carry_ref.shape, F32)
    acc_ref[...] = jnp.zeros(acc_ref.shape, F32)
    kpos = lax.broadcasted_iota(jnp.int32, (tk, tq), 0)
    qpos = lax.broadcasted_iota(jnp.int32, (tk, tq), 1)
    causal = kpos < qpos
    state = (qts, k_ref, vt_ref, tri, carry_ref, acc_ref, tk)

    @pl.when(qb == 0)
    def _():
        _stick_blocks([(qb, causal)], *state)

    @pl.when(qb > 0)
    def _():
        _stick_blocks([(qb, causal), (qb - 1, None)], *state)

    def more(state_):
        i, cmax = state_
        return jnp.logical_and(i < qb, cmax > SB_LOG2W_UNDERFLOW)

    def earlier(state_):
        i, _ = state_
        _stick_blocks([(qb - 1 - i, None)], *state)
        return i + 1, jnp.max(carry_ref[...])

    lax.while_loop(more, earlier, (jnp.int32(1), jnp.max(carry_ref[...])))
    for g in range(hg):
        o_ref[span, g * HEAD_W:(g + 1) * HEAD_W] = acc_ref[g].T.astype(o_ref.dtype)


def _stick_call(qt, k, vt4, tri):
    s = k.shape[0]
    tq = tk = min(ATT_TQ, s)
    nkb = s // tk
    hg = SB_HEAD_GROUP
    return pl.pallas_call(
        functools.partial(_stick_kernel, tq=tq, tk=tk, hg=hg),
        grid=(SB_HEADS // hg, s // (tq * ATT_QSUB)),
        in_specs=[pl.BlockSpec((hg * HEAD_W, tq * ATT_QSUB), lambda h, i: (h, i)),
                  _resident_spec((s, hg * HEAD_W), lambda h, i: (0, h)),
                  _resident_spec((hg, nkb, HEAD_W, tk), lambda h, i: (h, 0, 0, 0)),
                  pl.BlockSpec((tk, tk), lambda h, i: (0, 0))],
        out_specs=pl.BlockSpec((tq * ATT_QSUB, hg * HEAD_W), lambda h, i: (i, h)),
        out_shape=jax.ShapeDtypeStruct((s, SB_HEADS * HEAD_W), BF),
        scratch_shapes=[pltpu.VMEM((hg, 1, tq), F32), pltpu.VMEM((hg, HEAD_W, tq), F32)],
        compiler_params=_cparams("arbitrary", "arbitrary"),
        name="stick_breaking",
    )(qt, k, vt4, tri)


def _merge_kernel(ht_ref, ya_ref, yb_ref, yc_ref, wga_ref, wgb_ref, wgc_ref, wb_ref, o_ref):
    ht = ht_ref[...]
    acc = None
    for n, (y_ref, wg_ref) in enumerate(((ya_ref, wga_ref), (yb_ref, wgb_ref), (yc_ref, wgc_ref))):
        gate = jax.nn.sigmoid(jnp.dot(wg_ref[...], ht, preferred_element_type=F32)).T
        up = jnp.dot(y_ref[...], wb_ref[n], preferred_element_type=F32)
        acc = gate * up if acc is None else acc + gate * up
    o_ref[...] = acc.astype(o_ref.dtype)


def _merge_call(ht, ya, yb, yc, wt_tail, wb, layer):
    d, s = ht.shape
    tm = min(1024, s)
    tn = 512
    ysp = pl.BlockSpec((tm, BRANCH_WIDTH), lambda i, j: (i, 0))

    def gate_spec(n):
        blk0 = (SEG_GATE * SEG + n * d) // tn
        return pl.BlockSpec((None, tn, d), lambda i, j: (layer, blk0 + j, 0))

    return pl.pallas_call(
        _merge_kernel,
        grid=(s // tm, d // tn),
        in_specs=[pl.BlockSpec((d, tm), lambda i, j: (0, i)), ysp, ysp, ysp,
                  gate_spec(0), gate_spec(1), gate_spec(2),
                  pl.BlockSpec((None, N_BRANCHES, BRANCH_WIDTH, tn), lambda i, j: (layer, 0, 0, j))],
        out_specs=pl.BlockSpec((tm, tn), lambda i, j: (i, j)),
        out_shape=jax.ShapeDtypeStruct((s, d), BF),
        compiler_params=_cparams("arbitrary", "arbitrary"),
        name="gated_merge",
    )(ht, ya, yb, yc, wt_tail, wt_tail, wt_tail, wb)


def _mlp_kernel(x_ref, ng_ref, sc_ref, sh_ref, w1_ref, w2_ref, g_ref, o_ref, acc_ref, h_ref):
    f = pl.program_id(1)

    @pl.when(f == 0)
    def _():
        acc_ref[...] = jnp.zeros(acc_ref.shape, F32)
        x = x_ref[...]
        y = x * lax.rsqrt(jnp.mean(x * x, axis=-1, keepdims=True) + EPS) * ng_ref[...]
        h_ref[...] = (y * (1.0 + sc_ref[...]) + sh_ref[...]).astype(BF)

    u = jnp.dot(h_ref[...], w1_ref[...], preferred_element_type=F32)
    u = jnp.square(jnp.maximum(u, 0.0)).astype(BF)
    acc_ref[...] += jnp.dot(u, w2_ref[...], preferred_element_type=F32)

    @pl.when(f == pl.num_programs(1) - 1)
    def _():
        o_ref[...] = x_ref[...] + g_ref[...] * acc_ref[...]


def _mlp_call(x, norm_g, sc, sh, w1, w2, layer, g):
    s, d = x.shape
    ff = w1.shape[2]
    tm = min(512, s)
    tf = 1024
    row = pl.BlockSpec((1, d), lambda i, f: (0, 0))
    return pl.pallas_call(
        _mlp_kernel,
        grid=(s // tm, ff // tf),
        in_specs=[pl.BlockSpec((tm, d), lambda i, f: (i, 0)), row, row, row,
                  pl.BlockSpec((None, d, tf), lambda i, f: (layer, 0, f)),
                  pl.BlockSpec((None, tf, d), lambda i, f: (layer, f, 0)),
                  row],
        out_specs=pl.BlockSpec((tm, d), lambda i, f: (i, 0)),
        out_shape=jax.ShapeDtypeStruct((s, d), F32),
        scratch_shapes=[pltpu.VMEM((tm, d), F32), pltpu.VMEM((tm, d), BF)],
        compiler_params=_cparams("arbitrary", "arbitrary"),
        name="sqrelu_mlp",
    )(x, norm_g, sc, sh, w1, w2, g)


def _t5_bucket(rel):
    nb = T5_BUCKETS // 2
    max_exact = nb // 2
    n = jnp.abs(rel)
    large = max_exact + (jnp.log(jnp.maximum(n, 1).astype(F32) / max_exact)
                         / math.log(T5_MAX_DIST / max_exact) * (nb - max_exact)).astype(jnp.int32)
    large = jnp.minimum(large, nb - 1)
    return jnp.where(rel > 0, nb, 0) + jnp.where(n < max_exact, n, large)


def _bias_tiles(t5_bias, tq, tk):
    table = t5_bias.astype(F32)

    def lookup(bucket):
        hit = bucket[..., None, None] == jnp.arange(T5_BUCKETS)[:, None]
        return jnp.sum(jnp.where(hit, table, 0.0), axis=-2)

    kl = jnp.arange(tk)[:, None]
    ql = jnp.arange(tq)[None, :]
    far = lookup(_t5_bucket(jnp.array(-(tk + tq), jnp.int32)))
    diag = jnp.transpose(lookup(_t5_bucket(kl - ql)) - far, (2, 0, 1))
    prev = jnp.transpose(lookup(_t5_bucket(kl - tk - ql)) - far, (2, 0, 1))
    allowed = (kl // CHUNK) <= (ql // CHUNK)
    diag = jnp.where(allowed[None], diag * LOG2E, NEG_INF)
    near = jnp.stack([jnp.concatenate([prev * LOG2E, diag], axis=1),
                      jnp.concatenate([diag, jnp.full_like(diag, NEG_INF)], axis=1)], axis=1)
    return jnp.concatenate([near, near], axis=3)


def _rope_tables(s):
    half = MLA_ROPE // 2
    inv = ROPE_BASE ** (-jnp.arange(half, dtype=F32) / half)
    ang = jnp.arange(s).astype(F32)[:, None] * inv[None, :]
    return jnp.cos(ang).T, jnp.sin(ang).T


def kernel(x, c, w_ada, b_ada, norm_mix_g, norm_mlp_g, w_in, diff_qk_g, diff_lambda, diff_subln_g,
           t5_bias, mla_q_norm_g, mla_kv_norm_g, w_q_up, w_kv_up, mla_qk_g, w_branch, w_out,
           w_mlp_in, w_mlp_out):
    b, s, d = x.shape
    assert b == 1 and d == D_MODEL
    assert s % 1024 == 0, "token tiles are 512 / 1024 rows, query blocks come in pairs of 256"
    depth = w_ada.shape[0]
    tq = tk = min(ATT_TQ, s)
    hw = MLA_NOPE + MLA_ROPE

    mod = _mod_call(jnp.broadcast_to(c, (8, d)).astype(BF), w_ada, b_ada[:, None, :])[:, 0, :]
    mod = mod.reshape(depth, N_MOD, 1, d)

    wt_all = jnp.swapaxes(w_in, 1, 2).astype(BF)
    wt_tail = wt_all[:, TAIL_ROW0:]
    w_br = w_branch.astype(BF)
    w_o = w_out.astype(BF)
    w_1 = w_mlp_in.astype(BF)
    w_2 = w_mlp_out.astype(BF)
    wt_qu = jnp.swapaxes(w_q_up, 1, 2).astype(BF)
    wkv = w_kv_up.reshape(depth, MLA_KV_LORA, MLA_HEADS, MLA_NOPE + MLA_V)
    wt_kn = jnp.swapaxes(wkv[..., :MLA_NOPE].reshape(depth, MLA_KV_LORA, -1), 1, 2).astype(BF)
    wt_v = jnp.swapaxes(wkv[..., MLA_NOPE:].reshape(depth, MLA_KV_LORA, -1), 1, 2).astype(BF)

    bias_near = _bias_tiles(t5_bias, tq, tk)
    kl = jnp.arange(tk)[:, None]
    ql = jnp.arange(tq)[None, :]
    mask_diag = jnp.where((kl // CHUNK) <= (ql // CHUNK), 0.0, NEG_INF).astype(F32)
    tri = -(jnp.arange(tk)[None, :] > jnp.arange(tk)[:, None]).astype(BF)
    cos_t, sin_t = _rope_tables(s)

    ones_col = jnp.ones((SEG, 1), F32)
    sb_scale = jnp.full((SB_HEADS * SB_DIM, 1), SB_DIM ** -0.5 * LOG2E, F32)
    da_scale = DA_QK ** -0.5 * LOG2E
    mla_scale = hw ** -0.5 * LOG2E

    x2 = x[0]
    for l in range(depth):
        sh1, sc1, g1, sh2, sc2, g2 = (mod[l, i] for i in range(N_MOD))
        lam_init = 0.8 - 0.6 * math.exp(-0.3 * l)

        ht = _norm_call(x2, norm_mix_g[l][None], sc1, sh1)

        gq_rows = jnp.tile(diff_qk_g[l, 0], 2 * DA_HEADS)[:, None] * da_scale
        gk_rows = jnp.tile(diff_qk_g[l, 1], 2 * DA_HEADS)[:, None]
        qa, ka, va, cbt = _proj_call(
            wt_all, l, SEG_DQ, ht, jnp.concatenate([gq_rows, gk_rows, ones_col, ones_col]),
            [(DA_QK, "qpad", BF), (DA_QK, "n", BF), (0, "t4", BF), (0, "t", F32)], name="proj_head")
        qc, kc, vc = _proj_call(
            wt_tail, l, SEG_SQ, ht, jnp.concatenate([sb_scale, ones_col, ones_col]),
            [(0, "t", BF), (0, "n", BF), (0, "t4", BF)], name="proj_tail")
        gqk = mla_qk_g[l]
        qb_t, kb, vb_t = _mla_up_call(
            cbt, l, mla_q_norm_g[l][:, None], mla_kv_norm_g[l][:, None], wt_qu, wt_kn, wt_v,
            gqk[0, :MLA_NOPE, None] * mla_scale, gqk[0, MLA_NOPE:, None] * mla_scale,
            gqk[1, :MLA_NOPE, None], gqk[1, MLA_NOPE:, None], cos_t, sin_t)

        ya = _diff_attn_call(qa, ka, va, bias_near, diff_lambda[l], diff_subln_g[l][:, None],
                             lam_init=lam_init)
        yb = _mla_attn_call(qb_t, kb, vb_t, mask_diag)
        yc = _stick_call(qc, kc, vc, tri)

        merged = _merge_call(ht, ya, yb, yc, wt_tail, w_br, l)
        x2 = _resid_proj_call(merged, w_o, l, x2, g1)

        x2 = _mlp_call(x2, norm_mlp_g[l][None], sc2, sh2, w_1, w_2, l, g2)
    return x2[None]
```
